```python
import jax
import jax.numpy as jnp
from jax import lax
import numpy as np

D_MODEL = 1024
BATCH = 8
SEQ = 4096
DEPTH = 4

N_BRANCH = 4
ROPE_THETA = 500000.0
NORM_EPS = 1e-6
NEG_INF = -1e30
QBLOCK = 128
MAX_POS_OFFSET = 1024

A_GROUPS = 4
A_GROUP_DIM = 64
A_WIDTH = A_GROUPS * A_GROUP_DIM
A_CHUNK = 128
B_HEADS = 4
B_HEAD_DIM = 64
B_WIDTH = B_HEADS * B_HEAD_DIM
B_ROT_DIM = B_HEAD_DIM // 4
MOBA_BLOCK = 256
MOBA_TOPK = 3
MOBA_QCHUNK = 64
C_HEADS = 4
C_Q_RANK = 192
C_KV_RANK = 128
C_NOPE_DIM = 64
C_ROPE_DIM = 32
C_V_DIM = 64
C_QK_DIM = C_NOPE_DIM + C_ROPE_DIM
C_WIDTH = C_HEADS * C_V_DIM
D_WIDTH = 256
D_CONV = 31
FFN_DIM = 2816
N_EXPERTS = 8
TOP_K = 2
EXPERT_DIM = 1408
EXPERT_ROWS = 512
N_DENSE = (DEPTH + 1) // 2
N_MOE = DEPTH // 2

GATE_COLS = N_BRANCH * D_MODEL
A_COLS = 2 * A_WIDTH
B_COLS = 3 * B_WIDTH
C_COLS = C_Q_RANK + C_KV_RANK + C_ROPE_DIM
D_COLS = 2 * D_WIDTH
IN_COLS = GATE_COLS + A_COLS + B_COLS + C_COLS + D_COLS
IN_SPLITS = [GATE_COLS, GATE_COLS + A_COLS, GATE_COLS + A_COLS + B_COLS, GATE_COLS + A_COLS + B_COLS + C_COLS]

kernel_name = 'hybrid_gated_gmlp_moba_mla_conv_moe_trunk'


def rmsnorm(x, g):
    xf = x.astype(jnp.float32)
    y = xf * lax.rsqrt(jnp.mean(xf * xf, axis=-1, keepdims=True) + NORM_EPS)
    return (y * g.astype(jnp.float32)).astype(x.dtype)


def layernorm(x, g, b):
    xf = x.astype(jnp.float32)
    mu = jnp.mean(xf, axis=-1, keepdims=True)
    xc = xf - mu
    y = xc * lax.rsqrt(jnp.mean(xc * xc, axis=-1, keepdims=True) + NORM_EPS)
    return (y * g.astype(jnp.float32) + b.astype(jnp.float32)).astype(x.dtype)


def rope(x, pos):
    r = x.shape[-1]
    inv = jnp.float32(ROPE_THETA) ** (-jnp.arange(0, r, 2, dtype=jnp.float32) / r)
    ang = pos.astype(jnp.float32)[:, :, None, None] * inv
    cos, sin = jnp.cos(ang), jnp.sin(ang)
    xf = x.astype(jnp.float32)
    x1, x2 = xf[..., : r // 2], xf[..., r // 2:]
    return jnp.concatenate([x1 * cos - x2 * sin, x2 * cos + x1 * sin], axis=-1).astype(x.dtype)


def dense_causal_attention(q, k, v):
    _, S, _, dk = q.shape
    scale = dk ** -0.5
    outs = []
    for i in range(S // QBLOCK):
        lo, hi = i * QBLOCK, (i + 1) * QBLOCK
        s = jnp.einsum('bqhd,bkhd->bhqk', q[:, lo:hi], k[:, :hi]).astype(jnp.float32) * scale
        causal = jnp.arange(hi)[None, :] <= jnp.arange(lo, hi)[:, None]
        p = jax.nn.softmax(jnp.where(causal, s, NEG_INF), axis=-1).astype(v.dtype)
        outs.append(jnp.einsum('bhqk,bkhd->bqhd', p, v[:, :hi]))
    return jnp.concatenate(outs, axis=1)


def moba_attention(q, k, v):
    B, S, H, D = q.shape
    nb = -(-S // MOBA_BLOCK)
    sp = nb * MOBA_BLOCK
    pad = [(0, 0), (0, sp - S), (0, 0), (0, 0)]
    q = jnp.pad(q, pad).transpose(0, 2, 1, 3)
    k = jnp.pad(k, pad).transpose(0, 2, 1, 3)
    v = jnp.pad(v, pad).transpose(0, 2, 1, 3)
    kb = k.reshape(B, H, nb, MOBA_BLOCK, D)
    vb = v.reshape(B, H, nb, MOBA_BLOCK, D)
    k_mean = jnp.mean(kb.astype(jnp.float32), axis=3)
    q_block = jnp.arange(sp) // MOBA_BLOCK
    score = jnp.einsum('bhsd,bhnd->bhsn', q.astype(jnp.float32), k_mean)
    past = jnp.arange(nb)[None, :] < q_block[:, None]
    score = jnp.where(past, score, NEG_INF)
    n_sel = min(MOBA_TOPK, nb)
    _, sel = lax.top_k(score, n_sel)
    sel_ok = sel < q_block[None, None, :, None]
    bi = jnp.arange(B)[:, None, None, None]
    hi = jnp.arange(H)[None, :, None, None]
    scale = D ** -0.5

    def chunk(c):
        start = c * MOBA_QCHUNK
        qc = lax.dynamic_slice_in_dim(q, start, MOBA_QCHUNK, axis=2)
        idx = lax.dynamic_slice_in_dim(sel, start, MOBA_QCHUNK, axis=2)
        ok = lax.dynamic_slice_in_dim(sel_ok, start, MOBA_QCHUNK, axis=2)
        kg = kb[bi, hi, idx]
        vg = vb[bi, hi, idx]
        own = start // MOBA_BLOCK
        ko = lax.dynamic_index_in_dim(kb, own, axis=2, keepdims=False)
        vo = lax.dynamic_index_in_dim(vb, own, axis=2, keepdims=False)
        s_sel = jnp.einsum('bhqd,bhqnkd->bhqnk', qc, kg).astype(jnp.float32) * scale
        s_sel = jnp.where(ok[..., None], s_sel, NEG_INF).reshape(B, H, MOBA_QCHUNK, n_sel * MOBA_BLOCK)
        s_own = jnp.einsum('bhqd,bhkd->bhqk', qc, ko).astype(jnp.float32) * scale
        qpos = start + jnp.arange(MOBA_QCHUNK)[:, None]
        kpos = own * MOBA_BLOCK + jnp.arange(MOBA_BLOCK)[None, :]
        s_own = jnp.where(kpos <= qpos, s_own, NEG_INF)
        p = jax.nn.softmax(jnp.concatenate([s_sel, s_own], axis=-1), axis=-1).astype(v.dtype)
        p_sel = p[..., : n_sel * MOBA_BLOCK].reshape(B, H, MOBA_QCHUNK, n_sel, MOBA_BLOCK)
        p_own = p[..., n_sel * MOBA_BLOCK:]
        return (jnp.einsum('bhqnk,bhqnkd->bhqd', p_sel, vg)
                + jnp.einsum('bhqk,bhkd->bhqd', p_own, vo))

    out = lax.map(chunk, jnp.arange(sp // MOBA_QCHUNK))
    out = out.transpose(1, 0, 3, 2, 4).reshape(B, sp, H, D)
    return out[:, :S]


def chunked_sgu(u, v, ln_g, ln_b, w_s, b_s):
    B, S, _ = v.shape
    v = layernorm(v, ln_g, ln_b).reshape(B, S // A_CHUNK, A_CHUNK, A_GROUPS, A_GROUP_DIM)
    causal = jnp.tril(jnp.ones((A_CHUNK, A_CHUNK), w_s.dtype))
    z = jnp.einsum('gts,bcsgd->bctgd', w_s * causal, v) + b_s.T[None, None, :, :, None]
    return u * z.reshape(B, S, A_WIDTH)


def mla_attention(cq, ckv, kr, pos, cq_g, ckv_g, w_uq, w_ukv, qn_g, kn_g):
    B, S, _ = cq.shape
    q = (rmsnorm(cq, cq_g) @ w_uq).reshape(B, S, C_HEADS, C_QK_DIM)
    kv = (rmsnorm(ckv, ckv_g) @ w_ukv).reshape(B, S, C_HEADS, C_NOPE_DIM + C_V_DIM)
    k_nope, v = kv[..., :C_NOPE_DIM], kv[..., C_NOPE_DIM:]
    k_rope = jnp.broadcast_to(kr[:, :, None, :], (B, S, C_HEADS, C_ROPE_DIM))
    k = jnp.concatenate([k_nope, k_rope], axis=-1)
    q = rmsnorm(q, qn_g)
    k = rmsnorm(k, kn_g)
    q = jnp.concatenate([q[..., :C_NOPE_DIM], rope(q[..., C_NOPE_DIM:], pos)], axis=-1)
    k = jnp.concatenate([k[..., :C_NOPE_DIM], rope(k[..., C_NOPE_DIM:], pos)], axis=-1)
    return dense_causal_attention(q, k, v).reshape(B, S, C_WIDTH)


def conformer_conv(a, g, conv_w, conv_b, ln_g, ln_b):
    z = a * jax.nn.sigmoid(g)
    z = lax.conv_general_dilated(z, conv_w[:, None, :], window_strides=(1,),
                                 padding=[(D_CONV - 1, 0)],
                                 dimension_numbers=('NWC', 'WIO', 'NWC'),
                                 feature_group_count=D_WIDTH) + conv_b
    return jax.nn.silu(layernorm(z, ln_g, ln_b))


def hybrid_mixer(h, pos, w_in, a_ln_g, a_ln_b, a_ws, a_bs, b_qn, b_kn, c_cq_norm, c_ckv_norm,
                 c_w_uq, c_w_ukv, c_qn, c_kn, d_conv_w, d_conv_b, d_ln_g, d_ln_b, w_branch, w_out):
    B, S, _ = h.shape
    proj = h @ w_in
    gates, pa, pb, pc, pd = jnp.split(proj, IN_SPLITS, axis=-1)
    u, v = jnp.split(pa, 2, axis=-1)
    o_a = chunked_sgu(u, v, a_ln_g, a_ln_b, a_ws, a_bs)
    qkv = pb.reshape(B, S, 3, B_HEADS, B_HEAD_DIM)
    qb = rmsnorm(qkv[:, :, 0], b_qn)
    kb = rmsnorm(qkv[:, :, 1], b_kn)
    qb = jnp.concatenate([rope(qb[..., :B_ROT_DIM], pos), qb[..., B_ROT_DIM:]], axis=-1)
    kb = jnp.concatenate([rope(kb[..., :B_ROT_DIM], pos), kb[..., B_ROT_DIM:]], axis=-1)
    o_b = moba_attention(qb, kb, qkv[:, :, 2]).reshape(B, S, B_WIDTH)
    cq, ckv, kr = jnp.split(pc, [C_Q_RANK, C_Q_RANK + C_KV_RANK], axis=-1)
    o_c = mla_attention(cq, ckv, kr, pos, c_cq_norm, c_ckv_norm, c_w_uq, c_w_ukv, c_qn, c_kn)
    da, dg = jnp.split(pd, 2, axis=-1)
    o_d = conformer_conv(da, dg, d_conv_w, d_conv_b, d_ln_g, d_ln_b)
    y = sum(jax.nn.sigmoid(gates[..., n * D_MODEL:(n + 1) * D_MODEL]) * (o @ w_branch[n])
            for n, o in enumerate((o_a, o_b, o_c, o_d)))
    return y @ w_out


def swiglu(h, w_gate, w_up, w_down):
    return (jax.nn.silu(h @ w_gate) * (h @ w_up)) @ w_down


def moe_swiglu(h, w_router, b_router, w_gate, w_up, w_down):
    B, S, D = h.shape
    T = B * S
    TK = T * TOP_K
    xt = h.reshape(T, D)
    logits = (xt @ w_router).astype(jnp.float32) + b_router.astype(jnp.float32)
    top_logit, top_e = lax.top_k(logits, TOP_K)
    top_w = jax.nn.softmax(top_logit, axis=-1)
    flat_e = top_e.reshape(TK)
    flat_tok = jnp.repeat(jnp.arange(T, dtype=jnp.int32), TOP_K)
    flat_w = top_w.reshape(TK)
    order = jnp.argsort(flat_e)
    sorted_e = flat_e[order]
    counts = jnp.bincount(flat_e, length=N_EXPERTS)
    starts = jnp.cumsum(counts) - counts
    padded = (counts + EXPERT_ROWS - 1) // EXPERT_ROWS * EXPERT_ROWS
    pends = jnp.cumsum(padded)
    pstarts = pends - padded
    dest = pstarts[sorted_e] + jnp.arange(TK) - starts[sorted_e]
    n_blocks = -(-TK // EXPERT_ROWS) + N_EXPERTS
    rows = n_blocks * EXPERT_ROWS
    row_tok = jnp.zeros((rows,), jnp.int32).at[dest].set(flat_tok[order])
    row_w = jnp.zeros((rows,), jnp.float32).at[dest].set(flat_w[order])
    block_e = jnp.minimum(jnp.searchsorted(pends, jnp.arange(n_blocks) * EXPERT_ROWS, side='right'),
                          N_EXPERTS - 1)
    xs = xt[row_tok].reshape(n_blocks, EXPERT_ROWS, D)

    def expert_block(args):
        xb, e = args
        return (jax.nn.silu(xb @ w_gate[e]) * (xb @ w_up[e])) @ w_down[e]

    ys = lax.map(expert_block, (xs, block_e)).reshape(rows, D)
    out = jnp.zeros((T, D), h.dtype).at[row_tok].add(ys * row_w[:, None].astype(ys.dtype))
    return out.reshape(B, S, D)


def setup_inputs(seed: int = 0) -> dict:
    key = jax.random.key(seed)
    ks = jax.random.split(key, 32)
    f32 = jnp.float32

    def nrm(k, shape, fan_in):
        return jax.random.normal(k, shape, f32) * (fan_in ** -0.5)

    def gain(k, shape):
        return 1.0 + 0.02 * jax.random.normal(k, shape, f32)

    def small(k, shape, s=0.02):
        return s * jax.random.normal(k, shape, f32)

    x = jax.random.normal(ks[0], (BATCH, SEQ, D_MODEL), f32)
    offset = jax.random.randint(ks[1], (BATCH, 1), 0, MAX_POS_OFFSET, dtype=jnp.int32)
    positions = offset + jnp.arange(SEQ, dtype=jnp.int32)[None, :]
    return {
        'x': x,
        'positions': positions,
        'attn_norm': gain(ks[2], (DEPTH, D_MODEL)),
        'ffn_norm': gain(ks[3], (DEPTH, D_MODEL)),
        'w_in': nrm(ks[4], (DEPTH, D_MODEL, IN_COLS), D_MODEL),
        'a_ln_g': gain(ks[5], (DEPTH, A_WIDTH)),
        'a_ln_b': small(ks[6], (DEPTH, A_WIDTH)),
        'a_ws': nrm(ks[7], (DEPTH, A_GROUPS, A_CHUNK, A_CHUNK), A_CHUNK),
        'a_bs': gain(ks[8], (DEPTH, A_GROUPS, A_CHUNK)),
        'b_qn': gain(ks[9], (DEPTH, B_HEAD_DIM)),
        'b_kn': gain(ks[10], (DEPTH, B_HEAD_DIM)),
        'c_cq_norm': gain(ks[11], (DEPTH, C_Q_RANK)),
        'c_ckv_norm': gain(ks[12], (DEPTH, C_KV_RANK)),
        'c_w_uq': nrm(ks[13], (DEPTH, C_Q_RANK, C_HEADS * C_QK_DIM), C_Q_RANK),
        'c_w_ukv': nrm(ks[14], (DEPTH, C_KV_RANK, C_HEADS * (C_NOPE_DIM + C_V_DIM)), C_KV_RANK),
        'c_qn': gain(ks[15], (DEPTH, C_QK_DIM)),
        'c_kn': gain(ks[16], (DEPTH, C_QK_DIM)),
        'd_conv_w': nrm(ks[17], (DEPTH, D_CONV, D_WIDTH), D_CONV),
        'd_conv_b': small(ks[18], (DEPTH, D_WIDTH)),
        'd_ln_g': gain(ks[19], (DEPTH, D_WIDTH)),
        'd_ln_b': small(ks[20], (DEPTH, D_WIDTH)),
        'w_branch': nrm(ks[21], (DEPTH, N_BRANCH, A_WIDTH, D_MODEL), A_WIDTH),
        'w_out': nrm(ks[22], (DEPTH, D_MODEL, D_MODEL), D_MODEL),
        'ffn_w_gate': nrm(ks[23], (N_DENSE, D_MODEL, FFN_DIM), D_MODEL),
        'ffn_w_up': nrm(ks[24], (N_DENSE, D_MODEL, FFN_DIM), D_MODEL),
        'ffn_w_down': nrm(ks[25], (N_DENSE, FFN_DIM, D_MODEL), FFN_DIM),
        'moe_router': nrm(ks[26], (N_MOE, D_MODEL, N_EXPERTS), D_MODEL),
        'moe_router_b': small(ks[27], (N_MOE, N_EXPERTS), 0.01),
        'moe_w_gate': nrm(ks[28], (N_MOE, N_EXPERTS, D_MODEL, EXPERT_DIM), D_MODEL),
        'moe_w_up': nrm(ks[29], (N_MOE, N_EXPERTS, D_MODEL, EXPERT_DIM), D_MODEL),
        'moe_w_down': nrm(ks[30], (N_MOE, N_EXPERTS, EXPERT_DIM, D_MODEL), EXPERT_DIM),
    }


def reference(x, positions, attn_norm, ffn_norm, w_in, a_ln_g, a_ln_b, a_ws, a_bs, b_qn, b_kn,
              c_cq_norm, c_ckv_norm, c_w_uq, c_w_ukv, c_qn, c_kn, d_conv_w, d_conv_b, d_ln_g, d_ln_b,
              w_branch, w_out, ffn_w_gate, ffn_w_up, ffn_w_down, moe_router, moe_router_b,
              moe_w_gate, moe_w_up, moe_w_down):
    for l in range(DEPTH):
        h = rmsnorm(x, attn_norm[l])
        x = x + hybrid_mixer(h, positions, w_in[l], a_ln_g[l], a_ln_b[l], a_ws[l], a_bs[l],
                             b_qn[l], b_kn[l], c_cq_norm[l], c_ckv_norm[l], c_w_uq[l], c_w_ukv[l],
                             c_qn[l], c_kn[l], d_conv_w[l], d_conv_b[l], d_ln_g[l], d_ln_b[l],
                             w_branch[l], w_out[l])
        h = rmsnorm(x, ffn_norm[l])
        i = l // 2
        if l % 2 == 0:
            x = x + swiglu(h, ffn_w_gate[i], ffn_w_up[i], ffn_w_down[i])
        else:
            x = x + moe_swiglu(h, moe_router[i], moe_router_b[i], moe_w_gate[i], moe_w_up[i],
                               moe_w_down[i])
    return x
```

```python
import functools

import jax
import jax.numpy as jnp
from jax import lax
from jax.experimental import pallas as pl
from jax.experimental.pallas import tpu as pltpu

D_MODEL = 1024
DEPTH = 4
N_BRANCH = 4
ROPE_THETA = 500000.0
NORM_EPS = 1e-6
NEG_INF = -1e30

WIDTH = 256
A_GROUPS = 4
A_GROUP_DIM = 64
A_CHUNK = 128
B_HEADS = 4
B_HEAD_DIM = 64
B_ROT_DIM = 16
MOBA_BLOCK = 256
MOBA_TOPK = 3
C_HEADS = 4
C_Q_RANK = 192
C_KV_RANK = 128
C_NOPE_DIM = 64
C_ROPE_DIM = 32
C_V_DIM = 64
C_QK_DIM = C_NOPE_DIM + C_ROPE_DIM
C_HEAD_PAD = 128
D_CONV = 31
FFN_DIM = 2816
N_EXPERTS = 8
EXPERT_DIM = 1408
EXPERT_ROWS = 512

ATT_BLOCK = 256
LANES = 128
SUBLANES = 8
CONV_HALO = 32
CONV_ROWS = 32

COL_GATE = 0
COL_AU = 4096
COL_AV = 4352
COL_BQ = 4608
COL_BK = 4864
COL_BV = 5120
COL_CQ = 5376
COL_CKV = 5632
COL_KR = 5760
COL_DA = 5888
COL_DG = 6144
PROJ_COLS = 6400

VMEM_LIMIT = 56 * 1024 * 1024

bf16 = jnp.bfloat16
f32 = jnp.float32


def _cparams(*sem):
    return pltpu.CompilerParams(dimension_semantics=sem, vmem_limit_bytes=VMEM_LIMIT)


def _rms(x, n):
    return x * lax.rsqrt(jnp.sum(x * x, axis=-1, keepdims=True) * (1.0 / n) + NORM_EPS)


def _layernorm(x, g, b):
    mu = jnp.mean(x, axis=-1, keepdims=True)
    xc = x - mu
    return xc * lax.rsqrt(jnp.mean(xc * xc, axis=-1, keepdims=True) + NORM_EPS) * g + b


def _iota(shape, dim):
    return lax.broadcasted_iota(jnp.int32, shape, dim)


def _rope_table_kernel(pos_ref, inv_ref, eb_ref, ec_ref, cb_ref, sb_ref, cc_ref, sc_ref):
    ang = pos_ref[0].astype(f32) * inv_ref[...]
    c = jnp.cos(ang)
    s = jnp.sin(ang)
    hi = lax.Precision.HIGHEST
    cb_ref[0] = jnp.dot(c, eb_ref[...], precision=hi, preferred_element_type=f32)
    sb_ref[0] = jnp.dot(s, eb_ref[...], precision=hi, preferred_element_type=f32)
    cc_ref[0] = jnp.dot(c, ec_ref[...], precision=hi, preferred_element_type=f32)
    sc_ref[0] = jnp.dot(s, ec_ref[...], precision=hi, preferred_element_type=f32)


def _rope_tables(positions):
    B, S = positions.shape
    R = min(512, S)
    nb = B_ROT_DIM // 2
    nc = C_ROPE_DIM // 2
    inv_b = f32(ROPE_THETA) ** (-jnp.arange(0, B_ROT_DIM, 2, dtype=f32) / B_ROT_DIM)
    inv_c = f32(ROPE_THETA) ** (-jnp.arange(0, C_ROPE_DIM, 2, dtype=f32) / C_ROPE_DIM)
    inv = jnp.zeros((1, LANES), f32).at[0, :nb].set(inv_b).at[0, nb:nb + nc].set(inv_c)
    zero_lane = LANES - 1
    ch = jnp.arange(WIDTH)
    d = ch % B_HEAD_DIM
    src_b = jnp.where(d < B_ROT_DIM, d % nb, zero_lane)
    eb = (jnp.arange(LANES)[:, None] == src_b[None, :]).astype(f32)
    dc = jnp.arange(C_HEAD_PAD)
    in_rope = (dc >= C_NOPE_DIM) & (dc < C_QK_DIM)
    src_c = jnp.where(in_rope, nb + (dc - C_NOPE_DIM) % nc, zero_lane)
    ec = (jnp.arange(LANES)[:, None] == src_c[None, :]).astype(f32)
    pos3 = positions.reshape(B, S, 1)
    out = pl.pallas_call(
        _rope_table_kernel,
        grid=(B, S // R),
        in_specs=[
            pl.BlockSpec((1, R, 1), lambda b, r: (b, r, 0)),
            pl.BlockSpec((1, LANES), lambda b, r: (0, 0)),
            pl.BlockSpec((LANES, WIDTH), lambda b, r: (0, 0)),
            pl.BlockSpec((LANES, C_HEAD_PAD), lambda b, r: (0, 0)),
        ],
        out_specs=[
            pl.BlockSpec((1, R, WIDTH), lambda b, r: (b, r, 0)),
            pl.BlockSpec((1, R, WIDTH), lambda b, r: (b, r, 0)),
            pl.BlockSpec((1, R, C_HEAD_PAD), lambda b, r: (b, r, 0)),
            pl.BlockSpec((1, R, C_HEAD_PAD), lambda b, r: (b, r, 0)),
        ],
        out_shape=[
            jax.ShapeDtypeStruct((B, S, WIDTH), f32),
            jax.ShapeDtypeStruct((B, S, WIDTH), f32),
            jax.ShapeDtypeStruct((B, S, C_HEAD_PAD), f32),
            jax.ShapeDtypeStruct((B, S, C_HEAD_PAD), f32),
        ],
        compiler_params=_cparams("parallel", "parallel"),
        name="rope_tables",
    )(pos3, inv, eb, ec)
    return out


def _inproj_kernel(x_ref, g_ref, w_ref, o_ref, *, col_chunk):
    h = (_rms(x_ref[...], D_MODEL) * g_ref[...]).astype(bf16)
    for c in range(PROJ_COLS // col_chunk):
        sl = slice(c * col_chunk, (c + 1) * col_chunk)
        o_ref[:, sl] = jnp.dot(h, w_ref[:, sl], preferred_element_type=f32).astype(bf16)


def _inproj(x2, g, w):
    T = x2.shape[0]
    tm = min(512, T)
    return pl.pallas_call(
        functools.partial(_inproj_kernel, col_chunk=1280),
        grid=(T // tm,),
        in_specs=[
            pl.BlockSpec((tm, D_MODEL), lambda t: (t, 0)),
            pl.BlockSpec((1, D_MODEL), lambda t: (0, 0)),
            pl.BlockSpec((D_MODEL, PROJ_COLS), lambda t: (0, 0)),
        ],
        out_specs=pl.BlockSpec((tm, PROJ_COLS), lambda t: (t, 0)),
        out_shape=jax.ShapeDtypeStruct((T, PROJ_COLS), bf16),
        compiler_params=_cparams("parallel"),
        name="inproj",
    )(x2, g, w)


def _sgu_kernel(u_ref, v_ref, g_ref, b_ref, ws_ref, bst_ref, o_ref, *, chunks):
    row = _iota((A_CHUNK, A_CHUNK), 0)
    col = _iota((A_CHUNK, A_CHUNK), 1)
    causal = col <= row
    group = _iota((A_CHUNK, WIDTH), 1) // A_GROUP_DIM
    wm = [jnp.where(causal, ws_ref[g], 0.0).astype(bf16) for g in range(A_GROUPS)]
    bias = jnp.zeros((A_CHUNK, WIDTH), f32)
    for g in range(A_GROUPS):
        bias = jnp.where(group == g, bst_ref[:, g:g + 1], bias)
    for c in range(chunks):
        rows = slice(c * A_CHUNK, (c + 1) * A_CHUNK)
        vn = _layernorm(v_ref[rows, :].astype(f32), g_ref[...], b_ref[...]).astype(bf16)
        z = bias
        for g in range(A_GROUPS):
            zg = jnp.dot(wm[g], vn, preferred_element_type=f32)
            z = z + jnp.where(group == g, zg, 0.0)
        o_ref[rows, :] = (u_ref[rows, :].astype(f32) * z).astype(bf16)


def _sgu(proj, ln_g, ln_b, ws, bs_t):
    T = proj.shape[0]
    ta = min(512, T)
    cu, cv = COL_AU // WIDTH, COL_AV // WIDTH
    return pl.pallas_call(
        functools.partial(_sgu_kernel, chunks=ta // A_CHUNK),
        grid=(T // ta,),
        in_specs=[
            pl.BlockSpec((ta, WIDTH), lambda t: (t, cu)),
            pl.BlockSpec((ta, WIDTH), lambda t: (t, cv)),
            pl.BlockSpec((1, WIDTH), lambda t: (0, 0)),
            pl.BlockSpec((1, WIDTH), lambda t: (0, 0)),
            pl.BlockSpec((A_GROUPS, A_CHUNK, A_CHUNK), lambda t: (0, 0, 0)),
            pl.BlockSpec((A_CHUNK, A_GROUPS), lambda t: (0, 0)),
        ],
        out_specs=pl.BlockSpec((ta, WIDTH), lambda t: (t, 0)),
        out_shape=jax.ShapeDtypeStruct((T, WIDTH), bf16),
        compiler_params=_cparams("parallel"),
        name="sgu",
    )(proj, proj, ln_g, ln_b, ws, bs_t)


def _conv_kernel(a_ref, g_ref, w_ref, cb_ref, lg_ref, lb_ref, o_ref, z_scr, *, seq):
    z_scr[0:CONV_HALO, :] = jnp.zeros((CONV_HALO, WIDTH), f32)
    z_scr[CONV_HALO:CONV_HALO + seq, :] = (
        a_ref[...].astype(f32) * jax.nn.sigmoid(g_ref[...].astype(f32)))
    first = CONV_HALO - (D_CONV - 1)
    window = CONV_ROWS + CONV_HALO

    def chunk(c, carry):
        base = pl.multiple_of(c * CONV_ROWS, CONV_ROWS)
        win = z_scr[pl.ds(base, window), :]
        acc = jnp.zeros((CONV_ROWS, WIDTH), f32) + cb_ref[...]
        for r in range(SUBLANES):
            taps = [k for k in range(D_CONV) if (first + k) % SUBLANES == r]
            rolled = win if r == 0 else pltpu.roll(win, shift=window - r, axis=0)
            for k in taps:
                lo = first + k - r
                acc = acc + w_ref[k:k + 1, :] * rolled[lo:lo + CONV_ROWS, :]
        y = _layernorm(acc, lg_ref[...], lb_ref[...])
        o_ref[pl.ds(base, CONV_ROWS), :] = (y * jax.nn.sigmoid(y)).astype(bf16)
        return carry

    lax.fori_loop(0, seq // CONV_ROWS, chunk, 0)


def _conformer(proj, B, S, conv_w, conv_b, ln_g, ln_b):
    ca, cg = COL_DA // WIDTH, COL_DG // WIDTH
    return pl.pallas_call(
        functools.partial(_conv_kernel, seq=S),
        grid=(B,),
        in_specs=[
            pl.BlockSpec((S, WIDTH), lambda b: (b, ca)),
            pl.BlockSpec((S, WIDTH), lambda b: (b, cg)),
            pl.BlockSpec((D_CONV, WIDTH), lambda b: (0, 0)),
            pl.BlockSpec((1, WIDTH), lambda b: (0, 0)),
            pl.BlockSpec((1, WIDTH), lambda b: (0, 0)),
            pl.BlockSpec((1, WIDTH), lambda b: (0, 0)),
        ],
        out_specs=pl.BlockSpec((S, WIDTH), lambda b: (b, 0)),
        out_shape=jax.ShapeDtypeStruct((B * S, WIDTH), bf16),
        scratch_shapes=[pltpu.VMEM((CONV_HALO + S, WIDTH), f32)],
        compiler_params=_cparams("parallel"),
        name="conformer_conv",
    )(proj, proj, conv_w, conv_b, ln_g, ln_b)


def _moba_prep_kernel(q_ref, k_ref, v_ref, cos_ref, sin_ref, qg_ref, kg_ref, gsum_ref, rot_ref,
                      qo_ref, ko_ref, vt_ref, bias_ref, kmean_scr, *, nblk):
    c = pl.program_id(1)

    @pl.when(c == 0)
    def _():
        kmean_scr[...] = jnp.zeros(kmean_scr.shape, f32)

    cos = cos_ref[0]
    sin = sin_ref[0]

    def norm_rope(x, gain):
        ss = jnp.dot((x * x).astype(bf16), gsum_ref[...], preferred_element_type=f32)
        xn = x * lax.rsqrt(ss + NORM_EPS) * gain
        xr = jnp.dot(xn.astype(bf16), rot_ref[...], preferred_element_type=f32)
        return xn * cos + xr * sin

    q = norm_rope(q_ref[...].astype(f32), qg_ref[...])
    k = norm_rope(k_ref[...].astype(f32), kg_ref[...])
    qo_ref[...] = (q * (B_HEAD_DIM ** -0.5)).astype(bf16)
    ko_ref[0, 0] = k.astype(bf16)
    vt_ref[0, 0] = v_ref[...].astype(f32).T.astype(bf16)

    head = _iota((nblk, WIDTH), 1) // B_HEAD_DIM
    blk = _iota((nblk, ATT_BLOCK), 0)
    past = blk < c
    kmean = kmean_scr[...]
    for h in range(B_HEADS):
        km = jnp.where(head == h, kmean, 0.0)
        sc = lax.dot_general(km, q, (((1,), (1,)), ((), ())), precision=lax.Precision.HIGHEST,
                             preferred_element_type=f32)
        sc = jnp.where(past, sc, NEG_INF)
        rank = jnp.zeros((nblk, ATT_BLOCK), jnp.int32)
        for j in range(nblk):
            other = sc[j:j + 1, :]
            ahead = (other > sc) | ((other == sc) & (j < blk))
            rank = rank + ahead.astype(jnp.int32)
        keep = past & (rank < MOBA_TOPK)
        bias_ref[0, h * nblk:(h + 1) * nblk, :] = jnp.where(keep, 0.0, NEG_INF)

    kmean_scr[pl.ds(c, 1), :] = jnp.mean(k, axis=0, keepdims=True)


def _moba_prep(proj, B, S, cos_b, sin_b, qg, kg, gsum, rot):
    nblk = S // ATT_BLOCK
    cq, ck, cv = COL_BQ // WIDTH, COL_BK // WIDTH, COL_BV // WIDTH
    tile = lambda col: pl.BlockSpec((ATT_BLOCK, WIDTH), lambda b, c: (b * nblk + c, col))
    table = pl.BlockSpec((1, ATT_BLOCK, WIDTH), lambda b, c: (b, c, 0))
    const = lambda shape: pl.BlockSpec(shape, lambda b, c: (0,) * len(shape))
    return pl.pallas_call(
        functools.partial(_moba_prep_kernel, nblk=nblk),
        grid=(B, nblk),
        in_specs=[tile(cq), tile(ck), tile(cv), table, table,
                  const((1, WIDTH)), const((1, WIDTH)), const((WIDTH, WIDTH)), const((WIDTH, WIDTH))],
        out_specs=[
            pl.BlockSpec((ATT_BLOCK, WIDTH), lambda b, c: (b * nblk + c, 0)),
            pl.BlockSpec((1, 1, ATT_BLOCK, WIDTH), lambda b, c: (b, c, 0, 0)),
            pl.BlockSpec((1, 1, WIDTH, ATT_BLOCK), lambda b, c: (b, c, 0, 0)),
            pl.BlockSpec((1, B_HEADS * nblk, ATT_BLOCK), lambda b, c: (b, 0, c)),
        ],
        out_shape=[
            jax.ShapeDtypeStruct((B * S, WIDTH), bf16),
            jax.ShapeDtypeStruct((B, nblk, ATT_BLOCK, WIDTH), bf16),
            jax.ShapeDtypeStruct((B, nblk, WIDTH, ATT_BLOCK), bf16),
            jax.ShapeDtypeStruct((B, B_HEADS * nblk, S), f32),
        ],
        scratch_shapes=[pltpu.VMEM((nblk, WIDTH), f32)],
        compiler_params=_cparams("parallel", "arbitrary"),
        name="moba_prep",
    )(proj, proj, proj, cos_b, sin_b, qg, kg, gsum, rot)


def _mla_prep_kernel(cq_ref, ckv_ref, kr_ref, cos_ref, sin_ref, cqg_ref, ckvg_ref, wuq_ref, wuk_ref,
                     wuv_ref, qg_ref, kg_ref, gsum_ref, rot_ref, qo_ref, ko_ref, vt_ref):
    cqn = (_rms(cq_ref[...].astype(f32), C_Q_RANK) * cqg_ref[...]).astype(bf16)
    ckvn = (_rms(ckv_ref[...].astype(f32), C_KV_RANK) * ckvg_ref[...]).astype(bf16)
    q = jnp.dot(cqn, wuq_ref[...], preferred_element_type=f32)
    k = jnp.dot(ckvn, wuk_ref[...], preferred_element_type=f32)
    v = jnp.dot(ckvn, wuv_ref[...], preferred_element_type=f32)
    kr = kr_ref[...].astype(f32)
    cos = cos_ref[0]
    sin = sin_ref[0]

    def norm_rope(x, gain):
        ss = jnp.dot((x * x).astype(bf16), gsum_ref[...], preferred_element_type=f32)
        xn = x * lax.rsqrt(ss + NORM_EPS) * gain
        xr = jnp.dot(xn.astype(bf16), rot_ref[...], preferred_element_type=f32)
        return xn * cos + xr * sin

    for h in range(C_HEADS):
        sl = slice(h * C_HEAD_PAD, (h + 1) * C_HEAD_PAD)
        qh = norm_rope(q[:, sl], qg_ref[...])
        kh = norm_rope(k[:, sl] + kr, kg_ref[...])
        qo_ref[:, sl] = (qh * (C_QK_DIM ** -0.5)).astype(bf16)
        ko_ref[0, 0, :, sl] = kh.astype(bf16)
    vt_ref[0, 0] = v.T.astype(bf16)


def _mla_prep(proj, B, S, cos_c, sin_c, cqg, ckvg, wuq, wuk, wuv, qg, kg, gsum, rot):
    nblk = S // ATT_BLOCK
    HD = C_HEADS * C_HEAD_PAD
    table = pl.BlockSpec((1, ATT_BLOCK, C_HEAD_PAD), lambda b, c: (b, c, 0))
    const = lambda shape: pl.BlockSpec(shape, lambda b, c: (0,) * len(shape))
    return pl.pallas_call(
        _mla_prep_kernel,
        grid=(B, nblk),
        in_specs=[
            pl.BlockSpec((ATT_BLOCK, WIDTH), lambda b, c: (b * nblk + c, COL_CQ // WIDTH)),
            pl.BlockSpec((ATT_BLOCK, LANES), lambda b, c: (b * nblk + c, COL_CKV // LANES)),
            pl.BlockSpec((ATT_BLOCK, LANES), lambda b, c: (b * nblk + c, COL_KR // LANES)),
            table, table,
            const((1, WIDTH)), const((1, C_KV_RANK)), const((WIDTH, HD)), const((C_KV_RANK, HD)),
            const((C_KV_RANK, WIDTH)), const((1, C_HEAD_PAD)), const((1, C_HEAD_PAD)),
            const((C_HEAD_PAD, C_HEAD_PAD)), const((C_HEAD_PAD, C_HEAD_PAD)),
        ],
        out_specs=[
            pl.BlockSpec((ATT_BLOCK, HD), lambda b, c: (b * nblk + c, 0)),
            pl.BlockSpec((1, 1, ATT_BLOCK, HD), lambda b, c: (b, c, 0, 0)),
            pl.BlockSpec((1, 1, WIDTH, ATT_BLOCK), lambda b, c: (b, c, 0, 0)),
        ],
        out_shape=[
            jax.ShapeDtypeStruct((B * S, HD), bf16),
            jax.ShapeDtypeStruct((B, nblk, ATT_BLOCK, HD), bf16),
            jax.ShapeDtypeStruct((B, nblk, WIDTH, ATT_BLOCK), bf16),
        ],
        compiler_params=_cparams("parallel", "parallel"),
        name="mla_prep",
    )(proj, proj, proj, cos_c, sin_c, cqg, ckvg, wuq, wuk, wuv, qg, kg, gsum, rot)


def _attn_kernel(*refs, heads, head_lanes, head_dim_v, nblk, masked):
    if masked:
        q_ref, k_ref, vt_ref, bias_ref, o_ref, out_scr = refs
    else:
        q_ref, k_ref, vt_ref, o_ref, out_scr = refs
        bias_ref = None
    i = pl.program_id(1)
    hd_all = q_ref.shape[-1]
    split_lanes = head_lanes % LANES == 0
    kpos = _iota((ATT_BLOCK, ATT_BLOCK), 0)
    qpos = _iota((ATT_BLOCK, ATT_BLOCK), 1)
    causal = kpos <= qpos
    nt = (((1,), (1,)), ((), ()))

    for h in range(heads):
        if split_lanes:
            lanes = slice(h * head_lanes, (h + 1) * head_lanes)
            qh = q_ref[:, lanes]
            kblock = lambda j: k_ref[0, j, :, lanes]
        else:
            lane_head = _iota((ATT_BLOCK, hd_all), 1) // head_lanes
            qh = jnp.where(lane_head == h, q_ref[...], jnp.zeros((), bf16))
            kblock = lambda j: k_ref[0, j]
        vrows = slice(h * head_dim_v, (h + 1) * head_dim_v)

        s = lax.dot_general(kblock(i), qh, nt, preferred_element_type=f32)
        s = jnp.where(causal, s, NEG_INF)
        m0 = jnp.max(s, axis=0, keepdims=True)
        p = jnp.exp(s - m0)
        l0 = jnp.sum(p, axis=0, keepdims=True)
        acc0 = jnp.dot(vt_ref[0, i, vrows, :], p.astype(bf16), preferred_element_type=f32)

        def past_block(j, carry):
            m, l, acc = carry
            s = lax.dot_general(kblock(j), qh, nt, preferred_element_type=f32)
            if masked:
                s = s + bias_ref[0, pl.ds(h * nblk + j, 1), :]
            m_new = jnp.maximum(m, jnp.max(s, axis=0, keepdims=True))
            alpha = jnp.exp(m - m_new)
            p = jnp.exp(s - m_new)
            l = alpha * l + jnp.sum(p, axis=0, keepdims=True)
            acc = alpha * acc + jnp.dot(vt_ref[0, j, vrows, :], p.astype(bf16),
                                        preferred_element_type=f32)
            return m_new, l, acc

        m, l, acc = lax.fori_loop(0, i, past_block, (m0, l0, acc0))
        out_scr[vrows, :] = acc / l

    o_ref[...] = out_scr[...].T.astype(bf16)


def _attention(q, k, vt, bias, B, S, head_lanes):
    nblk = S // ATT_BLOCK
    HD = q.shape[-1]
    masked = bias is not None
    in_specs = [
        pl.BlockSpec((ATT_BLOCK, HD), lambda b, i: (b * nblk + i, 0)),
        pl.BlockSpec((1, nblk, ATT_BLOCK, HD), lambda b, i: (b, 0, 0, 0)),
        pl.BlockSpec((1, nblk, WIDTH, ATT_BLOCK), lambda b, i: (b, 0, 0, 0)),
    ]
    args = [q, k, vt]
    if masked:
        in_specs.append(pl.BlockSpec((1, bias.shape[1], ATT_BLOCK), lambda b, i: (b, 0, i)))
        args.append(bias)
    return pl.pallas_call(
        functools.partial(_attn_kernel, heads=4, head_lanes=head_lanes, head_dim_v=64,
                          nblk=nblk, masked=masked),
        grid=(B, nblk),
        in_specs=in_specs,
        out_specs=pl.BlockSpec((ATT_BLOCK, WIDTH), lambda b, i: (b * nblk + i, 0)),
        out_shape=jax.ShapeDtypeStruct((B * S, WIDTH), bf16),
        scratch_shapes=[pltpu.VMEM((WIDTH, ATT_BLOCK), f32)],
        compiler_params=_cparams("parallel", "arbitrary"),
        name="moba_attention" if masked else "mla_attention",
    )(*args)


def _merge_kernel(x_ref, g0_ref, g1_ref, g2_ref, g3_ref, oa_ref, ob_ref, oc_ref, od_ref,
                  wb_ref, wo_ref, ng_ref, xo_ref, ho_ref):
    y = None
    for n, (g_ref, o_ref) in enumerate(((g0_ref, oa_ref), (g1_ref, ob_ref),
                                        (g2_ref, oc_ref), (g3_ref, od_ref))):
        br = jnp.dot(o_ref[...], wb_ref[n], preferred_element_type=f32)
        t = jax.nn.sigmoid(g_ref[...].astype(f32)) * br
        y = t if y is None else y + t
    x = x_ref[...] + jnp.dot(y.astype(bf16), wo_ref[...], preferred_element_type=f32)
    xo_ref[...] = x
    ho_ref[...] = (_rms(x, D_MODEL) * ng_ref[...]).astype(bf16)


def _merge(x2, proj, oa, ob, oc, od, wb, wo, ng):
    T = x2.shape[0]
    tm = min(512, T)
    gate = lambda n: pl.BlockSpec((tm, D_MODEL), lambda t: (t, n))
    branch = pl.BlockSpec((tm, WIDTH), lambda t: (t, 0))
    return pl.pallas_call(
        _merge_kernel,
        grid=(T // tm,),
        in_specs=[
            pl.BlockSpec((tm, D_MODEL), lambda t: (t, 0)),
            gate(0), gate(1), gate(2), gate(3), branch, branch, branch, branch,
            pl.BlockSpec((N_BRANCH, WIDTH, D_MODEL), lambda t: (0, 0, 0)),
            pl.BlockSpec((D_MODEL, D_MODEL), lambda t: (0, 0)),
            pl.BlockSpec((1, D_MODEL), lambda t: (0, 0)),
        ],
        out_specs=[
            pl.BlockSpec((tm, D_MODEL), lambda t: (t, 0)),
            pl.BlockSpec((tm, D_MODEL), lambda t: (t, 0)),
        ],
        out_shape=[
            jax.ShapeDtypeStruct((T, D_MODEL), f32),
            jax.ShapeDtypeStruct((T, D_MODEL), bf16),
        ],
        compiler_params=_cparams("parallel"),
        name="merge",
    )(x2, proj, proj, proj, proj, oa, ob, oc, od, wb, wo, ng)


def _swiglu_block(h, wg, wu, wd):
    a = jnp.dot(h, wg, preferred_element_type=f32)
    b = jnp.dot(h, wu, preferred_element_type=f32)
    t = (a * jax.nn.sigmoid(a) * b).astype(bf16)
    return jnp.dot(t, wd, preferred_element_type=f32)


def _ffn_kernel(x_ref, h_ref, wg_ref, wu_ref, wd_ref, o_ref, acc_ref):
    f = pl.program_id(1)
    y = _swiglu_block(h_ref[...], wg_ref[...], wu_ref[...], wd_ref[...])

    @pl.when(f == 0)
    def _():
        acc_ref[...] = x_ref[...] + y

    @pl.when(f > 0)
    def _():
        acc_ref[...] += y

    @pl.when(f == pl.num_programs(1) - 1)
    def _():
        o_ref[...] = acc_ref[...]


def _ffn(x2, h2, wg, wu, wd):
    T = x2.shape[0]
    tm = min(1024, T)
    tf = EXPERT_DIM
    return pl.pallas_call(
        _ffn_kernel,
        grid=(T // tm, FFN_DIM // tf),
        in_specs=[
            pl.BlockSpec((tm, D_MODEL), lambda t, f: (t, 0)),
            pl.BlockSpec((tm, D_MODEL), lambda t, f: (t, 0)),
            pl.BlockSpec((D_MODEL, tf), lambda t, f: (0, f)),
            pl.BlockSpec((D_MODEL, tf), lambda t, f: (0, f)),
            pl.BlockSpec((tf, D_MODEL), lambda t, f: (f, 0)),
        ],
        out_specs=pl.BlockSpec((tm, D_MODEL), lambda t, f: (t, 0)),
        out_shape=jax.ShapeDtypeStruct((T, D_MODEL), f32),
        scratch_shapes=[pltpu.VMEM((tm, D_MODEL), f32)],
        compiler_params=_cparams("parallel", "arbitrary"),
        name="dense_swiglu",
    )(x2, h2, wg, wu, wd)


ROUTE_E1, ROUTE_E2, ROUTE_W1, ROUTE_W2, ROUTE_R1, ROUTE_R2 = range(6)


def _route_kernel(x_ref, ng_ref, wr_ref, br_ref, h_ref, route_ref, count_ref, carry_scr, *, tm):
    t = pl.program_id(0)

    @pl.when(t == 0)
    def _():
        carry_scr[...] = jnp.zeros(carry_scr.shape, f32)

    h = _rms(x_ref[...], D_MODEL) * ng_ref[...]
    h_ref[...] = h
    lane = _iota((tm, LANES), 1)
    logits = jnp.dot(h, wr_ref[...], precision=lax.Precision.HIGHEST,
                     preferred_element_type=f32) + br_ref[...]
    logits = jnp.where(lane < N_EXPERTS, logits, NEG_INF)
    m1 = jnp.max(logits, axis=-1, keepdims=True)
    e1 = jnp.min(jnp.where(logits == m1, lane, LANES), axis=-1, keepdims=True)
    rest = jnp.where(lane == e1, NEG_INF, logits)
    m2 = jnp.max(rest, axis=-1, keepdims=True)
    e2 = jnp.min(jnp.where(rest == m2, lane, LANES), axis=-1, keepdims=True)
    ex = jnp.exp(m2 - m1)
    w1 = 1.0 / (1.0 + ex)
    w2 = ex / (1.0 + ex)
    chosen = ((lane == e1) | (lane == e2)).astype(f32)
    earlier = (_iota((tm, tm), 1) < _iota((tm, tm), 0)).astype(bf16)
    before = jnp.dot(earlier, chosen.astype(bf16), preferred_element_type=f32) + carry_scr[...]
    r1 = jnp.sum(jnp.where(lane == e1, before, 0.0), axis=-1, keepdims=True)
    r2 = jnp.sum(jnp.where(lane == e2, before, 0.0), axis=-1, keepdims=True)
    out = jnp.zeros((tm, LANES), f32)
    for idx, val in ((ROUTE_E1, e1.astype(f32)), (ROUTE_E2, e2.astype(f32)), (ROUTE_W1, w1),
                     (ROUTE_W2, w2), (ROUTE_R1, r1), (ROUTE_R2, r2)):
        out = jnp.where(lane == idx, val, out)
    route_ref[...] = out
    carry_scr[...] += jnp.sum(chosen, axis=0, keepdims=True)
    count_ref[...] = carry_scr[...]


def _route(x2, ng, wr, br):
    T = x2.shape[0]
    tm = min(512, T)
    return pl.pallas_call(
        functools.partial(_route_kernel, tm=tm),
        grid=(T // tm,),
        in_specs=[
            pl.BlockSpec((tm, D_MODEL), lambda t: (t, 0)),
            pl.BlockSpec((1, D_MODEL), lambda t: (0, 0)),
            pl.BlockSpec((D_MODEL, LANES), lambda t: (0, 0)),
            pl.BlockSpec((1, LANES), lambda t: (0, 0)),
        ],
        out_specs=[
            pl.BlockSpec((tm, D_MODEL), lambda t: (t, 0)),
            pl.BlockSpec((tm, LANES), lambda t: (t, 0)),
            pl.BlockSpec((1, LANES), lambda t: (0, 0)),
        ],
        out_shape=[
            jax.ShapeDtypeStruct((T, D_MODEL), f32),
            jax.ShapeDtypeStruct((T, LANES), f32),
            jax.ShapeDtypeStruct((1, LANES), f32),
        ],
        scratch_shapes=[pltpu.VMEM((1, LANES), f32)],
        compiler_params=_cparams("arbitrary"),
        name="moe_route",
    )(x2, ng, wr, br)


def _row_copy(src_ref, src_row, dst_ref, dst_row, sem):
    return pltpu.make_async_copy(src_ref.at[pl.ds(src_row, 1)], dst_ref.at[pl.ds(dst_row, 1)], sem)


def _dispatch_kernel(dest_ref, h_ref, xs_in_ref, xs_ref, sem, *, tm):
    del xs_in_ref

    def issue(r, carry):
        _row_copy(h_ref, r, xs_ref, dest_ref[0, 0, 2 * r], sem).start()
        _row_copy(h_ref, r, xs_ref, dest_ref[0, 0, 2 * r + 1], sem).start()
        return carry

    lax.fori_loop(0, tm, issue, 0)
    for _ in range(2):
        pltpu.make_async_copy(h_ref, xs_ref.at[pl.ds(0, tm)], sem).wait()


def _dispatch(h2, dest3, rows):
    T = h2.shape[0]
    tm = dest3.shape[-1] // 2
    xs0 = jnp.zeros((rows, D_MODEL), f32)
    return pl.pallas_call(
        functools.partial(_dispatch_kernel, tm=tm),
        grid=(T // tm,),
        in_specs=[
            pl.BlockSpec((1, 1, 2 * tm), lambda t: (t, 0, 0), memory_space=pltpu.SMEM),
            pl.BlockSpec((tm, D_MODEL), lambda t: (t, 0)),
            pl.BlockSpec(memory_space=pl.ANY),
        ],
        out_specs=pl.BlockSpec(memory_space=pl.ANY),
        out_shape=jax.ShapeDtypeStruct((rows, D_MODEL), f32),
        scratch_shapes=[pltpu.SemaphoreType.DMA(())],
        input_output_aliases={2: 0},
        compiler_params=_cparams("arbitrary"),
        name="moe_dispatch",
    )(dest3, h2, xs0)


def _expert_kernel(be_ref, nv_ref, xs_ref, wg_ref, wu_ref, wd_ref, ys_ref):
    j = pl.program_id(0)

    @pl.when(j < nv_ref[0])
    def _():
        ys_ref[...] = _swiglu_block(xs_ref[...].astype(bf16), wg_ref[0], wu_ref[0], wd_ref[0])

    @pl.when(j >= nv_ref[0])
    def _():
        ys_ref[...] = jnp.zeros(ys_ref.shape, f32)


def _experts(block_e, n_valid, xs, wg, wu, wd):
    rows = xs.shape[0]
    grid_spec = pltpu.PrefetchScalarGridSpec(
        num_scalar_prefetch=2,
        grid=(rows // EXPERT_ROWS,),
        in_specs=[
            pl.BlockSpec((EXPERT_ROWS, D_MODEL), lambda j, be, nv: (j, 0)),
            pl.BlockSpec((1, D_MODEL, EXPERT_DIM), lambda j, be, nv: (be[j], 0, 0)),
            pl.BlockSpec((1, D_MODEL, EXPERT_DIM), lambda j, be, nv: (be[j], 0, 0)),
            pl.BlockSpec((1, EXPERT_DIM, D_MODEL), lambda j, be, nv: (be[j], 0, 0)),
        ],
        out_specs=pl.BlockSpec((EXPERT_ROWS, D_MODEL), lambda j, be, nv: (j, 0)),
    )
    return pl.pallas_call(
        _expert_kernel,
        grid_spec=grid_spec,
        out_shape=jax.ShapeDtypeStruct((rows, D_MODEL), f32),
        compiler_params=_cparams("arbitrary"),
        name="moe_experts",
    )(block_e, n_valid, xs, wg, wu, wd)


def _combine_kernel(dest_ref, x_ref, route_ref, ys_ref, o_ref, buf, sem, *, tm):
    def issue(r, carry):
        _row_copy(ys_ref, dest_ref[0, 0, 2 * r], buf.at[0], r, sem).start()
        _row_copy(ys_ref, dest_ref[0, 0, 2 * r + 1], buf.at[1], r, sem).start()
        return carry

    lax.fori_loop(0, tm, issue, 0)
    for slot in range(2):
        pltpu.make_async_copy(ys_ref.at[pl.ds(0, tm)], buf.at[slot], sem).wait()
    route = route_ref[...]
    w1 = route[:, ROUTE_W1:ROUTE_W1 + 1]
    w2 = route[:, ROUTE_W2:ROUTE_W2 + 1]
    o_ref[...] = x_ref[...] + w1 * buf[0] + w2 * buf[1]


def _combine(x2, route, dest3, ys):
    T = x2.shape[0]
    tm = dest3.shape[-1] // 2
    return pl.pallas_call(
        functools.partial(_combine_kernel, tm=tm),
        grid=(T // tm,),
        in_specs=[
            pl.BlockSpec((1, 1, 2 * tm), lambda t: (t, 0, 0), memory_space=pltpu.SMEM),
            pl.BlockSpec((tm, D_MODEL), lambda t: (t, 0)),
            pl.BlockSpec((tm, LANES), lambda t: (t, 0)),
            pl.BlockSpec(memory_space=pl.ANY),
        ],
        out_specs=pl.BlockSpec((tm, D_MODEL), lambda t: (t, 0)),
        out_shape=jax.ShapeDtypeStruct((T, D_MODEL), f32),
        scratch_shapes=[pltpu.VMEM((2, tm, D_MODEL), f32), pltpu.SemaphoreType.DMA(())],
        compiler_params=_cparams("arbitrary"),
        name="moe_combine",
    )(dest3, x2, route, ys)


def _moe(x2, ng, wr, br, wg, wu, wd):
    T = x2.shape[0]
    tm = min(256, T)
    h2, route, counts = _route(x2, ng, wr, br)
    counts = counts[0, :N_EXPERTS].astype(jnp.int32)
    padded = (counts + EXPERT_ROWS - 1) // EXPERT_ROWS * EXPERT_ROWS
    pends = jnp.cumsum(padded)
    pstarts = pends - padded
    n_blocks = -(-(2 * T) // EXPERT_ROWS) + N_EXPERTS
    block_e = jnp.minimum(
        jnp.searchsorted(pends, jnp.arange(n_blocks, dtype=jnp.int32) * EXPERT_ROWS, side='right'),
        N_EXPERTS - 1).astype(jnp.int32)
    n_valid = (pends[-1:] // EXPERT_ROWS).astype(jnp.int32)
    e12 = route[:, ROUTE_E1:ROUTE_E2 + 1].astype(jnp.int32)
    r12 = route[:, ROUTE_R1:ROUTE_R2 + 1].astype(jnp.int32)
    dest = pstarts[e12] + r12
    dest3 = dest.reshape(T // tm, 1, 2 * tm)
    xs = _dispatch(h2, dest3, n_blocks * EXPERT_ROWS)
    ys = _experts(block_e, n_valid, xs, wg, wu, wd)
    return _combine(x2, route, dest3, ys)


def _pad_cols(w, n):
    return jnp.pad(w, ((0, 0), (0, n - w.shape[1])))


def _layer_weights(w_in, c_w_uq, c_w_ukv):
    gate_end = N_BRANCH * D_MODEL
    a_end = gate_end + 2 * WIDTH
    b_end = a_end + 3 * WIDTH
    cq = w_in[:, b_end:b_end + C_Q_RANK]
    ckv = w_in[:, b_end + C_Q_RANK:b_end + C_Q_RANK + C_KV_RANK]
    kr = w_in[:, b_end + C_Q_RANK + C_KV_RANK:b_end + C_Q_RANK + C_KV_RANK + C_ROPE_DIM]
    d_start = b_end + C_Q_RANK + C_KV_RANK + C_ROPE_DIM
    kr_block = jnp.pad(kr, ((0, 0), (C_NOPE_DIM, C_HEAD_PAD - C_QK_DIM)))
    w = jnp.concatenate([w_in[:, :b_end], _pad_cols(cq, WIDTH), ckv, kr_block, w_in[:, d_start:]],
                        axis=1).astype(bf16)
    wuq = c_w_uq.reshape(C_Q_RANK, C_HEADS, C_QK_DIM)
    wuq = jnp.pad(wuq, ((0, WIDTH - C_Q_RANK), (0, 0), (0, C_HEAD_PAD - C_QK_DIM)))
    wuq = wuq.reshape(WIDTH, C_HEADS * C_HEAD_PAD).astype(bf16)
    wukv = c_w_ukv.reshape(C_KV_RANK, C_HEADS, C_NOPE_DIM + C_V_DIM)
    wuk = jnp.pad(wukv[:, :, :C_NOPE_DIM], ((0, 0), (0, 0), (0, C_HEAD_PAD - C_NOPE_DIM)))
    wuk = wuk.reshape(C_KV_RANK, C_HEADS * C_HEAD_PAD).astype(bf16)
    wuv = wukv[:, :, C_NOPE_DIM:].reshape(C_KV_RANK, C_HEADS * C_V_DIM).astype(bf16)
    return w, wuq, wuk, wuv


def _rotation_constants():
    ch = jnp.arange(WIDTH)
    gsum_b = ((ch[:, None] // B_HEAD_DIM) == (ch[None, :] // B_HEAD_DIM)).astype(f32) / B_HEAD_DIM
    half_b = B_ROT_DIM // 2
    src, dst = ch[:, None], ch[None, :]
    same = (src // B_HEAD_DIM) == (dst // B_HEAD_DIM)
    ds_, dd = src % B_HEAD_DIM, dst % B_HEAD_DIM
    rot_b = jnp.where(same & (dd < half_b) & (ds_ == dd + half_b), -1.0,
                      jnp.where(same & (dd >= half_b) & (dd < B_ROT_DIM) & (ds_ == dd - half_b), 1.0, 0.0))
    lc = jnp.arange(C_HEAD_PAD)
    real = lc < C_QK_DIM
    gsum_c = (real[:, None] & real[None, :]).astype(f32) / C_QK_DIM
    half_c = C_ROPE_DIM // 2
    s, d = lc[:, None], lc[None, :]
    lo = (d >= C_NOPE_DIM) & (d < C_NOPE_DIM + half_c)
    hi = (d >= C_NOPE_DIM + half_c) & (d < C_QK_DIM)
    rot_c = jnp.where(lo & (s == d + half_c), -1.0, jnp.where(hi & (s == d - half_c), 1.0, 0.0))
    return gsum_b.astype(bf16), rot_b.astype(bf16), gsum_c.astype(bf16), rot_c.astype(bf16)


def kernel(x, positions, attn_norm, ffn_norm, w_in, a_ln_g, a_ln_b, a_ws, a_bs, b_qn, b_kn, c_cq_norm, c_ckv_norm, c_w_uq, c_w_ukv, c_qn, c_kn, d_conv_w, d_conv_b, d_ln_g, d_ln_b, w_branch, w_out, ffn_w_gate, ffn_w_up, ffn_w_down, moe_router, moe_router_b, moe_w_gate, moe_w_up, moe_w_down):
    B, S, _ = x.shape
    T = B * S
    assert S % ATT_BLOCK == 0 and T % EXPERT_ROWS == 0
    x2 = x.reshape(T, D_MODEL)
    cos_b, sin_b, cos_c, sin_c = _rope_tables(positions)
    gsum_b, rot_b, gsum_c, rot_c = _rotation_constants()
    row = lambda v: v.reshape(1, -1)

    for l in range(DEPTH):
        w, wuq, wuk, wuv = _layer_weights(w_in[l], c_w_uq[l], c_w_ukv[l])
        proj = _inproj(x2, row(attn_norm[l]), w)

        o_a = _sgu(proj, row(a_ln_g[l]), row(a_ln_b[l]), a_ws[l], a_bs[l].T)

        qb, kb, vtb, bias = _moba_prep(proj, B, S, cos_b, sin_b,
                                       row(jnp.tile(b_qn[l], B_HEADS)), row(jnp.tile(b_kn[l], B_HEADS)),
                                       gsum_b, rot_b)
        o_b = _attention(qb, kb, vtb, bias, B, S, B_HEAD_DIM)

        pad_gain = lambda g, n: row(jnp.pad(g, (0, n - g.shape[0])))
        qc, kc, vtc = _mla_prep(proj, B, S, cos_c, sin_c, pad_gain(c_cq_norm[l], WIDTH),
                                row(c_ckv_norm[l]), wuq, wuk, wuv,
                                pad_gain(c_qn[l], C_HEAD_PAD), pad_gain(c_kn[l], C_HEAD_PAD),
                                gsum_c, rot_c)
        o_c = _attention(qc, kc, vtc, None, B, S, C_HEAD_PAD)

        o_d = _conformer(proj, B, S, d_conv_w[l], row(d_conv_b[l]), row(d_ln_g[l]), row(d_ln_b[l]))

        x2, h2 = _merge(x2, proj, o_a, o_b, o_c, o_d, w_branch[l].astype(bf16),
                        w_out[l].astype(bf16), row(ffn_norm[l]))
        i = l // 2
        if l % 2 == 0:
            x2 = _ffn(x2, h2, ffn_w_gate[i].astype(bf16), ffn_w_up[i].astype(bf16),
                      ffn_w_down[i].astype(bf16))
        else:
            wr = _pad_cols(moe_router[i], LANES)
            br = _pad_cols(row(moe_router_b[i]), LANES)
            x2 = _moe(x2, row(ffn_norm[l]), wr, br, moe_w_gate[i].astype(bf16),
                      moe_w_up[i].astype(bf16), moe_w_down[i].astype(bf16))
    return x2.reshape(B, S, D_MODEL)
```

```python
import functools

import jax
import jax.numpy as jnp
from jax import lax
from jax.experimental import pallas as pl
from jax.experimental.pallas import tpu as pltpu

D_MODEL = 1024
DEPTH = 4
N_BRANCH = 4
ROPE_THETA = 500000.0
NORM_EPS = 1e-6
NEG_INF = -1e30

WIDTH = 256
A_GROUPS = 4
A_GROUP_DIM = 64
A_CHUNK = 128
B_HEADS = 4
B_HEAD_DIM = 64
B_ROT_DIM = 16
MOBA_BLOCK = 256
MOBA_TOPK = 3
C_HEADS = 4
C_Q_RANK = 192
C_KV_RANK = 128
C_NOPE_DIM = 64
C_ROPE_DIM = 32
C_V_DIM = 64
C_QK_DIM = C_NOPE_DIM + C_ROPE_DIM
C_HEAD_PAD = 128
D_CONV = 31
FFN_DIM = 2816
N_EXPERTS = 8
EXPERT_DIM = 1408
EXPERT_ROWS = 512

ATT_BLOCK = 256
ATT_HEADS = 4
HEAD_SLOT = 128
V_AUG = C_V_DIM + 16
LOG2E = 1.4426950408889634
SM_ROWS = 32
LANES = 128
SUBLANES = 8
CONV_HALO = 32
CONV_ROWS = 32

COL_GATE = 0
COL_AU = 4096
COL_AV = 4352
COL_BQ = 4608
COL_BK = 4864
COL_BV = 5120
COL_CQ = 5376
COL_CKV = 5632
COL_KR = 5760
COL_DA = 5888
COL_DG = 6144
PROJ_COLS = 6400

VMEM_LIMIT = 56 * 1024 * 1024

bf16 = jnp.bfloat16
f32 = jnp.float32


def _cparams(*sem):
    return pltpu.CompilerParams(dimension_semantics=sem, vmem_limit_bytes=VMEM_LIMIT)


def _rms(x, n):
    return x * lax.rsqrt(jnp.sum(x * x, axis=-1, keepdims=True) * (1.0 / n) + NORM_EPS)


def _layernorm(x, g, b):
    mu = jnp.mean(x, axis=-1, keepdims=True)
    xc = x - mu
    return xc * lax.rsqrt(jnp.mean(xc * xc, axis=-1, keepdims=True) + NORM_EPS) * g + b


def _iota(shape, dim):
    return lax.broadcasted_iota(jnp.int32, shape, dim)


def _rope_table_kernel(pos_ref, inv_ref, eb_ref, ec_ref, cb_ref, sb_ref, cc_ref, sc_ref):
    ang = pos_ref[0].astype(f32) * inv_ref[...]
    c = jnp.cos(ang)
    s = jnp.sin(ang)
    hi = lax.Precision.HIGHEST
    cb_ref[0] = jnp.dot(c, eb_ref[...], precision=hi, preferred_element_type=f32)
    sb_ref[0] = jnp.dot(s, eb_ref[...], precision=hi, preferred_element_type=f32)
    cc_ref[0] = jnp.dot(c, ec_ref[...], precision=hi, preferred_element_type=f32)
    sc_ref[0] = jnp.dot(s, ec_ref[...], precision=hi, preferred_element_type=f32)


def _rope_tables(positions):
    B, S = positions.shape
    R = min(512, S)
    nb = B_ROT_DIM // 2
    nc = C_ROPE_DIM // 2
    inv_b = f32(ROPE_THETA) ** (-jnp.arange(0, B_ROT_DIM, 2, dtype=f32) / B_ROT_DIM)
    inv_c = f32(ROPE_THETA) ** (-jnp.arange(0, C_ROPE_DIM, 2, dtype=f32) / C_ROPE_DIM)
    inv = jnp.zeros((1, LANES), f32).at[0, :nb].set(inv_b).at[0, nb:nb + nc].set(inv_c)
    zero_lane = LANES - 1
    ch = jnp.arange(WIDTH)
    d = ch % B_HEAD_DIM
    src_b = jnp.where(d < B_ROT_DIM, d % nb, zero_lane)
    eb = (jnp.arange(LANES)[:, None] == src_b[None, :]).astype(f32)
    dc = jnp.arange(C_HEAD_PAD)
    in_rope = (dc >= C_NOPE_DIM) & (dc < C_QK_DIM)
    src_c = jnp.where(in_rope, nb + (dc - C_NOPE_DIM) % nc, zero_lane)
    ec = (jnp.arange(LANES)[:, None] == src_c[None, :]).astype(f32)
    pos3 = positions.reshape(B, S, 1)
    out = pl.pallas_call(
        _rope_table_kernel,
        grid=(B, S // R),
        in_specs=[
            pl.BlockSpec((1, R, 1), lambda b, r: (b, r, 0)),
            pl.BlockSpec((1, LANES), lambda b, r: (0, 0)),
            pl.BlockSpec((LANES, WIDTH), lambda b, r: (0, 0)),
            pl.BlockSpec((LANES, C_HEAD_PAD), lambda b, r: (0, 0)),
        ],
        out_specs=[
            pl.BlockSpec((1, R, WIDTH), lambda b, r: (b, r, 0)),
            pl.BlockSpec((1, R, WIDTH), lambda b, r: (b, r, 0)),
            pl.BlockSpec((1, R, C_HEAD_PAD), lambda b, r: (b, r, 0)),
            pl.BlockSpec((1, R, C_HEAD_PAD), lambda b, r: (b, r, 0)),
        ],
        out_shape=[
            jax.ShapeDtypeStruct((B, S, WIDTH), f32),
            jax.ShapeDtypeStruct((B, S, WIDTH), f32),
            jax.ShapeDtypeStruct((B, S, C_HEAD_PAD), f32),
            jax.ShapeDtypeStruct((B, S, C_HEAD_PAD), f32),
        ],
        compiler_params=_cparams("parallel", "parallel"),
        name="rope_tables",
    )(pos3, inv, eb, ec)
    return out


def _inproj_kernel(x_ref, g_ref, w_ref, o_ref, *, col_chunk):
    h = (_rms(x_ref[...], D_MODEL) * g_ref[...]).astype(bf16)
    for c in range(PROJ_COLS // col_chunk):
        sl = slice(c * col_chunk, (c + 1) * col_chunk)
        o_ref[:, sl] = jnp.dot(h, w_ref[:, sl], preferred_element_type=f32).astype(bf16)


def _inproj(x2, g, w):
    T = x2.shape[0]
    tm = min(512, T)
    return pl.pallas_call(
        functools.partial(_inproj_kernel, col_chunk=1280),
        grid=(T // tm,),
        in_specs=[
            pl.BlockSpec((tm, D_MODEL), lambda t: (t, 0)),
            pl.BlockSpec((1, D_MODEL), lambda t: (0, 0)),
            pl.BlockSpec((D_MODEL, PROJ_COLS), lambda t: (0, 0)),
        ],
        out_specs=pl.BlockSpec((tm, PROJ_COLS), lambda t: (t, 0)),
        out_shape=jax.ShapeDtypeStruct((T, PROJ_COLS), bf16),
        compiler_params=_cparams("parallel"),
        name="inproj",
    )(x2, g, w)


def _sgu_kernel(u_ref, v_ref, g_ref, b_ref, ws_ref, bst_ref, o_ref, *, chunks):
    row = _iota((A_CHUNK, A_CHUNK), 0)
    col = _iota((A_CHUNK, A_CHUNK), 1)
    causal = col <= row
    group = _iota((A_CHUNK, WIDTH), 1) // A_GROUP_DIM
    wm = [jnp.where(causal, ws_ref[g], 0.0).astype(bf16) for g in range(A_GROUPS)]
    bias = jnp.zeros((A_CHUNK, WIDTH), f32)
    for g in range(A_GROUPS):
        bias = jnp.where(group == g, bst_ref[:, g:g + 1], bias)
    for c in range(chunks):
        rows = slice(c * A_CHUNK, (c + 1) * A_CHUNK)
        vn = _layernorm(v_ref[rows, :].astype(f32), g_ref[...], b_ref[...]).astype(bf16)
        z = bias
        for g in range(A_GROUPS):
            zg = jnp.dot(wm[g], vn, preferred_element_type=f32)
            z = z + jnp.where(group == g, zg, 0.0)
        o_ref[rows, :] = (u_ref[rows, :].astype(f32) * z).astype(bf16)


def _sgu(proj, ln_g, ln_b, ws, bs_t):
    T = proj.shape[0]
    ta = min(512, T)
    cu, cv = COL_AU // WIDTH, COL_AV // WIDTH
    return pl.pallas_call(
        functools.partial(_sgu_kernel, chunks=ta // A_CHUNK),
        grid=(T // ta,),
        in_specs=[
            pl.BlockSpec((ta, WIDTH), lambda t: (t, cu)),
            pl.BlockSpec((ta, WIDTH), lambda t: (t, cv)),
            pl.BlockSpec((1, WIDTH), lambda t: (0, 0)),
            pl.BlockSpec((1, WIDTH), lambda t: (0, 0)),
            pl.BlockSpec((A_GROUPS, A_CHUNK, A_CHUNK), lambda t: (0, 0, 0)),
            pl.BlockSpec((A_CHUNK, A_GROUPS), lambda t: (0, 0)),
        ],
        out_specs=pl.BlockSpec((ta, WIDTH), lambda t: (t, 0)),
        out_shape=jax.ShapeDtypeStruct((T, WIDTH), bf16),
        compiler_params=_cparams("parallel"),
        name="sgu",
    )(proj, proj, ln_g, ln_b, ws, bs_t)


def _conv_kernel(a_ref, g_ref, w_ref, cb_ref, lg_ref, lb_ref, o_ref, z_scr, *, seq):
    z_scr[0:CONV_HALO, :] = jnp.zeros((CONV_HALO, WIDTH), f32)
    z_scr[CONV_HALO:CONV_HALO + seq, :] = (
        a_ref[...].astype(f32) * jax.nn.sigmoid(g_ref[...].astype(f32)))
    first = CONV_HALO - (D_CONV - 1)
    window = CONV_ROWS + CONV_HALO

    def chunk(c, carry):
        base = pl.multiple_of(c * CONV_ROWS, CONV_ROWS)
        win = z_scr[pl.ds(base, window), :]
        acc = jnp.zeros((CONV_ROWS, WIDTH), f32) + cb_ref[...]
        for r in range(SUBLANES):
            taps = [k for k in range(D_CONV) if (first + k) % SUBLANES == r]
            rolled = win if r == 0 else pltpu.roll(win, shift=window - r, axis=0)
            for k in taps:
                lo = first + k - r
                acc = acc + w_ref[k:k + 1, :] * rolled[lo:lo + CONV_ROWS, :]
        y = _layernorm(acc, lg_ref[...], lb_ref[...])
        o_ref[pl.ds(base, CONV_ROWS), :] = (y * jax.nn.sigmoid(y)).astype(bf16)
        return carry

    lax.fori_loop(0, seq // CONV_ROWS, chunk, 0)


def _conformer(proj, B, S, conv_w, conv_b, ln_g, ln_b):
    ca, cg = COL_DA // WIDTH, COL_DG // WIDTH
    return pl.pallas_call(
        functools.partial(_conv_kernel, seq=S),
        grid=(B,),
        in_specs=[
            pl.BlockSpec((S, WIDTH), lambda b: (b, ca)),
            pl.BlockSpec((S, WIDTH), lambda b: (b, cg)),
            pl.BlockSpec((D_CONV, WIDTH), lambda b: (0, 0)),
            pl.BlockSpec((1, WIDTH), lambda b: (0, 0)),
            pl.BlockSpec((1, WIDTH), lambda b: (0, 0)),
            pl.BlockSpec((1, WIDTH), lambda b: (0, 0)),
        ],
        out_specs=pl.BlockSpec((S, WIDTH), lambda b: (b, 0)),
        out_shape=jax.ShapeDtypeStruct((B * S, WIDTH), bf16),
        scratch_shapes=[pltpu.VMEM((CONV_HALO + S, WIDTH), f32)],
        compiler_params=_cparams("parallel"),
        name="conformer_conv",
    )(proj, proj, conv_w, conv_b, ln_g, ln_b)


def _value_rows(v_tok):
    vt = v_tok.T
    ones = jnp.ones((V_AUG - C_V_DIM, vt.shape[1]), f32)
    rows = []
    for h in range(ATT_HEADS):
        rows += [vt[h * C_V_DIM:(h + 1) * C_V_DIM], ones]
    return jnp.concatenate(rows, axis=0).astype(bf16)


def _moba_prep_kernel(q_ref, k_ref, v_ref, cos_ref, sin_ref, qg_ref, kg_ref, gsum_ref, rot_ref,
                      place_ref, qo_ref, ko_ref, vt_ref, kmean_scr, *, nblk):
    c = pl.program_id(1)

    @pl.when(c == 0)
    def _():
        kmean_scr[...] = jnp.zeros(kmean_scr.shape, f32)

    cos = cos_ref[0]
    sin = sin_ref[0]

    def norm_rope(x, gain):
        ss = jnp.dot((x * x).astype(bf16), gsum_ref[...], preferred_element_type=f32)
        xn = x * lax.rsqrt(ss + NORM_EPS) * gain
        xr = jnp.dot(xn.astype(bf16), rot_ref[...], preferred_element_type=f32)
        return xn * cos + xr * sin

    q = norm_rope(q_ref[...].astype(f32), qg_ref[...])
    k = norm_rope(k_ref[...].astype(f32), kg_ref[...])
    slot_lane = _iota((ATT_BLOCK, ATT_HEADS * HEAD_SLOT), 1) % HEAD_SLOT
    k_slots = jnp.dot(k.astype(bf16), place_ref[...], preferred_element_type=f32)
    ko_ref[0, 0] = jnp.where(slot_lane == B_HEAD_DIM + c, 1.0, k_slots).astype(bf16)
    vt_ref[0, 0] = _value_rows(v_ref[...].astype(f32))

    head = _iota((nblk, WIDTH), 1) // B_HEAD_DIM
    blk = _iota((nblk, ATT_BLOCK), 0)
    past = blk < c
    kmean = kmean_scr[...]
    zeros_lo = jnp.zeros((B_HEAD_DIM, ATT_BLOCK), f32)
    zeros_hi = jnp.zeros((HEAD_SLOT - B_HEAD_DIM - nblk, ATT_BLOCK), f32)
    bias_rows = []
    for h in range(B_HEADS):
        km = jnp.where(head == h, kmean, 0.0)
        sc = lax.dot_general(km, q, (((1,), (1,)), ((), ())), precision=lax.Precision.HIGHEST,
                             preferred_element_type=f32)
        sc = jnp.where(past, sc, NEG_INF)
        rank = jnp.zeros((nblk, ATT_BLOCK), jnp.int32)
        for j in range(nblk):
            other = sc[j:j + 1, :]
            ahead = (other > sc) | ((other == sc) & (j < blk))
            rank = rank + ahead.astype(jnp.int32)
        visible = (past & (rank < MOBA_TOPK)) | (blk == c)
        bias_rows += [zeros_lo, jnp.where(visible, 0.0, NEG_INF), zeros_hi]
    bias = jnp.concatenate(bias_rows, axis=0).T
    q_slots = jnp.dot((q * (B_HEAD_DIM ** -0.5 * LOG2E)).astype(bf16), place_ref[...],
                      preferred_element_type=f32)
    qo_ref[...] = (q_slots + bias).astype(bf16)

    kmean_scr[pl.ds(c, 1), :] = jnp.mean(k, axis=0, keepdims=True)


def _moba_prep(proj, B, S, cos_b, sin_b, qg, kg, gsum, rot, place):
    nblk = S // ATT_BLOCK
    assert nblk <= HEAD_SLOT - B_HEAD_DIM
    HD = ATT_HEADS * HEAD_SLOT
    cq, ck, cv = COL_BQ // WIDTH, COL_BK // WIDTH, COL_BV // WIDTH
    tile = lambda col: pl.BlockSpec((ATT_BLOCK, WIDTH), lambda b, c: (b * nblk + c, col))
    table = pl.BlockSpec((1, ATT_BLOCK, WIDTH), lambda b, c: (b, c, 0))
    const = lambda shape: pl.BlockSpec(shape, lambda b, c: (0,) * len(shape))
    return pl.pallas_call(
        functools.partial(_moba_prep_kernel, nblk=nblk),
        grid=(B, nblk),
        in_specs=[tile(cq), tile(ck), tile(cv), table, table,
                  const((1, WIDTH)), const((1, WIDTH)), const((WIDTH, WIDTH)), const((WIDTH, WIDTH)),
                  const((WIDTH, HD))],
        out_specs=[
            pl.BlockSpec((ATT_BLOCK, HD), lambda b, c: (b * nblk + c, 0)),
            pl.BlockSpec((1, 1, ATT_BLOCK, HD), lambda b, c: (b, c, 0, 0)),
            pl.BlockSpec((1, 1, ATT_HEADS * V_AUG, ATT_BLOCK), lambda b, c: (b, c, 0, 0)),
        ],
        out_shape=[
            jax.ShapeDtypeStruct((B * S, HD), bf16),
            jax.ShapeDtypeStruct((B, nblk, ATT_BLOCK, HD), bf16),
            jax.ShapeDtypeStruct((B, nblk, ATT_HEADS * V_AUG, ATT_BLOCK), bf16),
        ],
        scratch_shapes=[pltpu.VMEM((nblk, WIDTH), f32)],
        compiler_params=_cparams("parallel", "arbitrary"),
        name="moba_prep",
    )(proj, proj, proj, cos_b, sin_b, qg, kg, gsum, rot, place)


def _mla_prep_kernel(cq_ref, ckv_ref, kr_ref, cos_ref, sin_ref, cqg_ref, ckvg_ref, wuq_ref, wuk_ref,
                     wuv_ref, qg_ref, kg_ref, gsum_ref, rot_ref, qo_ref, ko_ref, vt_ref):
    cqn = (_rms(cq_ref[...].astype(f32), C_Q_RANK) * cqg_ref[...]).astype(bf16)
    ckvn = (_rms(ckv_ref[...].astype(f32), C_KV_RANK) * ckvg_ref[...]).astype(bf16)
    q = jnp.dot(cqn, wuq_ref[...], preferred_element_type=f32)
    k = jnp.dot(ckvn, wuk_ref[...], preferred_element_type=f32)
    v = jnp.dot(ckvn, wuv_ref[...], preferred_element_type=f32)
    kr = kr_ref[...].astype(f32)
    cos = cos_ref[0]
    sin = sin_ref[0]

    def norm_rope(x, gain):
        ss = jnp.dot((x * x).astype(bf16), gsum_ref[...], preferred_element_type=f32)
        xn = x * lax.rsqrt(ss + NORM_EPS) * gain
        xr = jnp.dot(xn.astype(bf16), rot_ref[...], preferred_element_type=f32)
        return xn * cos + xr * sin

    for h in range(C_HEADS):
        sl = slice(h * C_HEAD_PAD, (h + 1) * C_HEAD_PAD)
        qh = norm_rope(q[:, sl], qg_ref[...])
        kh = norm_rope(k[:, sl] + kr, kg_ref[...])
        qo_ref[:, sl] = (qh * (C_QK_DIM ** -0.5 * LOG2E)).astype(bf16)
        ko_ref[0, 0, :, sl] = kh.astype(bf16)
    vt_ref[0, 0] = _value_rows(v)


def _mla_prep(proj, B, S, cos_c, sin_c, cqg, ckvg, wuq, wuk, wuv, qg, kg, gsum, rot):
    nblk = S // ATT_BLOCK
    HD = C_HEADS * C_HEAD_PAD
    table = pl.BlockSpec((1, ATT_BLOCK, C_HEAD_PAD), lambda b, c: (b, c, 0))
    const = lambda shape: pl.BlockSpec(shape, lambda b, c: (0,) * len(shape))
    return pl.pallas_call(
        _mla_prep_kernel,
        grid=(B, nblk),
        in_specs=[
            pl.BlockSpec((ATT_BLOCK, WIDTH), lambda b, c: (b * nblk + c, COL_CQ // WIDTH)),
            pl.BlockSpec((ATT_BLOCK, LANES), lambda b, c: (b * nblk + c, COL_CKV // LANES)),
            pl.BlockSpec((ATT_BLOCK, LANES), lambda b, c: (b * nblk + c, COL_KR // LANES)),
            table, table,
            const((1, WIDTH)), const((1, C_KV_RANK)), const((WIDTH, HD)), const((C_KV_RANK, HD)),
            const((C_KV_RANK, WIDTH)), const((1, C_HEAD_PAD)), const((1, C_HEAD_PAD)),
            const((C_HEAD_PAD, C_HEAD_PAD)), const((C_HEAD_PAD, C_HEAD_PAD)),
        ],
        out_specs=[
            pl.BlockSpec((ATT_BLOCK, HD), lambda b, c: (b * nblk + c, 0)),
            pl.BlockSpec((1, 1, ATT_BLOCK, HD), lambda b, c: (b, c, 0, 0)),
            pl.BlockSpec((1, 1, ATT_HEADS * V_AUG, ATT_BLOCK), lambda b, c: (b, c, 0, 0)),
        ],
        out_shape=[
            jax.ShapeDtypeStruct((B * S, HD), bf16),
            jax.ShapeDtypeStruct((B, nblk, ATT_BLOCK, HD), bf16),
            jax.ShapeDtypeStruct((B, nblk, ATT_HEADS * V_AUG, ATT_BLOCK), bf16),
        ],
        compiler_params=_cparams("parallel", "parallel"),
        name="mla_prep",
    )(proj, proj, proj, cos_c, sin_c, cqg, ckvg, wuq, wuk, wuv, qg, kg, gsum, rot)


def _attn_kernel(q_ref, k_ref, vt_ref, o_ref, *scratch):
    H = ATT_HEADS
    s_scr, p_scr, acc_scr, m_scr, a_scr = (scratch[n * H:(n + 1) * H] for n in range(5))
    out_scr = scratch[5 * H]
    i = pl.program_id(1)
    nt = (((1,), (1,)), ((), ()))
    chunks = ATT_BLOCK // SM_ROWS

    def scores(j):
        for h in range(H):
            lanes = slice(h * HEAD_SLOT, (h + 1) * HEAD_SLOT)
            s_scr[h][...] = lax.dot_general(k_ref[0, j, :, lanes], q_ref[:, lanes], nt,
                                            preferred_element_type=f32)

    def softmax(own):
        for h in range(H):
            def rows(r):
                s = s_scr[h][r * SM_ROWS:(r + 1) * SM_ROWS, :]
                if own:
                    kpos = _iota((SM_ROWS, ATT_BLOCK), 0) + r * SM_ROWS
                    s = jnp.where(kpos <= _iota((SM_ROWS, ATT_BLOCK), 1), s, NEG_INF)
                return s
            mx = rows(0)
            for r in range(1, chunks):
                mx = jnp.maximum(mx, rows(r))
            mx = jnp.max(mx, axis=0, keepdims=True)
            if own:
                m_new = mx
            else:
                m_old = m_scr[h][...]
                m_new = jnp.maximum(m_old, mx)
                a_scr[h][...] = jnp.exp2(m_old - m_new)
            m_scr[h][...] = m_new
            for r in range(chunks):
                p_scr[h][r * SM_ROWS:(r + 1) * SM_ROWS, :] = jnp.exp2(rows(r) - m_new).astype(bf16)

    def values(j, first):
        for h in range(H):
            pv = jnp.dot(vt_ref[0, j, h * V_AUG:(h + 1) * V_AUG, :], p_scr[h][...],
                         preferred_element_type=f32)
            acc_scr[h][...] = pv if first else a_scr[h][...] * acc_scr[h][...] + pv

    scores(i)
    softmax(own=True)

    @pl.when(i > 0)
    def _():
        scores(0)
        values(i, first=True)

        def step(t, carry):
            values(t - 2, first=False)
            softmax(own=False)
            scores(t)
            return carry

        @pl.when(i > 1)
        def _():
            softmax(own=False)
            scores(1)
            lax.fori_loop(2, i, step, 0)
            values(i - 2, first=False)

        softmax(own=False)
        values(i - 1, first=False)

    @pl.when(i == 0)
    def _():
        values(i, first=True)

    for h in range(H):
        a = acc_scr[h][...]
        out_scr[h * C_V_DIM:(h + 1) * C_V_DIM, :] = a[:C_V_DIM] / a[C_V_DIM:C_V_DIM + 1]
    o_ref[...] = out_scr[...].T.astype(bf16)


def _attention(q, k, vt, B, S, name):
    nblk = S // ATT_BLOCK
    HD = ATT_HEADS * HEAD_SLOT
    return pl.pallas_call(
        _attn_kernel,
        grid=(B, nblk),
        in_specs=[
            pl.BlockSpec((ATT_BLOCK, HD), lambda b, i: (b * nblk + i, 0)),
            pl.BlockSpec((1, nblk, ATT_BLOCK, HD), lambda b, i: (b, 0, 0, 0)),
            pl.BlockSpec((1, nblk, ATT_HEADS * V_AUG, ATT_BLOCK), lambda b, i: (b, 0, 0, 0)),
        ],
        out_specs=pl.BlockSpec((ATT_BLOCK, WIDTH), lambda b, i: (b * nblk + i, 0)),
        out_shape=jax.ShapeDtypeStruct((B * S, WIDTH), bf16),
        scratch_shapes=(
            [pltpu.VMEM((ATT_BLOCK, ATT_BLOCK), f32) for _ in range(ATT_HEADS)]
            + [pltpu.VMEM((ATT_BLOCK, ATT_BLOCK), bf16) for _ in range(ATT_HEADS)]
            + [pltpu.VMEM((V_AUG, ATT_BLOCK), f32) for _ in range(ATT_HEADS)]
            + [pltpu.VMEM((1, ATT_BLOCK), f32) for _ in range(2 * ATT_HEADS)]
            + [pltpu.VMEM((WIDTH, ATT_BLOCK), f32)]),
        compiler_params=_cparams("parallel", "arbitrary"),
        name=name,
    )(q, k, vt)


def _merge_kernel(x_ref, g0_ref, g1_ref, g2_ref, g3_ref, oa_ref, ob_ref, oc_ref, od_ref,
                  wb_ref, wo_ref, ng_ref, xo_ref, ho_ref):
    y = None
    for n, (g_ref, o_ref) in enumerate(((g0_ref, oa_ref), (g1_ref, ob_ref),
                                        (g2_ref, oc_ref), (g3_ref, od_ref))):
        br = jnp.dot(o_ref[...], wb_ref[n], preferred_element_type=f32)
        t = jax.nn.sigmoid(g_ref[...].astype(f32)) * br
        y = t if y is None else y + t
    x = x_ref[...] + jnp.dot(y.astype(bf16), wo_ref[...], preferred_element_type=f32)
    xo_ref[...] = x
    ho_ref[...] = (_rms(x, D_MODEL) * ng_ref[...]).astype(bf16)


def _merge(x2, proj, oa, ob, oc, od, wb, wo, ng):
    T = x2.shape[0]
    tm = min(512, T)
    gate = lambda n: pl.BlockSpec((tm, D_MODEL), lambda t: (t, n))
    branch = pl.BlockSpec((tm, WIDTH), lambda t: (t, 0))
    return pl.pallas_call(
        _merge_kernel,
        grid=(T // tm,),
        in_specs=[
            pl.BlockSpec((tm, D_MODEL), lambda t: (t, 0)),
            gate(0), gate(1), gate(2), gate(3), branch, branch, branch, branch,
            pl.BlockSpec((N_BRANCH, WIDTH, D_MODEL), lambda t: (0, 0, 0)),
            pl.BlockSpec((D_MODEL, D_MODEL), lambda t: (0, 0)),
            pl.BlockSpec((1, D_MODEL), lambda t: (0, 0)),
        ],
        out_specs=[
            pl.BlockSpec((tm, D_MODEL), lambda t: (t, 0)),
            pl.BlockSpec((tm, D_MODEL), lambda t: (t, 0)),
        ],
        out_shape=[
            jax.ShapeDtypeStruct((T, D_MODEL), f32),
            jax.ShapeDtypeStruct((T, D_MODEL), bf16),
        ],
        compiler_params=_cparams("parallel"),
        name="merge",
    )(x2, proj, proj, proj, proj, oa, ob, oc, od, wb, wo, ng)


def _swiglu_block(h, wg, wu, wd):
    a = jnp.dot(h, wg, preferred_element_type=f32)
    b = jnp.dot(h, wu, preferred_element_type=f32)
    t = (a * jax.nn.sigmoid(a) * b).astype(bf16)
    return jnp.dot(t, wd, preferred_element_type=f32)


def _ffn_kernel(x_ref, h_ref, wg_ref, wu_ref, wd_ref, o_ref, acc_ref):
    f = pl.program_id(1)
    y = _swiglu_block(h_ref[...], wg_ref[...], wu_ref[...], wd_ref[...])

    @pl.when(f == 0)
    def _():
        acc_ref[...] = x_ref[...] + y

    @pl.when(f > 0)
    def _():
        acc_ref[...] += y

    @pl.when(f == pl.num_programs(1) - 1)
    def _():
        o_ref[...] = acc_ref[...]


def _ffn(x2, h2, wg, wu, wd):
    T = x2.shape[0]
    tm = min(1024, T)
    tf = EXPERT_DIM
    return pl.pallas_call(
        _ffn_kernel,
        grid=(T // tm, FFN_DIM // tf),
        in_specs=[
            pl.BlockSpec((tm, D_MODEL), lambda t, f: (t, 0)),
            pl.BlockSpec((tm, D_MODEL), lambda t, f: (t, 0)),
            pl.BlockSpec((D_MODEL, tf), lambda t, f: (0, f)),
            pl.BlockSpec((D_MODEL, tf), lambda t, f: (0, f)),
            pl.BlockSpec((tf, D_MODEL), lambda t, f: (f, 0)),
        ],
        out_specs=pl.BlockSpec((tm, D_MODEL), lambda t, f: (t, 0)),
        out_shape=jax.ShapeDtypeStruct((T, D_MODEL), f32),
        scratch_shapes=[pltpu.VMEM((tm, D_MODEL), f32)],
        compiler_params=_cparams("parallel", "arbitrary"),
        name="dense_swiglu",
    )(x2, h2, wg, wu, wd)


ROUTE_E1, ROUTE_E2, ROUTE_W1, ROUTE_W2, ROUTE_R1, ROUTE_R2 = range(6)


def _route_kernel(x_ref, ng_ref, wr_ref, br_ref, h_ref, route_ref, count_ref, carry_scr, *, tm):
    t = pl.program_id(0)

    @pl.when(t == 0)
    def _():
        carry_scr[...] = jnp.zeros(carry_scr.shape, f32)

    h = _rms(x_ref[...], D_MODEL) * ng_ref[...]
    h_ref[...] = h
    lane = _iota((tm, LANES), 1)
    logits = jnp.dot(h, wr_ref[...], precision=lax.Precision.HIGHEST,
                     preferred_element_type=f32) + br_ref[...]
    logits = jnp.where(lane < N_EXPERTS, logits, NEG_INF)
    m1 = jnp.max(logits, axis=-1, keepdims=True)
    e1 = jnp.min(jnp.where(logits == m1, lane, LANES), axis=-1, keepdims=True)
    rest = jnp.where(lane == e1, NEG_INF, logits)
    m2 = jnp.max(rest, axis=-1, keepdims=True)
    e2 = jnp.min(jnp.where(rest == m2, lane, LANES), axis=-1, keepdims=True)
    ex = jnp.exp(m2 - m1)
    w1 = 1.0 / (1.0 + ex)
    w2 = ex / (1.0 + ex)
    chosen = ((lane == e1) | (lane == e2)).astype(f32)
    earlier = (_iota((tm, tm), 1) < _iota((tm, tm), 0)).astype(bf16)
    before = jnp.dot(earlier, chosen.astype(bf16), preferred_element_type=f32) + carry_scr[...]
    r1 = jnp.sum(jnp.where(lane == e1, before, 0.0), axis=-1, keepdims=True)
    r2 = jnp.sum(jnp.where(lane == e2, before, 0.0), axis=-1, keepdims=True)
    out = jnp.zeros((tm, LANES), f32)
    for idx, val in ((ROUTE_E1, e1.astype(f32)), (ROUTE_E2, e2.astype(f32)), (ROUTE_W1, w1),
                     (ROUTE_W2, w2), (ROUTE_R1, r1), (ROUTE_R2, r2)):
        out = jnp.where(lane == idx, val, out)
    route_ref[...] = out
    carry_scr[...] += jnp.sum(chosen, axis=0, keepdims=True)
    count_ref[...] = carry_scr[...]


def _route(x2, ng, wr, br):
    T = x2.shape[0]
    tm = min(512, T)
    return pl.pallas_call(
        functools.partial(_route_kernel, tm=tm),
        grid=(T // tm,),
        in_specs=[
            pl.BlockSpec((tm, D_MODEL), lambda t: (t, 0)),
            pl.BlockSpec((1, D_MODEL), lambda t: (0, 0)),
            pl.BlockSpec((D_MODEL, LANES), lambda t: (0, 0)),
            pl.BlockSpec((1, LANES), lambda t: (0, 0)),
        ],
        out_specs=[
            pl.BlockSpec((tm, D_MODEL), lambda t: (t, 0)),
            pl.BlockSpec((tm, LANES), lambda t: (t, 0)),
            pl.BlockSpec((1, LANES), lambda t: (0, 0)),
        ],
        out_shape=[
            jax.ShapeDtypeStruct((T, D_MODEL), f32),
            jax.ShapeDtypeStruct((T, LANES), f32),
            jax.ShapeDtypeStruct((1, LANES), f32),
        ],
        scratch_shapes=[pltpu.VMEM((1, LANES), f32)],
        compiler_params=_cparams("arbitrary"),
        name="moe_route",
    )(x2, ng, wr, br)


def _row_copy(src_ref, src_row, dst_ref, dst_row, sem):
    return pltpu.make_async_copy(src_ref.at[pl.ds(src_row, 1)], dst_ref.at[pl.ds(dst_row, 1)], sem)


def _dispatch_kernel(dest_ref, h_ref, xs_in_ref, xs_ref, sem, *, tm):
    del xs_in_ref

    def issue(r, carry):
        _row_copy(h_ref, r, xs_ref, dest_ref[0, 0, 2 * r], sem).start()
        _row_copy(h_ref, r, xs_ref, dest_ref[0, 0, 2 * r + 1], sem).start()
        return carry

    lax.fori_loop(0, tm, issue, 0)
    for _ in range(2):
        pltpu.make_async_copy(h_ref, xs_ref.at[pl.ds(0, tm)], sem).wait()


def _dispatch(h2, dest3, rows):
    T = h2.shape[0]
    tm = dest3.shape[-1] // 2
    xs0 = jnp.zeros((rows, D_MODEL), f32)
    return pl.pallas_call(
        functools.partial(_dispatch_kernel, tm=tm),
        grid=(T // tm,),
        in_specs=[
            pl.BlockSpec((1, 1, 2 * tm), lambda t: (t, 0, 0), memory_space=pltpu.SMEM),
            pl.BlockSpec((tm, D_MODEL), lambda t: (t, 0)),
            pl.BlockSpec(memory_space=pl.ANY),
        ],
        out_specs=pl.BlockSpec(memory_space=pl.ANY),
        out_shape=jax.ShapeDtypeStruct((rows, D_MODEL), f32),
        scratch_shapes=[pltpu.SemaphoreType.DMA(())],
        input_output_aliases={2: 0},
        compiler_params=_cparams("arbitrary"),
        name="moe_dispatch",
    )(dest3, h2, xs0)


def _expert_kernel(be_ref, nv_ref, xs_ref, wg_ref, wu_ref, wd_ref, ys_ref):
    j = pl.program_id(0)

    @pl.when(j < nv_ref[0])
    def _():
        ys_ref[...] = _swiglu_block(xs_ref[...].astype(bf16), wg_ref[0], wu_ref[0], wd_ref[0])

    @pl.when(j >= nv_ref[0])
    def _():
        ys_ref[...] = jnp.zeros(ys_ref.shape, f32)


def _experts(block_e, n_valid, xs, wg, wu, wd):
    rows = xs.shape[0]
    grid_spec = pltpu.PrefetchScalarGridSpec(
        num_scalar_prefetch=2,
        grid=(rows // EXPERT_ROWS,),
        in_specs=[
            pl.BlockSpec((EXPERT_ROWS, D_MODEL), lambda j, be, nv: (j, 0)),
            pl.BlockSpec((1, D_MODEL, EXPERT_DIM), lambda j, be, nv: (be[j], 0, 0)),
            pl.BlockSpec((1, D_MODEL, EXPERT_DIM), lambda j, be, nv: (be[j], 0, 0)),
            pl.BlockSpec((1, EXPERT_DIM, D_MODEL), lambda j, be, nv: (be[j], 0, 0)),
        ],
        out_specs=pl.BlockSpec((EXPERT_ROWS, D_MODEL), lambda j, be, nv: (j, 0)),
    )
    return pl.pallas_call(
        _expert_kernel,
        grid_spec=grid_spec,
        out_shape=jax.ShapeDtypeStruct((rows, D_MODEL), f32),
        compiler_params=_cparams("arbitrary"),
        name="moe_experts",
    )(block_e, n_valid, xs, wg, wu, wd)


def _combine_kernel(dest_ref, x_ref, route_ref, ys_ref, o_ref, buf, sem, *, tm):
    def issue(r, carry):
        _row_copy(ys_ref, dest_ref[0, 0, 2 * r], buf.at[0], r, sem).start()
        _row_copy(ys_ref, dest_ref[0, 0, 2 * r + 1], buf.at[1], r, sem).start()
        return carry

    lax.fori_loop(0, tm, issue, 0)
    for slot in range(2):
        pltpu.make_async_copy(ys_ref.at[pl.ds(0, tm)], buf.at[slot], sem).wait()
    route = route_ref[...]
    w1 = route[:, ROUTE_W1:ROUTE_W1 + 1]
    w2 = route[:, ROUTE_W2:ROUTE_W2 + 1]
    o_ref[...] = x_ref[...] + w1 * buf[0] + w2 * buf[1]


def _combine(x2, route, dest3, ys):
    T = x2.shape[0]
    tm = dest3.shape[-1] // 2
    return pl.pallas_call(
        functools.partial(_combine_kernel, tm=tm),
        grid=(T // tm,),
        in_specs=[
            pl.BlockSpec((1, 1, 2 * tm), lambda t: (t, 0, 0), memory_space=pltpu.SMEM),
            pl.BlockSpec((tm, D_MODEL), lambda t: (t, 0)),
            pl.BlockSpec((tm, LANES), lambda t: (t, 0)),
            pl.BlockSpec(memory_space=pl.ANY),
        ],
        out_specs=pl.BlockSpec((tm, D_MODEL), lambda t: (t, 0)),
        out_shape=jax.ShapeDtypeStruct((T, D_MODEL), f32),
        scratch_shapes=[pltpu.VMEM((2, tm, D_MODEL), f32), pltpu.SemaphoreType.DMA(())],
        compiler_params=_cparams("arbitrary"),
        name="moe_combine",
    )(dest3, x2, route, ys)


def _moe(x2, ng, wr, br, wg, wu, wd):
    T = x2.shape[0]
    tm = min(256, T)
    h2, route, counts = _route(x2, ng, wr, br)
    counts = counts[0, :N_EXPERTS].astype(jnp.int32)
    padded = (counts + EXPERT_ROWS - 1) // EXPERT_ROWS * EXPERT_ROWS
    pends = jnp.cumsum(padded)
    pstarts = pends - padded
    n_blocks = -(-(2 * T) // EXPERT_ROWS) + N_EXPERTS
    block_e = jnp.minimum(
        jnp.searchsorted(pends, jnp.arange(n_blocks, dtype=jnp.int32) * EXPERT_ROWS, side='right'),
        N_EXPERTS - 1).astype(jnp.int32)
    n_valid = (pends[-1:] // EXPERT_ROWS).astype(jnp.int32)
    e12 = route[:, ROUTE_E1:ROUTE_E2 + 1].astype(jnp.int32)
    r12 = route[:, ROUTE_R1:ROUTE_R2 + 1].astype(jnp.int32)
    dest = pstarts[e12] + r12
    dest3 = dest.reshape(T // tm, 1, 2 * tm)
    xs = _dispatch(h2, dest3, n_blocks * EXPERT_ROWS)
    ys = _experts(block_e, n_valid, xs, wg, wu, wd)
    return _combine(x2, route, dest3, ys)


def _pad_cols(w, n):
    return jnp.pad(w, ((0, 0), (0, n - w.shape[1])))


def _layer_weights(w_in, c_w_uq, c_w_ukv):
    gate_end = N_BRANCH * D_MODEL
    a_end = gate_end + 2 * WIDTH
    b_end = a_end + 3 * WIDTH
    cq = w_in[:, b_end:b_end + C_Q_RANK]
    ckv = w_in[:, b_end + C_Q_RANK:b_end + C_Q_RANK + C_KV_RANK]
    kr = w_in[:, b_end + C_Q_RANK + C_KV_RANK:b_end + C_Q_RANK + C_KV_RANK + C_ROPE_DIM]
    d_start = b_end + C_Q_RANK + C_KV_RANK + C_ROPE_DIM
    kr_block = jnp.pad(kr, ((0, 0), (C_NOPE_DIM, C_HEAD_PAD - C_QK_DIM)))
    w = jnp.concatenate([w_in[:, :b_end], _pad_cols(cq, WIDTH), ckv, kr_block, w_in[:, d_start:]],
                        axis=1).astype(bf16)
    wuq = c_w_uq.reshape(C_Q_RANK, C_HEADS, C_QK_DIM)
    wuq = jnp.pad(wuq, ((0, WIDTH - C_Q_RANK), (0, 0), (0, C_HEAD_PAD - C_QK_DIM)))
    wuq = wuq.reshape(WIDTH, C_HEADS * C_HEAD_PAD).astype(bf16)
    wukv = c_w_ukv.reshape(C_KV_RANK, C_HEADS, C_NOPE_DIM + C_V_DIM)
    wuk = jnp.pad(wukv[:, :, :C_NOPE_DIM], ((0, 0), (0, 0), (0, C_HEAD_PAD - C_NOPE_DIM)))
    wuk = wuk.reshape(C_KV_RANK, C_HEADS * C_HEAD_PAD).astype(bf16)
    wuv = wukv[:, :, C_NOPE_DIM:].reshape(C_KV_RANK, C_HEADS * C_V_DIM).astype(bf16)
    return w, wuq, wuk, wuv


def _rotation_constants():
    ch = jnp.arange(WIDTH)
    gsum_b = ((ch[:, None] // B_HEAD_DIM) == (ch[None, :] // B_HEAD_DIM)).astype(f32) / B_HEAD_DIM
    half_b = B_ROT_DIM // 2
    src, dst = ch[:, None], ch[None, :]
    same = (src // B_HEAD_DIM) == (dst // B_HEAD_DIM)
    ds_, dd = src % B_HEAD_DIM, dst % B_HEAD_DIM
    rot_b = jnp.where(same & (dd < half_b) & (ds_ == dd + half_b), -1.0,
                      jnp.where(same & (dd >= half_b) & (dd < B_ROT_DIM) & (ds_ == dd - half_b), 1.0, 0.0))
    lc = jnp.arange(C_HEAD_PAD)
    real = lc < C_QK_DIM
    gsum_c = (real[:, None] & real[None, :]).astype(f32) / C_QK_DIM
    half_c = C_ROPE_DIM // 2
    s, d = lc[:, None], lc[None, :]
    lo = (d >= C_NOPE_DIM) & (d < C_NOPE_DIM + half_c)
    hi = (d >= C_NOPE_DIM + half_c) & (d < C_QK_DIM)
    rot_c = jnp.where(lo & (s == d + half_c), -1.0, jnp.where(hi & (s == d - half_c), 1.0, 0.0))
    slot = jnp.arange(ATT_HEADS * HEAD_SLOT)
    place_b = ((slot[None, :] // HEAD_SLOT == ch[:, None] // B_HEAD_DIM)
               & (slot[None, :] % HEAD_SLOT == ch[:, None] % B_HEAD_DIM))
    return (gsum_b.astype(bf16), rot_b.astype(bf16), place_b.astype(bf16),
            gsum_c.astype(bf16), rot_c.astype(bf16))


def kernel(x, positions, attn_norm, ffn_norm, w_in, a_ln_g, a_ln_b, a_ws, a_bs, b_qn, b_kn, c_cq_norm, c_ckv_norm, c_w_uq, c_w_ukv, c_qn, c_kn, d_conv_w, d_conv_b, d_ln_g, d_ln_b, w_branch, w_out, ffn_w_gate, ffn_w_up, ffn_w_down, moe_router, moe_router_b, moe_w_gate, moe_w_up, moe_w_down):
    B, S, _ = x.shape
    T = B * S
    assert S % ATT_BLOCK == 0 and T % EXPERT_ROWS == 0
    x2 = x.reshape(T, D_MODEL)
    cos_b, sin_b, cos_c, sin_c = _rope_tables(positions)
    gsum_b, rot_b, place_b, gsum_c, rot_c = _rotation_constants()
    row = lambda v: v.reshape(1, -1)

    for l in range(DEPTH):
        w, wuq, wuk, wuv = _layer_weights(w_in[l], c_w_uq[l], c_w_ukv[l])
        proj = _inproj(x2, row(attn_norm[l]), w)

        o_a = _sgu(proj, row(a_ln_g[l]), row(a_ln_b[l]), a_ws[l], a_bs[l].T)

        qb, kb, vtb = _moba_prep(proj, B, S, cos_b, sin_b,
                                 row(jnp.tile(b_qn[l], B_HEADS)), row(jnp.tile(b_kn[l], B_HEADS)),
                                 gsum_b, rot_b, place_b)
        o_b = _attention(qb, kb, vtb, B, S, "moba_attention")

        pad_gain = lambda g, n: row(jnp.pad(g, (0, n - g.shape[0])))
        qc, kc, vtc = _mla_prep(proj, B, S, cos_c, sin_c, pad_gain(c_cq_norm[l], WIDTH),
                                row(c_ckv_norm[l]), wuq, wuk, wuv,
                                pad_gain(c_qn[l], C_HEAD_PAD), pad_gain(c_kn[l], C_HEAD_PAD),
                                gsum_c, rot_c)
        o_c = _attention(qc, kc, vtc, B, S, "mla_attention")

        o_d = _conformer(proj, B, S, d_conv_w[l], row(d_conv_b[l]), row(d_ln_g[l]), row(d_ln_b[l]))

        x2, h2 = _merge(x2, proj, o_a, o_b, o_c, o_d, w_branch[l].astype(bf16),
                        w_out[l].astype(bf16), row(ffn_norm[l]))
        i = l // 2
        if l % 2 == 0:
            x2 = _ffn(x2, h2, ffn_w_gate[i].astype(bf16), ffn_w_up[i].astype(bf16),
                      ffn_w_down[i].astype(bf16))
        else:
            wr = _pad_cols(moe_router[i], LANES)
            br = _pad_cols(row(moe_router_b[i]), LANES)
            x2 = _moe(x2, row(ffn_norm[l]), wr, br, moe_w_gate[i].astype(bf16),
                      moe_w_up[i].astype(bf16), moe_w_down[i].astype(bf16))
    return x2.reshape(B, S, D_MODEL)
```

```python
import functools

import jax
import jax.numpy as jnp
from jax import lax
from jax.experimental import pallas as pl
from jax.experimental.pallas import tpu as pltpu

D_MODEL = 1024
DEPTH = 4
N_BRANCH = 4
ROPE_THETA = 500000.0
NORM_EPS = 1e-6
NEG_INF = -1e30

WIDTH = 256
A_GROUPS = 4
A_GROUP_DIM = 64
A_CHUNK = 128
B_HEADS = 4
B_HEAD_DIM = 64
B_ROT_DIM = 16
MOBA_BLOCK = 256
MOBA_TOPK = 3
C_HEADS = 4
C_Q_RANK = 192
C_KV_RANK = 128
C_NOPE_DIM = 64
C_ROPE_DIM = 32
C_V_DIM = 64
C_QK_DIM = C_NOPE_DIM + C_ROPE_DIM
C_HEAD_PAD = 128
D_CONV = 31
FFN_DIM = 2816
N_EXPERTS = 8
EXPERT_DIM = 1408
EXPERT_ROWS = 512

ATT_BLOCK = 256
ATT_QBLOCK = 512
ATT_HEADS = 4
HEAD_SLOT = 128
V_AUG = C_V_DIM + 16
LOG2E = 1.4426950408889634
SM_ROWS = 32
LANES = 128
SUBLANES = 8
CONV_HALO = 32
CONV_ROWS = 32

COL_GATE = 0
COL_AU = 4096
COL_AV = 4352
COL_BQ = 4608
COL_BK = 4864
COL_BV = 5120
COL_CQ = 5376
COL_CKV = 5632
COL_KR = 5760
COL_DA = 5888
COL_DG = 6144
PROJ_COLS = 6400

VMEM_LIMIT = 56 * 1024 * 1024

bf16 = jnp.bfloat16
f32 = jnp.float32


def _cparams(*sem):
    return pltpu.CompilerParams(dimension_semantics=sem, vmem_limit_bytes=VMEM_LIMIT)


def _rms(x, n):
    return x * lax.rsqrt(jnp.sum(x * x, axis=-1, keepdims=True) * (1.0 / n) + NORM_EPS)


def _layernorm(x, g, b):
    mu = jnp.mean(x, axis=-1, keepdims=True)
    xc = x - mu
    return xc * lax.rsqrt(jnp.mean(xc * xc, axis=-1, keepdims=True) + NORM_EPS) * g + b


def _sigmoid(x):
    return 0.5 * jnp.tanh(0.5 * x) + 0.5


def _iota(shape, dim):
    return lax.broadcasted_iota(jnp.int32, shape, dim)


def _rope_table_kernel(pos_ref, inv_ref, eb_ref, ec_ref, cb_ref, sb_ref, cc_ref, sc_ref):
    ang = pos_ref[0].astype(f32) * inv_ref[...]
    c = jnp.cos(ang)
    s = jnp.sin(ang)
    hi = lax.Precision.HIGHEST
    cb_ref[0] = jnp.dot(c, eb_ref[...], precision=hi, preferred_element_type=f32)
    sb_ref[0] = jnp.dot(s, eb_ref[...], precision=hi, preferred_element_type=f32)
    cc_ref[0] = jnp.dot(c, ec_ref[...], precision=hi, preferred_element_type=f32)
    sc_ref[0] = jnp.dot(s, ec_ref[...], precision=hi, preferred_element_type=f32)


def _rope_tables(positions):
    B, S = positions.shape
    R = min(512, S)
    nb = B_ROT_DIM // 2
    nc = C_ROPE_DIM // 2
    inv_b = f32(ROPE_THETA) ** (-jnp.arange(0, B_ROT_DIM, 2, dtype=f32) / B_ROT_DIM)
    inv_c = f32(ROPE_THETA) ** (-jnp.arange(0, C_ROPE_DIM, 2, dtype=f32) / C_ROPE_DIM)
    inv = jnp.zeros((1, LANES), f32).at[0, :nb].set(inv_b).at[0, nb:nb + nc].set(inv_c)
    zero_lane = LANES - 1
    ch = jnp.arange(WIDTH)
    d = ch % B_HEAD_DIM
    src_b = jnp.where(d < B_ROT_DIM, d % nb, zero_lane)
    eb = (jnp.arange(LANES)[:, None] == src_b[None, :]).astype(f32)
    dc = jnp.arange(C_HEAD_PAD)
    in_rope = (dc >= C_NOPE_DIM) & (dc < C_QK_DIM)
    src_c = jnp.where(in_rope, nb + (dc - C_NOPE_DIM) % nc, zero_lane)
    ec = (jnp.arange(LANES)[:, None] == src_c[None, :]).astype(f32)
    pos3 = positions.reshape(B, S, 1)
    out = pl.pallas_call(
        _rope_table_kernel,
        grid=(B, S // R),
        in_specs=[
            pl.BlockSpec((1, R, 1), lambda b, r: (b, r, 0)),
            pl.BlockSpec((1, LANES), lambda b, r: (0, 0)),
            pl.BlockSpec((LANES, WIDTH), lambda b, r: (0, 0)),
            pl.BlockSpec((LANES, C_HEAD_PAD), lambda b, r: (0, 0)),
        ],
        out_specs=[
            pl.BlockSpec((1, R, WIDTH), lambda b, r: (b, r, 0)),
            pl.BlockSpec((1, R, WIDTH), lambda b, r: (b, r, 0)),
            pl.BlockSpec((1, R, C_HEAD_PAD), lambda b, r: (b, r, 0)),
            pl.BlockSpec((1, R, C_HEAD_PAD), lambda b, r: (b, r, 0)),
        ],
        out_shape=[
            jax.ShapeDtypeStruct((B, S, WIDTH), f32),
            jax.ShapeDtypeStruct((B, S, WIDTH), f32),
            jax.ShapeDtypeStruct((B, S, C_HEAD_PAD), f32),
            jax.ShapeDtypeStruct((B, S, C_HEAD_PAD), f32),
        ],
        compiler_params=_cparams("parallel", "parallel"),
        name="rope_tables",
    )(pos3, inv, eb, ec)
    return out


def _inproj_kernel(x_ref, g_ref, w_ref, o_ref, *, col_chunk):
    h = (_rms(x_ref[...], D_MODEL) * g_ref[...]).astype(bf16)
    for c in range(PROJ_COLS // col_chunk):
        sl = slice(c * col_chunk, (c + 1) * col_chunk)
        o_ref[:, sl] = jnp.dot(h, w_ref[:, sl], preferred_element_type=f32).astype(bf16)


def _inproj(x2, g, w):
    T = x2.shape[0]
    tm = min(512, T)
    return pl.pallas_call(
        functools.partial(_inproj_kernel, col_chunk=1280),
        grid=(T // tm,),
        in_specs=[
            pl.BlockSpec((tm, D_MODEL), lambda t: (t, 0)),
            pl.BlockSpec((1, D_MODEL), lambda t: (0, 0)),
            pl.BlockSpec((D_MODEL, PROJ_COLS), lambda t: (0, 0)),
        ],
        out_specs=pl.BlockSpec((tm, PROJ_COLS), lambda t: (t, 0)),
        out_shape=jax.ShapeDtypeStruct((T, PROJ_COLS), bf16),
        compiler_params=_cparams("parallel"),
        name="inproj",
    )(x2, g, w)


def _sgu_kernel(u_ref, v_ref, g_ref, b_ref, ws_ref, bst_ref, o_ref, *, chunks):
    row = _iota((A_CHUNK, A_CHUNK), 0)
    col = _iota((A_CHUNK, A_CHUNK), 1)
    causal = col <= row
    group = _iota((A_CHUNK, WIDTH), 1) // A_GROUP_DIM
    wm = [jnp.where(causal, ws_ref[g], 0.0).astype(bf16) for g in range(A_GROUPS)]
    bias = jnp.zeros((A_CHUNK, WIDTH), f32)
    for g in range(A_GROUPS):
        bias = jnp.where(group == g, bst_ref[:, g:g + 1], bias)
    for c in range(chunks):
        rows = slice(c * A_CHUNK, (c + 1) * A_CHUNK)
        vn = _layernorm(v_ref[rows, :].astype(f32), g_ref[...], b_ref[...]).astype(bf16)
        z = bias
        for g in range(A_GROUPS):
            zg = jnp.dot(wm[g], vn, preferred_element_type=f32)
            z = z + jnp.where(group == g, zg, 0.0)
        o_ref[rows, :] = (u_ref[rows, :].astype(f32) * z).astype(bf16)


def _sgu(proj, ln_g, ln_b, ws, bs_t):
    T = proj.shape[0]
    ta = min(512, T)
    cu, cv = COL_AU // WIDTH, COL_AV // WIDTH
    return pl.pallas_call(
        functools.partial(_sgu_kernel, chunks=ta // A_CHUNK),
        grid=(T // ta,),
        in_specs=[
            pl.BlockSpec((ta, WIDTH), lambda t: (t, cu)),
            pl.BlockSpec((ta, WIDTH), lambda t: (t, cv)),
            pl.BlockSpec((1, WIDTH), lambda t: (0, 0)),
            pl.BlockSpec((1, WIDTH), lambda t: (0, 0)),
            pl.BlockSpec((A_GROUPS, A_CHUNK, A_CHUNK), lambda t: (0, 0, 0)),
            pl.BlockSpec((A_CHUNK, A_GROUPS), lambda t: (0, 0)),
        ],
        out_specs=pl.BlockSpec((ta, WIDTH), lambda t: (t, 0)),
        out_shape=jax.ShapeDtypeStruct((T, WIDTH), bf16),
        compiler_params=_cparams("parallel"),
        name="sgu",
    )(proj, proj, ln_g, ln_b, ws, bs_t)


def _conv_kernel(a_ref, g_ref, w_ref, cb_ref, lg_ref, lb_ref, o_ref, z_scr, *, seq):
    z_scr[0:CONV_HALO, :] = jnp.zeros((CONV_HALO, WIDTH), f32)
    z_scr[CONV_HALO:CONV_HALO + seq, :] = (
        a_ref[...].astype(f32) * _sigmoid(g_ref[...].astype(f32)))
    first = CONV_HALO - (D_CONV - 1)
    window = CONV_ROWS + CONV_HALO

    def chunk(c, carry):
        base = pl.multiple_of(c * CONV_ROWS, CONV_ROWS)
        win = z_scr[pl.ds(base, window), :]
        acc = jnp.zeros((CONV_ROWS, WIDTH), f32) + cb_ref[...]
        for r in range(SUBLANES):
            taps = [k for k in range(D_CONV) if (first + k) % SUBLANES == r]
            rolled = win if r == 0 else pltpu.roll(win, shift=window - r, axis=0)
            for k in taps:
                lo = first + k - r
                acc = acc + w_ref[k:k + 1, :] * rolled[lo:lo + CONV_ROWS, :]
        y = _layernorm(acc, lg_ref[...], lb_ref[...])
        o_ref[pl.ds(base, CONV_ROWS), :] = (y * _sigmoid(y)).astype(bf16)
        return carry

    lax.fori_loop(0, seq // CONV_ROWS, chunk, 0)


def _conformer(proj, B, S, conv_w, conv_b, ln_g, ln_b):
    ca, cg = COL_DA // WIDTH, COL_DG // WIDTH
    return pl.pallas_call(
        functools.partial(_conv_kernel, seq=S),
        grid=(B,),
        in_specs=[
            pl.BlockSpec((S, WIDTH), lambda b: (b, ca)),
            pl.BlockSpec((S, WIDTH), lambda b: (b, cg)),
            pl.BlockSpec((D_CONV, WIDTH), lambda b: (0, 0)),
            pl.BlockSpec((1, WIDTH), lambda b: (0, 0)),
            pl.BlockSpec((1, WIDTH), lambda b: (0, 0)),
            pl.BlockSpec((1, WIDTH), lambda b: (0, 0)),
        ],
        out_specs=pl.BlockSpec((S, WIDTH), lambda b: (b, 0)),
        out_shape=jax.ShapeDtypeStruct((B * S, WIDTH), bf16),
        scratch_shapes=[pltpu.VMEM((CONV_HALO + S, WIDTH), f32)],
        compiler_params=_cparams("parallel"),
        name="conformer_conv",
    )(proj, proj, conv_w, conv_b, ln_g, ln_b)


def _value_rows(v_tok):
    vt = v_tok.T
    ones = jnp.ones((V_AUG - C_V_DIM, vt.shape[1]), f32)
    rows = []
    for h in range(ATT_HEADS):
        rows += [vt[h * C_V_DIM:(h + 1) * C_V_DIM], ones]
    return jnp.concatenate(rows, axis=0).astype(bf16)


def _moba_prep_kernel(q_ref, k_ref, v_ref, cos_ref, sin_ref, qg_ref, kg_ref, gsum_ref, rot_ref,
                      place_ref, qo_ref, ko_ref, vt_ref, kmean_scr, *, nblk):
    c = pl.program_id(1)

    @pl.when(c == 0)
    def _():
        kmean_scr[...] = jnp.zeros(kmean_scr.shape, f32)

    cos = cos_ref[0]
    sin = sin_ref[0]

    def norm_rope(x, gain):
        ss = jnp.dot((x * x).astype(bf16), gsum_ref[...], preferred_element_type=f32)
        xn = x * lax.rsqrt(ss + NORM_EPS) * gain
        xr = jnp.dot(xn.astype(bf16), rot_ref[...], preferred_element_type=f32)
        return xn * cos + xr * sin

    q = norm_rope(q_ref[...].astype(f32), qg_ref[...])
    k = norm_rope(k_ref[...].astype(f32), kg_ref[...])
    slot_lane = _iota((ATT_BLOCK, ATT_HEADS * HEAD_SLOT), 1) % HEAD_SLOT
    k_slots = jnp.dot(k.astype(bf16), place_ref[...], preferred_element_type=f32)
    ko_ref[0, 0] = jnp.where(slot_lane == B_HEAD_DIM + c, 1.0, k_slots).astype(bf16)
    vt_ref[0, 0] = _value_rows(v_ref[...].astype(f32))

    head = _iota((nblk, WIDTH), 1) // B_HEAD_DIM
    blk = _iota((nblk, ATT_BLOCK), 0)
    past = blk < c
    kmean = kmean_scr[...]
    zeros_lo = jnp.zeros((B_HEAD_DIM, ATT_BLOCK), f32)
    zeros_hi = jnp.zeros((HEAD_SLOT - B_HEAD_DIM - nblk, ATT_BLOCK), f32)
    bias_rows = []
    for h in range(B_HEADS):
        km = jnp.where(head == h, kmean, 0.0)
        sc = lax.dot_general(km, q, (((1,), (1,)), ((), ())), precision=lax.Precision.HIGHEST,
                             preferred_element_type=f32)
        sc = jnp.where(past, sc, NEG_INF)
        rank = jnp.zeros((nblk, ATT_BLOCK), jnp.int32)
        for j in range(nblk):
            other = sc[j:j + 1, :]
            ahead = (other > sc) | ((other == sc) & (j < blk))
            rank = rank + ahead.astype(jnp.int32)
        visible = (past & (rank < MOBA_TOPK)) | (blk == c)
        bias_rows += [zeros_lo, jnp.where(visible, 0.0, NEG_INF), zeros_hi]
    bias = jnp.concatenate(bias_rows, axis=0).T
    q_slots = jnp.dot((q * (B_HEAD_DIM ** -0.5 * LOG2E)).astype(bf16), place_ref[...],
                      preferred_element_type=f32)
    qo_ref[...] = (q_slots + bias).astype(bf16)

    kmean_scr[pl.ds(c, 1), :] = jnp.mean(k, axis=0, keepdims=True)


def _moba_prep(proj, B, S, cos_b, sin_b, qg, kg, gsum, rot, place):
    nblk = S // ATT_BLOCK
    assert nblk <= HEAD_SLOT - B_HEAD_DIM
    HD = ATT_HEADS * HEAD_SLOT
    cq, ck, cv = COL_BQ // WIDTH, COL_BK // WIDTH, COL_BV // WIDTH
    tile = lambda col: pl.BlockSpec((ATT_BLOCK, WIDTH), lambda b, c: (b * nblk + c, col))
    table = pl.BlockSpec((1, ATT_BLOCK, WIDTH), lambda b, c: (b, c, 0))
    const = lambda shape: pl.BlockSpec(shape, lambda b, c: (0,) * len(shape))
    return pl.pallas_call(
        functools.partial(_moba_prep_kernel, nblk=nblk),
        grid=(B, nblk),
        in_specs=[tile(cq), tile(ck), tile(cv), table, table,
                  const((1, WIDTH)), const((1, WIDTH)), const((WIDTH, WIDTH)), const((WIDTH, WIDTH)),
                  const((WIDTH, HD))],
        out_specs=[
            pl.BlockSpec((ATT_BLOCK, HD), lambda b, c: (b * nblk + c, 0)),
            pl.BlockSpec((1, 1, ATT_BLOCK, HD), lambda b, c: (b, c, 0, 0)),
            pl.BlockSpec((1, 1, ATT_HEADS * V_AUG, ATT_BLOCK), lambda b, c: (b, c, 0, 0)),
        ],
        out_shape=[
            jax.ShapeDtypeStruct((B * S, HD), bf16),
            jax.ShapeDtypeStruct((B, nblk, ATT_BLOCK, HD), bf16),
            jax.ShapeDtypeStruct((B, nblk, ATT_HEADS * V_AUG, ATT_BLOCK), bf16),
        ],
        scratch_shapes=[pltpu.VMEM((nblk, WIDTH), f32)],
        compiler_params=_cparams("parallel", "arbitrary"),
        name="moba_prep",
    )(proj, proj, proj, cos_b, sin_b, qg, kg, gsum, rot, place)


def _mla_prep_kernel(cq_ref, ckv_ref, kr_ref, cos_ref, sin_ref, cqg_ref, ckvg_ref, wuq_ref, wuk_ref,
                     wuv_ref, qg_ref, kg_ref, gsum_ref, rot_ref, qo_ref, ko_ref, vt_ref):
    cqn = (_rms(cq_ref[...].astype(f32), C_Q_RANK) * cqg_ref[...]).astype(bf16)
    ckvn = (_rms(ckv_ref[...].astype(f32), C_KV_RANK) * ckvg_ref[...]).astype(bf16)
    q = jnp.dot(cqn, wuq_ref[...], preferred_element_type=f32)
    k = jnp.dot(ckvn, wuk_ref[...], preferred_element_type=f32)
    v = jnp.dot(ckvn, wuv_ref[...], preferred_element_type=f32)
    kr = kr_ref[...].astype(f32)
    cos = cos_ref[0]
    sin = sin_ref[0]

    def norm_rope(x, gain):
        ss = jnp.dot((x * x).astype(bf16), gsum_ref[...], preferred_element_type=f32)
        xn = x * lax.rsqrt(ss + NORM_EPS) * gain
        xr = jnp.dot(xn.astype(bf16), rot_ref[...], preferred_element_type=f32)
        return xn * cos + xr * sin

    for h in range(C_HEADS):
        sl = slice(h * C_HEAD_PAD, (h + 1) * C_HEAD_PAD)
        qh = norm_rope(q[:, sl], qg_ref[...])
        kh = norm_rope(k[:, sl] + kr, kg_ref[...])
        qo_ref[:, sl] = (qh * (C_QK_DIM ** -0.5 * LOG2E)).astype(bf16)
        ko_ref[0, 0, :, sl] = kh.astype(bf16)
    vt_ref[0, 0] = _value_rows(v)


def _mla_prep(proj, B, S, cos_c, sin_c, cqg, ckvg, wuq, wuk, wuv, qg, kg, gsum, rot):
    nblk = S // ATT_BLOCK
    HD = C_HEADS * C_HEAD_PAD
    table = pl.BlockSpec((1, ATT_BLOCK, C_HEAD_PAD), lambda b, c: (b, c, 0))
    const = lambda shape: pl.BlockSpec(shape, lambda b, c: (0,) * len(shape))
    return pl.pallas_call(
        _mla_prep_kernel,
        grid=(B, nblk),
        in_specs=[
            pl.BlockSpec((ATT_BLOCK, WIDTH), lambda b, c: (b * nblk + c, COL_CQ // WIDTH)),
            pl.BlockSpec((ATT_BLOCK, LANES), lambda b, c: (b * nblk + c, COL_CKV // LANES)),
            pl.BlockSpec((ATT_BLOCK, LANES), lambda b, c: (b * nblk + c, COL_KR // LANES)),
            table, table,
            const((1, WIDTH)), const((1, C_KV_RANK)), const((WIDTH, HD)), const((C_KV_RANK, HD)),
            const((C_KV_RANK, WIDTH)), const((1, C_HEAD_PAD)), const((1, C_HEAD_PAD)),
            const((C_HEAD_PAD, C_HEAD_PAD)), const((C_HEAD_PAD, C_HEAD_PAD)),
        ],
        out_specs=[
            pl.BlockSpec((ATT_BLOCK, HD), lambda b, c: (b * nblk + c, 0)),
            pl.BlockSpec((1, 1, ATT_BLOCK, HD), lambda b, c: (b, c, 0, 0)),
            pl.BlockSpec((1, 1, ATT_HEADS * V_AUG, ATT_BLOCK), lambda b, c: (b, c, 0, 0)),
        ],
        out_shape=[
            jax.ShapeDtypeStruct((B * S, HD), bf16),
            jax.ShapeDtypeStruct((B, nblk, ATT_BLOCK, HD), bf16),
            jax.ShapeDtypeStruct((B, nblk, ATT_HEADS * V_AUG, ATT_BLOCK), bf16),
        ],
        compiler_params=_cparams("parallel", "parallel"),
        name="mla_prep",
    )(proj, proj, proj, cos_c, sin_c, cqg, ckvg, wuq, wuk, wuv, qg, kg, gsum, rot)


def _attn_kernel(q_ref, k_ref, vt_ref, o_ref, *scratch):
    H = ATT_HEADS
    s_scr, p_scr, acc_scr, m_scr, a_scr = (scratch[n * H:(n + 1) * H] for n in range(5))
    out_scr = scratch[5 * H]
    n_past = pl.program_id(1) * (ATT_QBLOCK // ATT_BLOCK)
    nt = (((1,), (1,)), ((), ()))
    chunks = ATT_BLOCK // SM_ROWS

    def scores(j):
        for h in range(H):
            lanes = slice(h * HEAD_SLOT, (h + 1) * HEAD_SLOT)
            s_scr[h][...] = lax.dot_general(k_ref[0, j, :, lanes], q_ref[:, lanes], nt,
                                            preferred_element_type=f32)

    def softmax(diag_offset):
        for h in range(H):
            def rows(r):
                s = s_scr[h][r * SM_ROWS:(r + 1) * SM_ROWS, :]
                if diag_offset is not None:
                    kpos = _iota((SM_ROWS, ATT_QBLOCK), 0) + (r * SM_ROWS + diag_offset)
                    s = jnp.where(kpos <= _iota((SM_ROWS, ATT_QBLOCK), 1), s, NEG_INF)
                return s
            mx = rows(0)
            for r in range(1, chunks):
                mx = jnp.maximum(mx, rows(r))
            m_old = m_scr[h][...]
            m_new = jnp.maximum(m_old, jnp.max(mx, axis=0, keepdims=True))
            a_scr[h][...] = jnp.exp2(m_old - m_new)
            m_scr[h][...] = m_new
            for r in range(chunks):
                p_scr[h][r * SM_ROWS:(r + 1) * SM_ROWS, :] = jnp.exp2(rows(r) - m_new).astype(bf16)

    def values(j):
        for h in range(H):
            pv = jnp.dot(vt_ref[0, j, h * V_AUG:(h + 1) * V_AUG, :], p_scr[h][...],
                         preferred_element_type=f32)
            acc_scr[h][...] = a_scr[h][...] * acc_scr[h][...] + pv

    for h in range(H):
        m_scr[h][...] = jnp.full(m_scr[h].shape, NEG_INF, f32)
        a_scr[h][...] = jnp.zeros(a_scr[h].shape, f32)
        acc_scr[h][...] = jnp.zeros(acc_scr[h].shape, f32)
        p_scr[h][...] = jnp.zeros(p_scr[h].shape, bf16)
    scores(0)

    def past_block(b, carry):
        values(jnp.maximum(b - 1, 0))
        softmax(None)
        scores(b + 1)
        return carry

    lax.fori_loop(0, n_past, past_block, 0)
    values(jnp.maximum(n_past - 1, 0))
    softmax(0)
    scores(n_past + 1)
    values(n_past)
    softmax(ATT_BLOCK)
    values(n_past + 1)

    for h in range(H):
        a = acc_scr[h][...]
        out_scr[h * C_V_DIM:(h + 1) * C_V_DIM, :] = a[:C_V_DIM] / a[C_V_DIM:C_V_DIM + 1]
    o_ref[...] = out_scr[...].T.astype(bf16)


def _attention(q, k, vt, B, S, name):
    nblk = S // ATT_BLOCK
    nq = S // ATT_QBLOCK
    HD = ATT_HEADS * HEAD_SLOT
    return pl.pallas_call(
        _attn_kernel,
        grid=(B, nq),
        in_specs=[
            pl.BlockSpec((ATT_QBLOCK, HD), lambda b, i: (b * nq + i, 0)),
            pl.BlockSpec((1, nblk, ATT_BLOCK, HD), lambda b, i: (b, 0, 0, 0)),
            pl.BlockSpec((1, nblk, ATT_HEADS * V_AUG, ATT_BLOCK), lambda b, i: (b, 0, 0, 0)),
        ],
        out_specs=pl.BlockSpec((ATT_QBLOCK, WIDTH), lambda b, i: (b * nq + i, 0)),
        out_shape=jax.ShapeDtypeStruct((B * S, WIDTH), bf16),
        scratch_shapes=(
            [pltpu.VMEM((ATT_BLOCK, ATT_QBLOCK), f32) for _ in range(ATT_HEADS)]
            + [pltpu.VMEM((ATT_BLOCK, ATT_QBLOCK), bf16) for _ in range(ATT_HEADS)]
            + [pltpu.VMEM((V_AUG, ATT_QBLOCK), f32) for _ in range(ATT_HEADS)]
            + [pltpu.VMEM((1, ATT_QBLOCK), f32) for _ in range(2 * ATT_HEADS)]
            + [pltpu.VMEM((WIDTH, ATT_QBLOCK), f32)]),
        compiler_params=_cparams("parallel", "arbitrary"),
        name=name,
    )(q, k, vt)


ROUTE_E1, ROUTE_E2, ROUTE_W1, ROUTE_W2, ROUTE_R1, ROUTE_R2 = range(6)


def _route_rows(h, wr_ref, br_ref, route_ref, count_ref, carry_scr):
    tm = h.shape[0]

    @pl.when(pl.program_id(0) == 0)
    def _():
        carry_scr[...] = jnp.zeros(carry_scr.shape, f32)

    lane = _iota((tm, LANES), 1)
    h_hi = h.astype(bf16)
    h_lo = (h - h_hi.astype(f32)).astype(bf16)
    w = wr_ref[...]
    w_hi = w.astype(bf16)
    w_lo = (w - w_hi.astype(f32)).astype(bf16)
    logits = (jnp.dot(h_hi, w_hi, preferred_element_type=f32)
              + (jnp.dot(h_hi, w_lo, preferred_element_type=f32)
                 + jnp.dot(h_lo, w_hi, preferred_element_type=f32))) + br_ref[...]
    logits = jnp.where(lane < N_EXPERTS, logits, NEG_INF)
    m1 = jnp.max(logits, axis=-1, keepdims=True)
    e1 = jnp.min(jnp.where(logits == m1, lane, LANES), axis=-1, keepdims=True)
    rest = jnp.where(lane == e1, NEG_INF, logits)
    m2 = jnp.max(rest, axis=-1, keepdims=True)
    e2 = jnp.min(jnp.where(rest == m2, lane, LANES), axis=-1, keepdims=True)
    ex = jnp.exp(m2 - m1)
    w1 = 1.0 / (1.0 + ex)
    w2 = ex / (1.0 + ex)
    chosen = ((lane == e1) | (lane == e2)).astype(f32)
    earlier = (_iota((tm, tm), 1) < _iota((tm, tm), 0)).astype(bf16)
    before = jnp.dot(earlier, chosen.astype(bf16), preferred_element_type=f32) + carry_scr[...]
    r1 = jnp.sum(jnp.where(lane == e1, before, 0.0), axis=-1, keepdims=True)
    r2 = jnp.sum(jnp.where(lane == e2, before, 0.0), axis=-1, keepdims=True)
    out = jnp.zeros((tm, LANES), f32)
    for idx, val in ((ROUTE_E1, e1.astype(f32)), (ROUTE_E2, e2.astype(f32)), (ROUTE_W1, w1),
                     (ROUTE_W2, w2), (ROUTE_R1, r1), (ROUTE_R2, r2)):
        out = jnp.where(lane == idx, val, out)
    route_ref[...] = out
    carry_scr[...] += jnp.sum(chosen, axis=0, keepdims=True)
    count_ref[...] = carry_scr[...]


def _merge_kernel(x_ref, g0_ref, g1_ref, g2_ref, g3_ref, oa_ref, ob_ref, oc_ref, od_ref,
                  wb_ref, wo_ref, ng_ref, *rest, routed):
    y = None
    for n, (g_ref, o_ref) in enumerate(((g0_ref, oa_ref), (g1_ref, ob_ref),
                                        (g2_ref, oc_ref), (g3_ref, od_ref))):
        br = jnp.dot(o_ref[...], wb_ref[n], preferred_element_type=f32)
        t = _sigmoid(g_ref[...].astype(f32)) * br
        y = t if y is None else y + t
    x = x_ref[...] + jnp.dot(y.astype(bf16), wo_ref[...], preferred_element_type=f32)
    h = _rms(x, D_MODEL) * ng_ref[...]
    if routed:
        wr_ref, br_ref, xo_ref, ho_ref, route_ref, count_ref, carry_scr = rest
        ho_ref[...] = h
        _route_rows(h, wr_ref, br_ref, route_ref, count_ref, carry_scr)
    else:
        xo_ref, ho_ref = rest
        ho_ref[...] = h.astype(bf16)
    xo_ref[...] = x


def _merge(x2, proj, oa, ob, oc, od, wb, wo, ng, router=None):
    T = x2.shape[0]
    tm = min(512, T)
    routed = router is not None
    gate = lambda n: pl.BlockSpec((tm, D_MODEL), lambda t: (t, n))
    branch = pl.BlockSpec((tm, WIDTH), lambda t: (t, 0))
    rows = pl.BlockSpec((tm, D_MODEL), lambda t: (t, 0))
    in_specs = [
        rows, gate(0), gate(1), gate(2), gate(3), branch, branch, branch, branch,
        pl.BlockSpec((N_BRANCH, WIDTH, D_MODEL), lambda t: (0, 0, 0)),
        pl.BlockSpec((D_MODEL, D_MODEL), lambda t: (0, 0)),
        pl.BlockSpec((1, D_MODEL), lambda t: (0, 0)),
    ]
    args = [x2, proj, proj, proj, proj, oa, ob, oc, od, wb, wo, ng]
    out_specs = [rows, rows]
    out_shape = [jax.ShapeDtypeStruct((T, D_MODEL), f32),
                 jax.ShapeDtypeStruct((T, D_MODEL), f32 if routed else bf16)]
    scratch = []
    if routed:
        in_specs += [pl.BlockSpec((D_MODEL, LANES), lambda t: (0, 0)),
                     pl.BlockSpec((1, LANES), lambda t: (0, 0))]
        args += list(router)
        out_specs += [pl.BlockSpec((tm, LANES), lambda t: (t, 0)),
                      pl.BlockSpec((1, LANES), lambda t: (0, 0))]
        out_shape += [jax.ShapeDtypeStruct((T, LANES), f32), jax.ShapeDtypeStruct((1, LANES), f32)]
        scratch = [pltpu.VMEM((1, LANES), f32)]
    return pl.pallas_call(
        functools.partial(_merge_kernel, routed=routed),
        grid=(T // tm,),
        in_specs=in_specs,
        out_specs=out_specs,
        out_shape=out_shape,
        scratch_shapes=scratch,
        compiler_params=_cparams("arbitrary" if routed else "parallel"),
        name="merge_route" if routed else "merge",
    )(*args)


def _swiglu_block(h, wg, wu, wd):
    a = jnp.dot(h, wg, preferred_element_type=f32)
    b = jnp.dot(h, wu, preferred_element_type=f32)
    t = (a * _sigmoid(a) * b).astype(bf16)
    return jnp.dot(t, wd, preferred_element_type=f32)


def _ffn_kernel(x_ref, h_ref, wg_ref, wu_ref, wd_ref, o_ref, acc_ref):
    f = pl.program_id(1)
    y = _swiglu_block(h_ref[...], wg_ref[...], wu_ref[...], wd_ref[...])

    @pl.when(f == 0)
    def _():
        acc_ref[...] = x_ref[...] + y

    @pl.when(f > 0)
    def _():
        acc_ref[...] += y

    @pl.when(f == pl.num_programs(1) - 1)
    def _():
        o_ref[...] = acc_ref[...]


def _ffn(x2, h2, wg, wu, wd):
    T = x2.shape[0]
    tm = min(1024, T)
    tf = EXPERT_DIM
    return pl.pallas_call(
        _ffn_kernel,
        grid=(T // tm, FFN_DIM // tf),
        in_specs=[
            pl.BlockSpec((tm, D_MODEL), lambda t, f: (t, 0)),
            pl.BlockSpec((tm, D_MODEL), lambda t, f: (t, 0)),
            pl.BlockSpec((D_MODEL, tf), lambda t, f: (0, f)),
            pl.BlockSpec((D_MODEL, tf), lambda t, f: (0, f)),
            pl.BlockSpec((tf, D_MODEL), lambda t, f: (f, 0)),
        ],
        out_specs=pl.BlockSpec((tm, D_MODEL), lambda t, f: (t, 0)),
        out_shape=jax.ShapeDtypeStruct((T, D_MODEL), f32),
        scratch_shapes=[pltpu.VMEM((tm, D_MODEL), f32)],
        compiler_params=_cparams("parallel", "arbitrary"),
        name="dense_swiglu",
    )(x2, h2, wg, wu, wd)


def _row_copy(src_ref, src_row, dst_ref, dst_row, sem):
    return pltpu.make_async_copy(src_ref.at[pl.ds(src_row, 1)], dst_ref.at[pl.ds(dst_row, 1)], sem)


def _dispatch_kernel(pends_ref, dest_ref, h_ref, xs_ref, zero_scr, sem, zero_sem, *, tm):
    def zero_fill(start):
        start = pl.multiple_of(start, EXPERT_ROWS)
        return pltpu.make_async_copy(zero_scr, xs_ref.at[pl.ds(start, EXPERT_ROWS)], zero_sem)

    @pl.when(pl.program_id(0) == 0)
    def _():
        zero_scr[...] = jnp.zeros(zero_scr.shape, f32)
        used = pends_ref[N_EXPERTS - 1]
        fills = [(jnp.maximum(pends_ref[e] - EXPERT_ROWS, 0), None) for e in range(N_EXPERTS)]
        fills += [(used + e * EXPERT_ROWS, used + e * EXPERT_ROWS < xs_ref.shape[0])
                  for e in range(N_EXPERTS)]
        for action in ("start", "wait"):
            for start, needed in fills:
                if needed is None:
                    getattr(zero_fill(start), action)()
                else:
                    @pl.when(needed)
                    def _():
                        getattr(zero_fill(start), action)()

    def issue(r, carry):
        _row_copy(h_ref, r, xs_ref, dest_ref[0, 0, 2 * r], sem).start()
        _row_copy(h_ref, r, xs_ref, dest_ref[0, 0, 2 * r + 1], sem).start()
        return carry

    lax.fori_loop(0, tm, issue, 0)
    for _ in range(2):
        pltpu.make_async_copy(h_ref, xs_ref.at[pl.ds(0, tm)], sem).wait()


def _dispatch(h2, dest3, pends, rows):
    T = h2.shape[0]
    tm = dest3.shape[-1] // 2
    grid_spec = pltpu.PrefetchScalarGridSpec(
        num_scalar_prefetch=1,
        grid=(T // tm,),
        in_specs=[
            pl.BlockSpec((1, 1, 2 * tm), lambda t, pe: (t, 0, 0), memory_space=pltpu.SMEM),
            pl.BlockSpec((tm, D_MODEL), lambda t, pe: (t, 0)),
        ],
        out_specs=pl.BlockSpec(memory_space=pl.ANY),
        scratch_shapes=[pltpu.VMEM((EXPERT_ROWS, D_MODEL), f32),
                        pltpu.SemaphoreType.DMA(()), pltpu.SemaphoreType.DMA(())],
    )
    return pl.pallas_call(
        functools.partial(_dispatch_kernel, tm=tm),
        grid_spec=grid_spec,
        out_shape=jax.ShapeDtypeStruct((rows, D_MODEL), f32),
        compiler_params=_cparams("arbitrary"),
        name="moe_dispatch",
    )(pends, dest3, h2)


def _expert_kernel(be_ref, nv_ref, xs_ref, wg_ref, wu_ref, wd_ref, ys_ref):
    j = pl.program_id(0)

    @pl.when(j < nv_ref[0])
    def _():
        ys_ref[...] = _swiglu_block(xs_ref[...].astype(bf16), wg_ref[0], wu_ref[0], wd_ref[0])

    @pl.when(j >= nv_ref[0])
    def _():
        ys_ref[...] = jnp.zeros(ys_ref.shape, f32)


def _experts(block_e, n_valid, xs, wg, wu, wd):
    rows = xs.shape[0]
    grid_spec = pltpu.PrefetchScalarGridSpec(
        num_scalar_prefetch=2,
        grid=(rows // EXPERT_ROWS,),
        in_specs=[
            pl.BlockSpec((EXPERT_ROWS, D_MODEL), lambda j, be, nv: (jnp.minimum(j, nv[0] - 1), 0)),
            pl.BlockSpec((1, D_MODEL, EXPERT_DIM), lambda j, be, nv: (be[j], 0, 0)),
            pl.BlockSpec((1, D_MODEL, EXPERT_DIM), lambda j, be, nv: (be[j], 0, 0)),
            pl.BlockSpec((1, EXPERT_DIM, D_MODEL), lambda j, be, nv: (be[j], 0, 0)),
        ],
        out_specs=pl.BlockSpec((EXPERT_ROWS, D_MODEL), lambda j, be, nv: (j, 0)),
    )
    return pl.pallas_call(
        _expert_kernel,
        grid_spec=grid_spec,
        out_shape=jax.ShapeDtypeStruct((rows, D_MODEL), f32),
        compiler_params=_cparams("arbitrary"),
        name="moe_experts",
    )(block_e, n_valid, xs, wg, wu, wd)


def _combine_kernel(dest_ref, x_ref, route_ref, ys_ref, o_ref, buf, sem, *, tm):
    def issue(r, carry):
        _row_copy(ys_ref, dest_ref[0, 0, 2 * r], buf.at[0], r, sem).start()
        _row_copy(ys_ref, dest_ref[0, 0, 2 * r + 1], buf.at[1], r, sem).start()
        return carry

    lax.fori_loop(0, tm, issue, 0)
    for slot in range(2):
        pltpu.make_async_copy(ys_ref.at[pl.ds(0, tm)], buf.at[slot], sem).wait()
    route = route_ref[...]
    w1 = route[:, ROUTE_W1:ROUTE_W1 + 1]
    w2 = route[:, ROUTE_W2:ROUTE_W2 + 1]
    o_ref[...] = x_ref[...] + w1 * buf[0] + w2 * buf[1]


def _combine(x2, route, dest3, ys):
    T = x2.shape[0]
    tm = dest3.shape[-1] // 2
    return pl.pallas_call(
        functools.partial(_combine_kernel, tm=tm),
        grid=(T // tm,),
        in_specs=[
            pl.BlockSpec((1, 1, 2 * tm), lambda t: (t, 0, 0), memory_space=pltpu.SMEM),
            pl.BlockSpec((tm, D_MODEL), lambda t: (t, 0)),
            pl.BlockSpec((tm, LANES), lambda t: (t, 0)),
            pl.BlockSpec(memory_space=pl.ANY),
        ],
        out_specs=pl.BlockSpec((tm, D_MODEL), lambda t: (t, 0)),
        out_shape=jax.ShapeDtypeStruct((T, D_MODEL), f32),
        scratch_shapes=[pltpu.VMEM((2, tm, D_MODEL), f32), pltpu.SemaphoreType.DMA(())],
        compiler_params=_cparams("arbitrary"),
        name="moe_combine",
    )(dest3, x2, route, ys)


def _moe(x2, h2, route, counts, wg, wu, wd):
    T = x2.shape[0]
    tm = min(256, T)
    counts = counts[0, :N_EXPERTS].astype(jnp.int32)
    padded = (counts + EXPERT_ROWS - 1) // EXPERT_ROWS * EXPERT_ROWS
    pends = jnp.cumsum(padded)
    pstarts = pends - padded
    n_blocks = -(-(2 * T) // EXPERT_ROWS) + N_EXPERTS
    block_e = jnp.minimum(
        jnp.searchsorted(pends, jnp.arange(n_blocks, dtype=jnp.int32) * EXPERT_ROWS, side='right'),
        N_EXPERTS - 1).astype(jnp.int32)
    n_valid = (pends[-1:] // EXPERT_ROWS).astype(jnp.int32)
    e12 = route[:, ROUTE_E1:ROUTE_E2 + 1].astype(jnp.int32)
    r12 = route[:, ROUTE_R1:ROUTE_R2 + 1].astype(jnp.int32)
    dest = pstarts[e12] + r12
    dest3 = dest.reshape(T // tm, 1, 2 * tm)
    xs = _dispatch(h2, dest3, pends.astype(jnp.int32), n_blocks * EXPERT_ROWS)
    ys = _experts(block_e, n_valid, xs, wg, wu, wd)
    return _combine(x2, route, dest3, ys)


def _pad_cols(w, n):
    return jnp.pad(w, ((0, 0), (0, n - w.shape[1])))


def _layer_weights(w_in, c_w_uq, c_w_ukv):
    gate_end = N_BRANCH * D_MODEL
    a_end = gate_end + 2 * WIDTH
    b_end = a_end + 3 * WIDTH
    cq = w_in[:, b_end:b_end + C_Q_RANK]
    ckv = w_in[:, b_end + C_Q_RANK:b_end + C_Q_RANK + C_KV_RANK]
    kr = w_in[:, b_end + C_Q_RANK + C_KV_RANK:b_end + C_Q_RANK + C_KV_RANK + C_ROPE_DIM]
    d_start = b_end + C_Q_RANK + C_KV_RANK + C_ROPE_DIM
    kr_block = jnp.pad(kr, ((0, 0), (C_NOPE_DIM, C_HEAD_PAD - C_QK_DIM)))
    w = jnp.concatenate([w_in[:, :b_end], _pad_cols(cq, WIDTH), ckv, kr_block, w_in[:, d_start:]],
                        axis=1).astype(bf16)
    wuq = c_w_uq.reshape(C_Q_RANK, C_HEADS, C_QK_DIM)
    wuq = jnp.pad(wuq, ((0, WIDTH - C_Q_RANK), (0, 0), (0, C_HEAD_PAD - C_QK_DIM)))
    wuq = wuq.reshape(WIDTH, C_HEADS * C_HEAD_PAD).astype(bf16)
    wukv = c_w_ukv.reshape(C_KV_RANK, C_HEADS, C_NOPE_DIM + C_V_DIM)
    wuk = jnp.pad(wukv[:, :, :C_NOPE_DIM], ((0, 0), (0, 0), (0, C_HEAD_PAD - C_NOPE_DIM)))
    wuk = wuk.reshape(C_KV_RANK, C_HEADS * C_HEAD_PAD).astype(bf16)
    wuv = wukv[:, :, C_NOPE_DIM:].reshape(C_KV_RANK, C_HEADS * C_V_DIM).astype(bf16)
    return w, wuq, wuk, wuv


def _rotation_constants():
    ch = jnp.arange(WIDTH)
    gsum_b = ((ch[:, None] // B_HEAD_DIM) == (ch[None, :] // B_HEAD_DIM)).astype(f32) / B_HEAD_DIM
    half_b = B_ROT_DIM // 2
    src, dst = ch[:, None], ch[None, :]
    same = (src // B_HEAD_DIM) == (dst // B_HEAD_DIM)
    ds_, dd = src % B_HEAD_DIM, dst % B_HEAD_DIM
    rot_b = jnp.where(same & (dd < half_b) & (ds_ == dd + half_b), -1.0,
                      jnp.where(same & (dd >= half_b) & (dd < B_ROT_DIM) & (ds_ == dd - half_b), 1.0, 0.0))
    lc = jnp.arange(C_HEAD_PAD)
    real = lc < C_QK_DIM
    gsum_c = (real[:, None] & real[None, :]).astype(f32) / C_QK_DIM
    half_c = C_ROPE_DIM // 2
    s, d = lc[:, None], lc[None, :]
    lo = (d >= C_NOPE_DIM) & (d < C_NOPE_DIM + half_c)
    hi = (d >= C_NOPE_DIM + half_c) & (d < C_QK_DIM)
    rot_c = jnp.where(lo & (s == d + half_c), -1.0, jnp.where(hi & (s == d - half_c), 1.0, 0.0))
    slot = jnp.arange(ATT_HEADS * HEAD_SLOT)
    place_b = ((slot[None, :] // HEAD_SLOT == ch[:, None] // B_HEAD_DIM)
               & (slot[None, :] % HEAD_SLOT == ch[:, None] % B_HEAD_DIM))
    return (gsum_b.astype(bf16), rot_b.astype(bf16), place_b.astype(bf16),
            gsum_c.astype(bf16), rot_c.astype(bf16))


def kernel(x, positions, attn_norm, ffn_norm, w_in, a_ln_g, a_ln_b, a_ws, a_bs, b_qn, b_kn, c_cq_norm, c_ckv_norm, c_w_uq, c_w_ukv, c_qn, c_kn, d_conv_w, d_conv_b, d_ln_g, d_ln_b, w_branch, w_out, ffn_w_gate, ffn_w_up, ffn_w_down, moe_router, moe_router_b, moe_w_gate, moe_w_up, moe_w_down):
    B, S, _ = x.shape
    T = B * S
    assert S % ATT_BLOCK == 0 and T % EXPERT_ROWS == 0
    x2 = x.reshape(T, D_MODEL)
    cos_b, sin_b, cos_c, sin_c = _rope_tables(positions)
    gsum_b, rot_b, place_b, gsum_c, rot_c = _rotation_constants()
    row = lambda v: v.reshape(1, -1)

    for l in range(DEPTH):
        w, wuq, wuk, wuv = _layer_weights(w_in[l], c_w_uq[l], c_w_ukv[l])
        proj = _inproj(x2, row(attn_norm[l]), w)

        o_a = _sgu(proj, row(a_ln_g[l]), row(a_ln_b[l]), a_ws[l], a_bs[l].T)

        qb, kb, vtb = _moba_prep(proj, B, S, cos_b, sin_b,
                                 row(jnp.tile(b_qn[l], B_HEADS)), row(jnp.tile(b_kn[l], B_HEADS)),
                                 gsum_b, rot_b, place_b)
        o_b = _attention(qb, kb, vtb, B, S, "moba_attention")

        pad_gain = lambda g, n: row(jnp.pad(g, (0, n - g.shape[0])))
        qc, kc, vtc = _mla_prep(proj, B, S, cos_c, sin_c, pad_gain(c_cq_norm[l], WIDTH),
                                row(c_ckv_norm[l]), wuq, wuk, wuv,
                                pad_gain(c_qn[l], C_HEAD_PAD), pad_gain(c_kn[l], C_HEAD_PAD),
                                gsum_c, rot_c)
        o_c = _attention(qc, kc, vtc, B, S, "mla_attention")

        o_d = _conformer(proj, B, S, d_conv_w[l], row(d_conv_b[l]), row(d_ln_g[l]), row(d_ln_b[l]))

        merge_args = (x2, proj, o_a, o_b, o_c, o_d, w_branch[l].astype(bf16),
                      w_out[l].astype(bf16), row(ffn_norm[l]))
        i = l // 2
        if l % 2 == 0:
            x2, h2 = _merge(*merge_args)
            x2 = _ffn(x2, h2, ffn_w_gate[i].astype(bf16), ffn_w_up[i].astype(bf16),
                      ffn_w_down[i].astype(bf16))
        else:
            router = (_pad_cols(moe_router[i], LANES), _pad_cols(row(moe_router_b[i]), LANES))
            x2, h2, route, counts = _merge(*merge_args, router=router)
            x2 = _moe(x2, h2, route, counts, moe_w_gate[i].astype(bf16),
                      moe_w_up[i].astype(bf16), moe_w_down[i].astype(bf16))
    return x2.reshape(B, S, D_MODEL)
```

```python
import functools

import jax
import jax.numpy as jnp
from jax import lax
from jax.experimental import pallas as pl
from jax.experimental.pallas import tpu as pltpu

D_MODEL = 1024
DEPTH = 4
N_BRANCH = 4
ROPE_THETA = 500000.0
NORM_EPS = 1e-6
NEG_INF = -1e30

WIDTH = 256
A_GROUPS = 4
A_GROUP_DIM = 64
A_CHUNK = 128
B_HEADS = 4
B_HEAD_DIM = 64
B_ROT_DIM = 16
MOBA_BLOCK = 256
MOBA_TOPK = 3
C_HEADS = 4
C_Q_RANK = 192
C_KV_RANK = 128
C_NOPE_DIM = 64
C_ROPE_DIM = 32
C_V_DIM = 64
C_QK_DIM = C_NOPE_DIM + C_ROPE_DIM
C_HEAD_PAD = 128
D_CONV = 31
FFN_DIM = 2816
N_EXPERTS = 8
EXPERT_DIM = 1408
EXPERT_ROWS = 512

ATT_BLOCK = 256
ATT_QBLOCK = 512
ATT_HEADS = 4
HEAD_SLOT = 128
V_AUG = C_V_DIM + 16
LOG2E = 1.4426950408889634
SM_ROWS = 32
LANES = 128
SUBLANES = 8
CONV_HALO = 32
CONV_ROWS = 64
LN_ROWS = 128

COL_GATE = 0
COL_AU = 4096
COL_AV = 4352
COL_BQ = 4608
COL_BK = 4864
COL_BV = 5120
COL_CQ = 5376
COL_CKV = 5632
COL_KR = 5760
COL_DA = 5888
COL_DG = 6144
PROJ_COLS = 6400

VMEM_LIMIT = 56 * 1024 * 1024

bf16 = jnp.bfloat16
f32 = jnp.float32


def _cparams(*sem):
    return pltpu.CompilerParams(dimension_semantics=sem, vmem_limit_bytes=VMEM_LIMIT)


def _rms(x, n):
    return x * lax.rsqrt(jnp.sum(x * x, axis=-1, keepdims=True) * (1.0 / n) + NORM_EPS)


def _layernorm(x, g, b):
    mu = jnp.mean(x, axis=-1, keepdims=True)
    xc = x - mu
    return xc * lax.rsqrt(jnp.mean(xc * xc, axis=-1, keepdims=True) + NORM_EPS) * g + b


def _sigmoid(x):
    return 0.5 * jnp.tanh(0.5 * x) + 0.5


def _iota(shape, dim):
    return lax.broadcasted_iota(jnp.int32, shape, dim)


def _rope_table_kernel(pos_ref, inv_ref, eb_ref, ec_ref, cb_ref, sb_ref, cc_ref, sc_ref):
    ang = pos_ref[0].astype(f32) * inv_ref[...]
    c = jnp.cos(ang)
    s = jnp.sin(ang)
    hi = lax.Precision.HIGHEST
    cb_ref[0] = jnp.dot(c, eb_ref[...], precision=hi, preferred_element_type=f32)
    sb_ref[0] = jnp.dot(s, eb_ref[...], precision=hi, preferred_element_type=f32)
    cc_ref[0] = jnp.dot(c, ec_ref[...], precision=hi, preferred_element_type=f32)
    sc_ref[0] = jnp.dot(s, ec_ref[...], precision=hi, preferred_element_type=f32)


def _rope_tables(positions):
    B, S = positions.shape
    R = min(512, S)
    nb = B_ROT_DIM // 2
    nc = C_ROPE_DIM // 2
    inv_b = f32(ROPE_THETA) ** (-jnp.arange(0, B_ROT_DIM, 2, dtype=f32) / B_ROT_DIM)
    inv_c = f32(ROPE_THETA) ** (-jnp.arange(0, C_ROPE_DIM, 2, dtype=f32) / C_ROPE_DIM)
    inv = jnp.zeros((1, LANES), f32).at[0, :nb].set(inv_b).at[0, nb:nb + nc].set(inv_c)
    zero_lane = LANES - 1
    ch = jnp.arange(WIDTH)
    d = ch % B_HEAD_DIM
    src_b = jnp.where(d < B_ROT_DIM, d % nb, zero_lane)
    eb = (jnp.arange(LANES)[:, None] == src_b[None, :]).astype(f32)
    dc = jnp.arange(C_HEAD_PAD)
    in_rope = (dc >= C_NOPE_DIM) & (dc < C_QK_DIM)
    src_c = jnp.where(in_rope, nb + (dc - C_NOPE_DIM) % nc, zero_lane)
    ec = (jnp.arange(LANES)[:, None] == src_c[None, :]).astype(f32)
    pos3 = positions.reshape(B, S, 1)
    out = pl.pallas_call(
        _rope_table_kernel,
        grid=(B, S // R),
        in_specs=[
            pl.BlockSpec((1, R, 1), lambda b, r: (b, r, 0)),
            pl.BlockSpec((1, LANES), lambda b, r: (0, 0)),
            pl.BlockSpec((LANES, WIDTH), lambda b, r: (0, 0)),
            pl.BlockSpec((LANES, C_HEAD_PAD), lambda b, r: (0, 0)),
        ],
        out_specs=[
            pl.BlockSpec((1, R, WIDTH), lambda b, r: (b, r, 0)),
            pl.BlockSpec((1, R, WIDTH), lambda b, r: (b, r, 0)),
            pl.BlockSpec((1, R, C_HEAD_PAD), lambda b, r: (b, r, 0)),
            pl.BlockSpec((1, R, C_HEAD_PAD), lambda b, r: (b, r, 0)),
        ],
        out_shape=[
            jax.ShapeDtypeStruct((B, S, WIDTH), f32),
            jax.ShapeDtypeStruct((B, S, WIDTH), f32),
            jax.ShapeDtypeStruct((B, S, C_HEAD_PAD), f32),
            jax.ShapeDtypeStruct((B, S, C_HEAD_PAD), f32),
        ],
        compiler_params=_cparams("parallel", "parallel"),
        name="rope_tables",
    )(pos3, inv, eb, ec)
    return out


def _inproj_kernel(x_ref, g_ref, w_ref, o_ref, *, col_chunk):
    h = (_rms(x_ref[...], D_MODEL) * g_ref[...]).astype(bf16)
    for c in range(PROJ_COLS // col_chunk):
        sl = slice(c * col_chunk, (c + 1) * col_chunk)
        o_ref[:, sl] = jnp.dot(h, w_ref[:, sl], preferred_element_type=f32).astype(bf16)


def _inproj(x2, g, w):
    T = x2.shape[0]
    tm = min(512, T)
    return pl.pallas_call(
        functools.partial(_inproj_kernel, col_chunk=1280),
        grid=(T // tm,),
        in_specs=[
            pl.BlockSpec((tm, D_MODEL), lambda t: (t, 0)),
            pl.BlockSpec((1, D_MODEL), lambda t: (0, 0)),
            pl.BlockSpec((D_MODEL, PROJ_COLS), lambda t: (0, 0)),
        ],
        out_specs=pl.BlockSpec((tm, PROJ_COLS), lambda t: (t, 0)),
        out_shape=jax.ShapeDtypeStruct((T, PROJ_COLS), bf16),
        compiler_params=_cparams("parallel"),
        name="inproj",
    )(x2, g, w)


def _sgu_kernel(u_ref, v_ref, g_ref, b_ref, ws_ref, bst_ref, o_ref, *, chunks):
    row = _iota((A_CHUNK, A_CHUNK), 0)
    col = _iota((A_CHUNK, A_CHUNK), 1)
    causal = col <= row
    group = _iota((A_CHUNK, WIDTH), 1) // A_GROUP_DIM
    wm = [jnp.where(causal, ws_ref[g], 0.0).astype(bf16) for g in range(A_GROUPS)]
    bias = jnp.zeros((A_CHUNK, WIDTH), f32)
    for g in range(A_GROUPS):
        bias = jnp.where(group == g, bst_ref[:, g:g + 1], bias)
    for c in range(chunks):
        rows = slice(c * A_CHUNK, (c + 1) * A_CHUNK)
        vn = _layernorm(v_ref[rows, :].astype(f32), g_ref[...], b_ref[...]).astype(bf16)
        z = bias
        for g in range(A_GROUPS):
            zg = jnp.dot(wm[g], vn, preferred_element_type=f32)
            z = z + jnp.where(group == g, zg, 0.0)
        o_ref[rows, :] = (u_ref[rows, :].astype(f32) * z).astype(bf16)


def _sgu(proj, ln_g, ln_b, ws, bs_t):
    T = proj.shape[0]
    ta = min(512, T)
    cu, cv = COL_AU // WIDTH, COL_AV // WIDTH
    return pl.pallas_call(
        functools.partial(_sgu_kernel, chunks=ta // A_CHUNK),
        grid=(T // ta,),
        in_specs=[
            pl.BlockSpec((ta, WIDTH), lambda t: (t, cu)),
            pl.BlockSpec((ta, WIDTH), lambda t: (t, cv)),
            pl.BlockSpec((1, WIDTH), lambda t: (0, 0)),
            pl.BlockSpec((1, WIDTH), lambda t: (0, 0)),
            pl.BlockSpec((A_GROUPS, A_CHUNK, A_CHUNK), lambda t: (0, 0, 0)),
            pl.BlockSpec((A_CHUNK, A_GROUPS), lambda t: (0, 0)),
        ],
        out_specs=pl.BlockSpec((ta, WIDTH), lambda t: (t, 0)),
        out_shape=jax.ShapeDtypeStruct((T, WIDTH), bf16),
        compiler_params=_cparams("parallel"),
        name="sgu",
    )(proj, proj, ln_g, ln_b, ws, bs_t)


def _conv_kernel(a_ref, g_ref, w_ref, cb_ref, lg_ref, lb_ref, o_ref, z_scr, *, seq):
    first = CONV_HALO - (D_CONV - 1)
    z_scr[0, 0:CONV_HALO, :] = jnp.zeros((CONV_HALO, WIDTH), f32)
    z_scr[0, CONV_HALO + seq:, :] = jnp.zeros((CONV_ROWS, WIDTH), f32)

    def glu(c, carry):
        base = pl.multiple_of(c * LN_ROWS, LN_ROWS)
        rows = pl.ds(base, LN_ROWS)
        z_scr[0, pl.ds(base + CONV_HALO, LN_ROWS), :] = (
            a_ref[rows, :].astype(f32) * _sigmoid(g_ref[rows, :].astype(f32)))
        return carry

    lax.fori_loop(0, seq // LN_ROWS, glu, 0)

    def shift(c, carry):
        base = pl.multiple_of(c * CONV_ROWS, CONV_ROWS)
        win = z_scr[0, pl.ds(base, CONV_ROWS + SUBLANES), :]
        for r in range(1, SUBLANES):
            rolled = pltpu.roll(win, shift=CONV_ROWS + SUBLANES - r, axis=0)
            z_scr[r, pl.ds(base, CONV_ROWS), :] = rolled[:CONV_ROWS]
        return carry

    lax.fori_loop(0, seq // CONV_ROWS + 1, shift, 0)

    def taps(c, carry):
        base = pl.multiple_of(c * CONV_ROWS, CONV_ROWS)
        acc = jnp.zeros((CONV_ROWS, WIDTH), f32) + cb_ref[...]
        for k in range(D_CONV):
            off = first + k
            lo = pl.multiple_of(base + off // SUBLANES * SUBLANES, SUBLANES)
            acc = acc + w_ref[k:k + 1, :] * z_scr[off % SUBLANES, pl.ds(lo, CONV_ROWS), :]
        z_scr[1, pl.ds(base, CONV_ROWS), :] = acc
        return carry

    lax.fori_loop(0, seq // CONV_ROWS, taps, 0)

    def norm(c, carry):
        base = pl.multiple_of(c * LN_ROWS, LN_ROWS)
        y = _layernorm(z_scr[1, pl.ds(base, LN_ROWS), :], lg_ref[...], lb_ref[...])
        o_ref[pl.ds(base, LN_ROWS), :] = (y * _sigmoid(y)).astype(bf16)
        return carry

    lax.fori_loop(0, seq // LN_ROWS, norm, 0, unroll=2)


def _conformer(proj, B, S, conv_w, conv_b, ln_g, ln_b):
    ca, cg = COL_DA // WIDTH, COL_DG // WIDTH
    return pl.pallas_call(
        functools.partial(_conv_kernel, seq=S),
        grid=(B,),
        in_specs=[
            pl.BlockSpec((S, WIDTH), lambda b: (b, ca)),
            pl.BlockSpec((S, WIDTH), lambda b: (b, cg)),
            pl.BlockSpec((D_CONV, WIDTH), lambda b: (0, 0)),
            pl.BlockSpec((1, WIDTH), lambda b: (0, 0)),
            pl.BlockSpec((1, WIDTH), lambda b: (0, 0)),
            pl.BlockSpec((1, WIDTH), lambda b: (0, 0)),
        ],
        out_specs=pl.BlockSpec((S, WIDTH), lambda b: (b, 0)),
        out_shape=jax.ShapeDtypeStruct((B * S, WIDTH), bf16),
        scratch_shapes=[pltpu.VMEM((SUBLANES, CONV_HALO + S + CONV_ROWS, WIDTH), f32)],
        compiler_params=_cparams("parallel"),
        name="conformer_conv",
    )(proj, proj, conv_w, conv_b, ln_g, ln_b)


def _value_rows(v_tok):
    vt = v_tok.T
    ones = jnp.ones((V_AUG - C_V_DIM, vt.shape[1]), f32)
    rows = []
    for h in range(ATT_HEADS):
        rows += [vt[h * C_V_DIM:(h + 1) * C_V_DIM], ones]
    return jnp.concatenate(rows, axis=0).astype(bf16)


def _moba_prep_kernel(q_ref, k_ref, v_ref, cos_ref, sin_ref, qg_ref, kg_ref, gsum_ref, rot_ref,
                      place_ref, qo_ref, ko_ref, vt_ref, kmean_scr, *, nblk):
    c = pl.program_id(1)

    @pl.when(c == 0)
    def _():
        kmean_scr[...] = jnp.zeros(kmean_scr.shape, f32)

    cos = cos_ref[0]
    sin = sin_ref[0]

    def norm_rope(x, gain):
        ss = jnp.dot((x * x).astype(bf16), gsum_ref[...], preferred_element_type=f32)
        xn = x * lax.rsqrt(ss + NORM_EPS) * gain
        xr = jnp.dot(xn.astype(bf16), rot_ref[...], preferred_element_type=f32)
        return xn * cos + xr * sin

    q = norm_rope(q_ref[...].astype(f32), qg_ref[...])
    k = norm_rope(k_ref[...].astype(f32), kg_ref[...])
    slot_lane = _iota((ATT_BLOCK, ATT_HEADS * HEAD_SLOT), 1) % HEAD_SLOT
    k_slots = jnp.dot(k.astype(bf16), place_ref[...], preferred_element_type=f32)
    ko_ref[0, 0] = jnp.where(slot_lane == B_HEAD_DIM + c, 1.0, k_slots).astype(bf16)
    vt_ref[0, 0] = _value_rows(v_ref[...].astype(f32))

    head = _iota((nblk, WIDTH), 1) // B_HEAD_DIM
    blk = _iota((nblk, ATT_BLOCK), 0)
    past = blk < c
    kmean = kmean_scr[...]
    zeros_lo = jnp.zeros((B_HEAD_DIM, ATT_BLOCK), f32)
    zeros_hi = jnp.zeros((HEAD_SLOT - B_HEAD_DIM - nblk, ATT_BLOCK), f32)
    bias_rows = []
    for h in range(B_HEADS):
        km = jnp.where(head == h, kmean, 0.0)
        sc = lax.dot_general(km, q, (((1,), (1,)), ((), ())), precision=lax.Precision.HIGHEST,
                             preferred_element_type=f32)
        sc = jnp.where(past, sc, NEG_INF)
        rank = jnp.zeros((nblk, ATT_BLOCK), jnp.int32)
        for j in range(nblk):
            other = sc[j:j + 1, :]
            ahead = (other > sc) | ((other == sc) & (j < blk))
            rank = rank + ahead.astype(jnp.int32)
        visible = (past & (rank < MOBA_TOPK)) | (blk == c)
        bias_rows += [zeros_lo, jnp.where(visible, 0.0, NEG_INF), zeros_hi]
    bias = jnp.concatenate(bias_rows, axis=0).T
    q_slots = jnp.dot((q * (B_HEAD_DIM ** -0.5 * LOG2E)).astype(bf16), place_ref[...],
                      preferred_element_type=f32)
    qo_ref[...] = (q_slots + bias).astype(bf16)

    kmean_scr[pl.ds(c, 1), :] = jnp.mean(k, axis=0, keepdims=True)


def _moba_prep(proj, B, S, cos_b, sin_b, qg, kg, gsum, rot, place):
    nblk = S // ATT_BLOCK
    assert nblk <= HEAD_SLOT - B_HEAD_DIM
    HD = ATT_HEADS * HEAD_SLOT
    cq, ck, cv = COL_BQ // WIDTH, COL_BK // WIDTH, COL_BV // WIDTH
    tile = lambda col: pl.BlockSpec((ATT_BLOCK, WIDTH), lambda b, c: (b * nblk + c, col))
    table = pl.BlockSpec((1, ATT_BLOCK, WIDTH), lambda b, c: (b, c, 0))
    const = lambda shape: pl.BlockSpec(shape, lambda b, c: (0,) * len(shape))
    return pl.pallas_call(
        functools.partial(_moba_prep_kernel, nblk=nblk),
        grid=(B, nblk),
        in_specs=[tile(cq), tile(ck), tile(cv), table, table,
                  const((1, WIDTH)), const((1, WIDTH)), const((WIDTH, WIDTH)), const((WIDTH, WIDTH)),
                  const((WIDTH, HD))],
        out_specs=[
            pl.BlockSpec((ATT_BLOCK, HD), lambda b, c: (b * nblk + c, 0)),
            pl.BlockSpec((1, 1, ATT_BLOCK, HD), lambda b, c: (b, c, 0, 0)),
            pl.BlockSpec((1, 1, ATT_HEADS * V_AUG, ATT_BLOCK), lambda b, c: (b, c, 0, 0)),
        ],
        out_shape=[
            jax.ShapeDtypeStruct((B * S, HD), bf16),
            jax.ShapeDtypeStruct((B, nblk, ATT_BLOCK, HD), bf16),
            jax.ShapeDtypeStruct((B, nblk, ATT_HEADS * V_AUG, ATT_BLOCK), bf16),
        ],
        scratch_shapes=[pltpu.VMEM((nblk, WIDTH), f32)],
        compiler_params=_cparams("parallel", "arbitrary"),
        name="moba_prep",
    )(proj, proj, proj, cos_b, sin_b, qg, kg, gsum, rot, place)


def _mla_prep_kernel(cq_ref, ckv_ref, kr_ref, cos_ref, sin_ref, cqg_ref, ckvg_ref, wuq_ref, wuk_ref,
                     wuv_ref, qg_ref, kg_ref, gsum_ref, rot_ref, qo_ref, ko_ref, vt_ref):
    cqn = (_rms(cq_ref[...].astype(f32), C_Q_RANK) * cqg_ref[...]).astype(bf16)
    ckvn = (_rms(ckv_ref[...].astype(f32), C_KV_RANK) * ckvg_ref[...]).astype(bf16)
    q = jnp.dot(cqn, wuq_ref[...], preferred_element_type=f32)
    k = jnp.dot(ckvn, wuk_ref[...], preferred_element_type=f32)
    v = jnp.dot(ckvn, wuv_ref[...], preferred_element_type=f32)
    kr = kr_ref[...].astype(f32)
    cos = cos_ref[0]
    sin = sin_ref[0]

    def norm_rope(x, gain):
        ss = jnp.dot((x * x).astype(bf16), gsum_ref[...], preferred_element_type=f32)
        xn = x * lax.rsqrt(ss + NORM_EPS) * gain
        xr = jnp.dot(xn.astype(bf16), rot_ref[...], preferred_element_type=f32)
        return xn * cos + xr * sin

    for h in range(C_HEADS):
        sl = slice(h * C_HEAD_PAD, (h + 1) * C_HEAD_PAD)
        qh = norm_rope(q[:, sl], qg_ref[...])
        kh = norm_rope(k[:, sl] + kr, kg_ref[...])
        qo_ref[:, sl] = (qh * (C_QK_DIM ** -0.5 * LOG2E)).astype(bf16)
        ko_ref[0, 0, :, sl] = kh.astype(bf16)
    vt_ref[0, 0] = _value_rows(v)


def _mla_prep(proj, B, S, cos_c, sin_c, cqg, ckvg, wuq, wuk, wuv, qg, kg, gsum, rot):
    nblk = S // ATT_BLOCK
    HD = C_HEADS * C_HEAD_PAD
    table = pl.BlockSpec((1, ATT_BLOCK, C_HEAD_PAD), lambda b, c: (b, c, 0))
    const = lambda shape: pl.BlockSpec(shape, lambda b, c: (0,) * len(shape))
    return pl.pallas_call(
        _mla_prep_kernel,
        grid=(B, nblk),
        in_specs=[
            pl.BlockSpec((ATT_BLOCK, WIDTH), lambda b, c: (b * nblk + c, COL_CQ // WIDTH)),
            pl.BlockSpec((ATT_BLOCK, LANES), lambda b, c: (b * nblk + c, COL_CKV // LANES)),
            pl.BlockSpec((ATT_BLOCK, LANES), lambda b, c: (b * nblk + c, COL_KR // LANES)),
            table, table,
            const((1, WIDTH)), const((1, C_KV_RANK)), const((WIDTH, HD)), const((C_KV_RANK, HD)),
            const((C_KV_RANK, WIDTH)), const((1, C_HEAD_PAD)), const((1, C_HEAD_PAD)),
            const((C_HEAD_PAD, C_HEAD_PAD)), const((C_HEAD_PAD, C_HEAD_PAD)),
        ],
        out_specs=[
            pl.BlockSpec((ATT_BLOCK, HD), lambda b, c: (b * nblk + c, 0)),
            pl.BlockSpec((1, 1, ATT_BLOCK, HD), lambda b, c: (b, c, 0, 0)),
            pl.BlockSpec((1, 1, ATT_HEADS * V_AUG, ATT_BLOCK), lambda b, c: (b, c, 0, 0)),
        ],
        out_shape=[
            jax.ShapeDtypeStruct((B * S, HD), bf16),
            jax.ShapeDtypeStruct((B, nblk, ATT_BLOCK, HD), bf16),
            jax.ShapeDtypeStruct((B, nblk, ATT_HEADS * V_AUG, ATT_BLOCK), bf16),
        ],
        compiler_params=_cparams("parallel", "parallel"),
        name="mla_prep",
    )(proj, proj, proj, cos_c, sin_c, cqg, ckvg, wuq, wuk, wuv, qg, kg, gsum, rot)


def _attn_kernel(q_ref, k_ref, vt_ref, o_ref, *scratch):
    H = ATT_HEADS
    s_scr, p_scr, acc_scr, m_scr, a_scr = (scratch[n * H:(n + 1) * H] for n in range(5))
    out_scr = scratch[5 * H]
    n_past = pl.program_id(1) * (ATT_QBLOCK // ATT_BLOCK)
    nt = (((1,), (1,)), ((), ()))
    chunks = ATT_BLOCK // SM_ROWS

    def scores(j):
        for h in range(H):
            lanes = slice(h * HEAD_SLOT, (h + 1) * HEAD_SLOT)
            s_scr[h][...] = lax.dot_general(k_ref[0, j, :, lanes], q_ref[:, lanes], nt,
                                            preferred_element_type=f32)

    def softmax(diag_offset):
        for h in range(H):
            def rows(r):
                s = s_scr[h][r * SM_ROWS:(r + 1) * SM_ROWS, :]
                if diag_offset is not None:
                    kpos = _iota((SM_ROWS, ATT_QBLOCK), 0) + (r * SM_ROWS + diag_offset)
                    s = jnp.where(kpos <= _iota((SM_ROWS, ATT_QBLOCK), 1), s, NEG_INF)
                return s
            mx = rows(0)
            for r in range(1, chunks):
                mx = jnp.maximum(mx, rows(r))
            m_old = m_scr[h][...]
            m_new = jnp.maximum(m_old, jnp.max(mx, axis=0, keepdims=True))
            a_scr[h][...] = jnp.exp2(m_old - m_new)
            m_scr[h][...] = m_new
            for r in range(chunks):
                p_scr[h][r * SM_ROWS:(r + 1) * SM_ROWS, :] = jnp.exp2(rows(r) - m_new).astype(bf16)

    def values(j):
        for h in range(H):
            pv = jnp.dot(vt_ref[0, j, h * V_AUG:(h + 1) * V_AUG, :], p_scr[h][...],
                         preferred_element_type=f32)
            acc_scr[h][...] = a_scr[h][...] * acc_scr[h][...] + pv

    for h in range(H):
        m_scr[h][...] = jnp.full(m_scr[h].shape, NEG_INF, f32)
        a_scr[h][...] = jnp.zeros(a_scr[h].shape, f32)
        acc_scr[h][...] = jnp.zeros(acc_scr[h].shape, f32)
        p_scr[h][...] = jnp.zeros(p_scr[h].shape, bf16)
    scores(0)

    def past_block(b, carry):
        values(jnp.maximum(b - 1, 0))
        softmax(None)
        scores(b + 1)
        return carry

    lax.fori_loop(0, n_past, past_block, 0)
    values(jnp.maximum(n_past - 1, 0))
    softmax(0)
    scores(n_past + 1)
    values(n_past)
    softmax(ATT_BLOCK)
    values(n_past + 1)

    for h in range(H):
        a = acc_scr[h][...]
        out_scr[h * C_V_DIM:(h + 1) * C_V_DIM, :] = a[:C_V_DIM] / a[C_V_DIM:C_V_DIM + 1]
    o_ref[...] = out_scr[...].T.astype(bf16)


def _attention(q, k, vt, B, S, name):
    nblk = S // ATT_BLOCK
    nq = S // ATT_QBLOCK
    HD = ATT_HEADS * HEAD_SLOT
    return pl.pallas_call(
        _attn_kernel,
        grid=(B, nq),
        in_specs=[
            pl.BlockSpec((ATT_QBLOCK, HD), lambda b, i: (b * nq + i, 0)),
            pl.BlockSpec((1, nblk, ATT_BLOCK, HD), lambda b, i: (b, 0, 0, 0)),
            pl.BlockSpec((1, nblk, ATT_HEADS * V_AUG, ATT_BLOCK), lambda b, i: (b, 0, 0, 0)),
        ],
        out_specs=pl.BlockSpec((ATT_QBLOCK, WIDTH), lambda b, i: (b * nq + i, 0)),
        out_shape=jax.ShapeDtypeStruct((B * S, WIDTH), bf16),
        scratch_shapes=(
            [pltpu.VMEM((ATT_BLOCK, ATT_QBLOCK), f32) for _ in range(ATT_HEADS)]
            + [pltpu.VMEM((ATT_BLOCK, ATT_QBLOCK), bf16) for _ in range(ATT_HEADS)]
            + [pltpu.VMEM((V_AUG, ATT_QBLOCK), f32) for _ in range(ATT_HEADS)]
            + [pltpu.VMEM((1, ATT_QBLOCK), f32) for _ in range(2 * ATT_HEADS)]
            + [pltpu.VMEM((WIDTH, ATT_QBLOCK), f32)]),
        compiler_params=_cparams("parallel", "arbitrary"),
        name=name,
    )(q, k, vt)


ROUTE_E1, ROUTE_E2, ROUTE_W1, ROUTE_W2, ROUTE_R1, ROUTE_R2 = range(6)


def _route_rows(h, wr_ref, br_ref, route_ref, count_ref, carry_scr):
    tm = h.shape[0]

    @pl.when(pl.program_id(0) == 0)
    def _():
        carry_scr[...] = jnp.zeros(carry_scr.shape, f32)

    lane = _iota((tm, LANES), 1)
    h_hi = h.astype(bf16)
    h_lo = (h - h_hi.astype(f32)).astype(bf16)
    w = wr_ref[...]
    w_hi = w.astype(bf16)
    w_lo = (w - w_hi.astype(f32)).astype(bf16)
    logits = (jnp.dot(h_hi, w_hi, preferred_element_type=f32)
              + (jnp.dot(h_hi, w_lo, preferred_element_type=f32)
                 + jnp.dot(h_lo, w_hi, preferred_element_type=f32))) + br_ref[...]
    logits = jnp.where(lane < N_EXPERTS, logits, NEG_INF)
    m1 = jnp.max(logits, axis=-1, keepdims=True)
    e1 = jnp.min(jnp.where(logits == m1, lane, LANES), axis=-1, keepdims=True)
    rest = jnp.where(lane == e1, NEG_INF, logits)
    m2 = jnp.max(rest, axis=-1, keepdims=True)
    e2 = jnp.min(jnp.where(rest == m2, lane, LANES), axis=-1, keepdims=True)
    ex = jnp.exp(m2 - m1)
    w1 = 1.0 / (1.0 + ex)
    w2 = ex / (1.0 + ex)
    chosen = ((lane == e1) | (lane == e2)).astype(f32)
    earlier = (_iota((tm, tm), 1) < _iota((tm, tm), 0)).astype(bf16)
    before = jnp.dot(earlier, chosen.astype(bf16), preferred_element_type=f32) + carry_scr[...]
    r1 = jnp.sum(jnp.where(lane == e1, before, 0.0), axis=-1, keepdims=True)
    r2 = jnp.sum(jnp.where(lane == e2, before, 0.0), axis=-1, keepdims=True)
    out = jnp.zeros((tm, LANES), f32)
    for idx, val in ((ROUTE_E1, e1.astype(f32)), (ROUTE_E2, e2.astype(f32)), (ROUTE_W1, w1),
                     (ROUTE_W2, w2), (ROUTE_R1, r1), (ROUTE_R2, r2)):
        out = jnp.where(lane == idx, val, out)
    route_ref[...] = out
    carry_scr[...] += jnp.sum(chosen, axis=0, keepdims=True)
    count_ref[...] = carry_scr[...]


def _merge_kernel(x_ref, g0_ref, g1_ref, g2_ref, g3_ref, oa_ref, ob_ref, oc_ref, od_ref,
                  wb_ref, wo_ref, ng_ref, *rest, routed):
    y = None
    for n, (g_ref, o_ref) in enumerate(((g0_ref, oa_ref), (g1_ref, ob_ref),
                                        (g2_ref, oc_ref), (g3_ref, od_ref))):
        br = jnp.dot(o_ref[...], wb_ref[n], preferred_element_type=f32)
        t = _sigmoid(g_ref[...].astype(f32)) * br
        y = t if y is None else y + t
    x = x_ref[...] + jnp.dot(y.astype(bf16), wo_ref[...], preferred_element_type=f32)
    h = _rms(x, D_MODEL) * ng_ref[...]
    if routed:
        wr_ref, br_ref, xo_ref, ho_ref, route_ref, count_ref, carry_scr = rest
        ho_ref[...] = h
        _route_rows(h, wr_ref, br_ref, route_ref, count_ref, carry_scr)
    else:
        xo_ref, ho_ref = rest
        ho_ref[...] = h.astype(bf16)
    xo_ref[...] = x


def _merge(x2, proj, oa, ob, oc, od, wb, wo, ng, router=None):
    T = x2.shape[0]
    tm = min(512, T)
    routed = router is not None
    gate = lambda n: pl.BlockSpec((tm, D_MODEL), lambda t: (t, n))
    branch = pl.BlockSpec((tm, WIDTH), lambda t: (t, 0))
    rows = pl.BlockSpec((tm, D_MODEL), lambda t: (t, 0))
    in_specs = [
        rows, gate(0), gate(1), gate(2), gate(3), branch, branch, branch, branch,
        pl.BlockSpec((N_BRANCH, WIDTH, D_MODEL), lambda t: (0, 0, 0)),
        pl.BlockSpec((D_MODEL, D_MODEL), lambda t: (0, 0)),
        pl.BlockSpec((1, D_MODEL), lambda t: (0, 0)),
    ]
    args = [x2, proj, proj, proj, proj, oa, ob, oc, od, wb, wo, ng]
    out_specs = [rows, rows]
    out_shape = [jax.ShapeDtypeStruct((T, D_MODEL), f32),
                 jax.ShapeDtypeStruct((T, D_MODEL), f32 if routed else bf16)]
    scratch = []
    if routed:
        in_specs += [pl.BlockSpec((D_MODEL, LANES), lambda t: (0, 0)),
                     pl.BlockSpec((1, LANES), lambda t: (0, 0))]
        args += list(router)
        out_specs += [pl.BlockSpec((tm, LANES), lambda t: (t, 0)),
                      pl.BlockSpec((1, LANES), lambda t: (0, 0))]
        out_shape += [jax.ShapeDtypeStruct((T, LANES), f32), jax.ShapeDtypeStruct((1, LANES), f32)]
        scratch = [pltpu.VMEM((1, LANES), f32)]
    return pl.pallas_call(
        functools.partial(_merge_kernel, routed=routed),
        grid=(T // tm,),
        in_specs=in_specs,
        out_specs=out_specs,
        out_shape=out_shape,
        scratch_shapes=scratch,
        compiler_params=_cparams("arbitrary" if routed else "parallel"),
        name="merge_route" if routed else "merge",
    )(*args)


def _swiglu_block(h, wg, wu, wd):
    a = jnp.dot(h, wg, preferred_element_type=f32)
    b = jnp.dot(h, wu, preferred_element_type=f32)
    t = (a * _sigmoid(a) * b).astype(bf16)
    return jnp.dot(t, wd, preferred_element_type=f32)


def _ffn_kernel(x_ref, h_ref, wg_ref, wu_ref, wd_ref, o_ref, acc_ref):
    f = pl.program_id(1)
    y = _swiglu_block(h_ref[...], wg_ref[...], wu_ref[...], wd_ref[...])

    @pl.when(f == 0)
    def _():
        acc_ref[...] = x_ref[...] + y

    @pl.when(f > 0)
    def _():
        acc_ref[...] += y

    @pl.when(f == pl.num_programs(1) - 1)
    def _():
        o_ref[...] = acc_ref[...]


def _ffn(x2, h2, wg, wu, wd):
    T = x2.shape[0]
    tm = min(1024, T)
    tf = EXPERT_DIM
    return pl.pallas_call(
        _ffn_kernel,
        grid=(T // tm, FFN_DIM // tf),
        in_specs=[
            pl.BlockSpec((tm, D_MODEL), lambda t, f: (t, 0)),
            pl.BlockSpec((tm, D_MODEL), lambda t, f: (t, 0)),
            pl.BlockSpec((D_MODEL, tf), lambda t, f: (0, f)),
            pl.BlockSpec((D_MODEL, tf), lambda t, f: (0, f)),
            pl.BlockSpec((tf, D_MODEL), lambda t, f: (f, 0)),
        ],
        out_specs=pl.BlockSpec((tm, D_MODEL), lambda t, f: (t, 0)),
        out_shape=jax.ShapeDtypeStruct((T, D_MODEL), f32),
        scratch_shapes=[pltpu.VMEM((tm, D_MODEL), f32)],
        compiler_params=_cparams("parallel", "arbitrary"),
        name="dense_swiglu",
    )(x2, h2, wg, wu, wd)


def _row_copy(src_ref, src_row, dst_ref, dst_row, sem):
    return pltpu.make_async_copy(src_ref.at[pl.ds(src_row, 1)], dst_ref.at[pl.ds(dst_row, 1)], sem)


def _dispatch_kernel(pends_ref, dest_ref, h_ref, xs_ref, zero_scr, sem, zero_sem, *, tm):
    def zero_fill(start):
        start = pl.multiple_of(start, EXPERT_ROWS)
        return pltpu.make_async_copy(zero_scr, xs_ref.at[pl.ds(start, EXPERT_ROWS)], zero_sem)

    @pl.when(pl.program_id(0) == 0)
    def _():
        zero_scr[...] = jnp.zeros(zero_scr.shape, f32)
        used = pends_ref[N_EXPERTS - 1]
        fills = [(jnp.maximum(pends_ref[e] - EXPERT_ROWS, 0), None) for e in range(N_EXPERTS)]
        fills += [(used + e * EXPERT_ROWS, used + e * EXPERT_ROWS < xs_ref.shape[0])
                  for e in range(N_EXPERTS)]
        for action in ("start", "wait"):
            for start, needed in fills:
                if needed is None:
                    getattr(zero_fill(start), action)()
                else:
                    @pl.when(needed)
                    def _():
                        getattr(zero_fill(start), action)()

    def issue(r, carry):
        _row_copy(h_ref, r, xs_ref, dest_ref[0, 0, 2 * r], sem).start()
        _row_copy(h_ref, r, xs_ref, dest_ref[0, 0, 2 * r + 1], sem).start()
        return carry

    lax.fori_loop(0, tm, issue, 0)
    for _ in range(2):
        pltpu.make_async_copy(h_ref, xs_ref.at[pl.ds(0, tm)], sem).wait()


def _dispatch(h2, dest3, pends, rows):
    T = h2.shape[0]
    tm = dest3.shape[-1] // 2
    grid_spec = pltpu.PrefetchScalarGridSpec(
        num_scalar_prefetch=1,
        grid=(T // tm,),
        in_specs=[
            pl.BlockSpec((1, 1, 2 * tm), lambda t, pe: (t, 0, 0), memory_space=pltpu.SMEM),
            pl.BlockSpec((tm, D_MODEL), lambda t, pe: (t, 0)),
        ],
        out_specs=pl.BlockSpec(memory_space=pl.ANY),
        scratch_shapes=[pltpu.VMEM((EXPERT_ROWS, D_MODEL), f32),
                        pltpu.SemaphoreType.DMA(()), pltpu.SemaphoreType.DMA(())],
    )
    return pl.pallas_call(
        functools.partial(_dispatch_kernel, tm=tm),
        grid_spec=grid_spec,
        out_shape=jax.ShapeDtypeStruct((rows, D_MODEL), f32),
        compiler_params=_cparams("arbitrary"),
        name="moe_dispatch",
    )(pends, dest3, h2)


def _expert_kernel(be_ref, nv_ref, xs_ref, wg_ref, wu_ref, wd_ref, ys_ref):
    j = pl.program_id(0)

    @pl.when(j < nv_ref[0])
    def _():
        ys_ref[...] = _swiglu_block(xs_ref[...].astype(bf16), wg_ref[0], wu_ref[0], wd_ref[0])

    @pl.when(j >= nv_ref[0])
    def _():
        ys_ref[...] = jnp.zeros(ys_ref.shape, f32)


def _experts(block_e, n_valid, xs, wg, wu, wd):
    rows = xs.shape[0]
    grid_spec = pltpu.PrefetchScalarGridSpec(
        num_scalar_prefetch=2,
        grid=(rows // EXPERT_ROWS,),
        in_specs=[
            pl.BlockSpec((EXPERT_ROWS, D_MODEL), lambda j, be, nv: (jnp.minimum(j, nv[0] - 1), 0)),
            pl.BlockSpec((1, D_MODEL, EXPERT_DIM), lambda j, be, nv: (be[j], 0, 0)),
            pl.BlockSpec((1, D_MODEL, EXPERT_DIM), lambda j, be, nv: (be[j], 0, 0)),
            pl.BlockSpec((1, EXPERT_DIM, D_MODEL), lambda j, be, nv: (be[j], 0, 0)),
        ],
        out_specs=pl.BlockSpec((EXPERT_ROWS, D_MODEL), lambda j, be, nv: (j, 0)),
    )
    return pl.pallas_call(
        _expert_kernel,
        grid_spec=grid_spec,
        out_shape=jax.ShapeDtypeStruct((rows, D_MODEL), f32),
        compiler_params=_cparams("arbitrary"),
        name="moe_experts",
    )(block_e, n_valid, xs, wg, wu, wd)


def _combine_kernel(dest_ref, x_ref, route_ref, ys_ref, o_ref, buf, sem, *, tm):
    def issue(r, carry):
        _row_copy(ys_ref, dest_ref[0, 0, 2 * r], buf.at[0], r, sem).start()
        _row_copy(ys_ref, dest_ref[0, 0, 2 * r + 1], buf.at[1], r, sem).start()
        return carry

    lax.fori_loop(0, tm, issue, 0)
    for slot in range(2):
        pltpu.make_async_copy(ys_ref.at[pl.ds(0, tm)], buf.at[slot], sem).wait()
    route = route_ref[...]
    w1 = route[:, ROUTE_W1:ROUTE_W1 + 1]
    w2 = route[:, ROUTE_W2:ROUTE_W2 + 1]
    o_ref[...] = x_ref[...] + w1 * buf[0] + w2 * buf[1]


def _combine(x2, route, dest3, ys):
    T = x2.shape[0]
    tm = dest3.shape[-1] // 2
    return pl.pallas_call(
        functools.partial(_combine_kernel, tm=tm),
        grid=(T // tm,),
        in_specs=[
            pl.BlockSpec((1, 1, 2 * tm), lambda t: (t, 0, 0), memory_space=pltpu.SMEM),
            pl.BlockSpec((tm, D_MODEL), lambda t: (t, 0)),
            pl.BlockSpec((tm, LANES), lambda t: (t, 0)),
            pl.BlockSpec(memory_space=pl.ANY),
        ],
        out_specs=pl.BlockSpec((tm, D_MODEL), lambda t: (t, 0)),
        out_shape=jax.ShapeDtypeStruct((T, D_MODEL), f32),
        scratch_shapes=[pltpu.VMEM((2, tm, D_MODEL), f32), pltpu.SemaphoreType.DMA(())],
        compiler_params=_cparams("arbitrary"),
        name="moe_combine",
    )(dest3, x2, route, ys)


def _moe(x2, h2, route, counts, wg, wu, wd):
    T = x2.shape[0]
    tm = min(256, T)
    counts = counts[0, :N_EXPERTS].astype(jnp.int32)
    padded = (counts + EXPERT_ROWS - 1) // EXPERT_ROWS * EXPERT_ROWS
    pends = jnp.cumsum(padded)
    pstarts = pends - padded
    n_blocks = -(-(2 * T) // EXPERT_ROWS) + N_EXPERTS
    block_e = jnp.minimum(
        jnp.searchsorted(pends, jnp.arange(n_blocks, dtype=jnp.int32) * EXPERT_ROWS, side='right'),
        N_EXPERTS - 1).astype(jnp.int32)
    n_valid = (pends[-1:] // EXPERT_ROWS).astype(jnp.int32)
    e12 = route[:, ROUTE_E1:ROUTE_E2 + 1].astype(jnp.int32)
    r12 = route[:, ROUTE_R1:ROUTE_R2 + 1].astype(jnp.int32)
    dest = pstarts[e12] + r12
    dest3 = dest.reshape(T // tm, 1, 2 * tm)
    xs = _dispatch(h2, dest3, pends.astype(jnp.int32), n_blocks * EXPERT_ROWS)
    ys = _experts(block_e, n_valid, xs, wg, wu, wd)
    return _combine(x2, route, dest3, ys)


def _pad_cols(w, n):
    return jnp.pad(w, ((0, 0), (0, n - w.shape[1])))


def _layer_weights(w_in, c_w_uq, c_w_ukv):
    gate_end = N_BRANCH * D_MODEL
    a_end = gate_end + 2 * WIDTH
    b_end = a_end + 3 * WIDTH
    cq = w_in[:, b_end:b_end + C_Q_RANK]
    ckv = w_in[:, b_end + C_Q_RANK:b_end + C_Q_RANK + C_KV_RANK]
    kr = w_in[:, b_end + C_Q_RANK + C_KV_RANK:b_end + C_Q_RANK + C_KV_RANK + C_ROPE_DIM]
    d_start = b_end + C_Q_RANK + C_KV_RANK + C_ROPE_DIM
    kr_block = jnp.pad(kr, ((0, 0), (C_NOPE_DIM, C_HEAD_PAD - C_QK_DIM)))
    w = jnp.concatenate([w_in[:, :b_end], _pad_cols(cq, WIDTH), ckv, kr_block, w_in[:, d_start:]],
                        axis=1).astype(bf16)
    wuq = c_w_uq.reshape(C_Q_RANK, C_HEADS, C_QK_DIM)
    wuq = jnp.pad(wuq, ((0, WIDTH - C_Q_RANK), (0, 0), (0, C_HEAD_PAD - C_QK_DIM)))
    wuq = wuq.reshape(WIDTH, C_HEADS * C_HEAD_PAD).astype(bf16)
    wukv = c_w_ukv.reshape(C_KV_RANK, C_HEADS, C_NOPE_DIM + C_V_DIM)
    wuk = jnp.pad(wukv[:, :, :C_NOPE_DIM], ((0, 0), (0, 0), (0, C_HEAD_PAD - C_NOPE_DIM)))
    wuk = wuk.reshape(C_KV_RANK, C_HEADS * C_HEAD_PAD).astype(bf16)
    wuv = wukv[:, :, C_NOPE_DIM:].reshape(C_KV_RANK, C_HEADS * C_V_DIM).astype(bf16)
    return w, wuq, wuk, wuv


def _rotation_constants():
    ch = jnp.arange(WIDTH)
    gsum_b = ((ch[:, None] // B_HEAD_DIM) == (ch[None, :] // B_HEAD_DIM)).astype(f32) / B_HEAD_DIM
    half_b = B_ROT_DIM // 2
    src, dst = ch[:, None], ch[None, :]
    same = (src // B_HEAD_DIM) == (dst // B_HEAD_DIM)
    ds_, dd = src % B_HEAD_DIM, dst % B_HEAD_DIM
    rot_b = jnp.where(same & (dd < half_b) & (ds_ == dd + half_b), -1.0,
                      jnp.where(same & (dd >= half_b) & (dd < B_ROT_DIM) & (ds_ == dd - half_b), 1.0, 0.0))
    lc = jnp.arange(C_HEAD_PAD)
    real = lc < C_QK_DIM
    gsum_c = (real[:, None] & real[None, :]).astype(f32) / C_QK_DIM
    half_c = C_ROPE_DIM // 2
    s, d = lc[:, None], lc[None, :]
    lo = (d >= C_NOPE_DIM) & (d < C_NOPE_DIM + half_c)
    hi = (d >= C_NOPE_DIM + half_c) & (d < C_QK_DIM)
    rot_c = jnp.where(lo & (s == d + half_c), -1.0, jnp.where(hi & (s == d - half_c), 1.0, 0.0))
    slot = jnp.arange(ATT_HEADS * HEAD_SLOT)
    place_b = ((slot[None, :] // HEAD_SLOT == ch[:, None] // B_HEAD_DIM)
               & (slot[None, :] % HEAD_SLOT == ch[:, None] % B_HEAD_DIM))
    return (gsum_b.astype(bf16), rot_b.astype(bf16), place_b.astype(bf16),
            gsum_c.astype(bf16), rot_c.astype(bf16))


def kernel(x, positions, attn_norm, ffn_norm, w_in, a_ln_g, a_ln_b, a_ws, a_bs, b_qn, b_kn, c_cq_norm, c_ckv_norm, c_w_uq, c_w_ukv, c_qn, c_kn, d_conv_w, d_conv_b, d_ln_g, d_ln_b, w_branch, w_out, ffn_w_gate, ffn_w_up, ffn_w_down, moe_router, moe_router_b, moe_w_gate, moe_w_up, moe_w_down):
    B, S, _ = x.shape
    T = B * S
    assert S % ATT_BLOCK == 0 and T % EXPERT_ROWS == 0
    x2 = x.reshape(T, D_MODEL)
    cos_b, sin_b, cos_c, sin_c = _rope_tables(positions)
    gsum_b, rot_b, place_b, gsum_c, rot_c = _rotation_constants()
    row = lambda v: v.reshape(1, -1)

    for l in range(DEPTH):
        w, wuq, wuk, wuv = _layer_weights(w_in[l], c_w_uq[l], c_w_ukv[l])
        proj = _inproj(x2, row(attn_norm[l]), w)

        o_a = _sgu(proj, row(a_ln_g[l]), row(a_ln_b[l]), a_ws[l], a_bs[l].T)

        qb, kb, vtb = _moba_prep(proj, B, S, cos_b, sin_b,
                                 row(jnp.tile(b_qn[l], B_HEADS)), row(jnp.tile(b_kn[l], B_HEADS)),
                                 gsum_b, rot_b, place_b)
        o_b = _attention(qb, kb, vtb, B, S, "moba_attention")

        pad_gain = lambda g, n: row(jnp.pad(g, (0, n - g.shape[0])))
        qc, kc, vtc = _mla_prep(proj, B, S, cos_c, sin_c, pad_gain(c_cq_norm[l], WIDTH),
                                row(c_ckv_norm[l]), wuq, wuk, wuv,
                                pad_gain(c_qn[l], C_HEAD_PAD), pad_gain(c_kn[l], C_HEAD_PAD),
                                gsum_c, rot_c)
        o_c = _attention(qc, kc, vtc, B, S, "mla_attention")

        o_d = _conformer(proj, B, S, d_conv_w[l], row(d_conv_b[l]), row(d_ln_g[l]), row(d_ln_b[l]))

        merge_args = (x2, proj, o_a, o_b, o_c, o_d, w_branch[l].astype(bf16),
                      w_out[l].astype(bf16), row(ffn_norm[l]))
        i = l // 2
        if l % 2 == 0:
            x2, h2 = _merge(*merge_args)
            x2 = _ffn(x2, h2, ffn_w_gate[i].astype(bf16), ffn_w_up[i].astype(bf16),
                      ffn_w_down[i].astype(bf16))
        else:
            router = (_pad_cols(moe_router[i], LANES), _pad_cols(row(moe_router_b[i]), LANES))
            x2, h2, route, counts = _merge(*merge_args, router=router)
            x2 = _moe(x2, h2, route, counts, moe_w_gate[i].astype(bf16),
                      moe_w_up[i].astype(bf16), moe_w_down[i].astype(bf16))
    return x2.reshape(B, S, D_MODEL)
```

```python
import functools

import jax
import jax.numpy as jnp
from jax import lax
from jax.experimental import pallas as pl
from jax.experimental.pallas import tpu as pltpu

D_MODEL = 1024
DEPTH = 4
N_BRANCH = 4
ROPE_THETA = 500000.0
NORM_EPS = 1e-6
NEG_INF = -1e30

WIDTH = 256
A_GROUPS = 4
A_GROUP_DIM = 64
A_CHUNK = 128
B_HEADS = 4
B_HEAD_DIM = 64
B_ROT_DIM = 16
MOBA_BLOCK = 256
MOBA_TOPK = 3
C_HEADS = 4
C_Q_RANK = 192
C_KV_RANK = 128
C_NOPE_DIM = 64
C_ROPE_DIM = 32
C_V_DIM = 64
C_QK_DIM = C_NOPE_DIM + C_ROPE_DIM
C_HEAD_PAD = 128
D_CONV = 31
FFN_DIM = 2816
N_EXPERTS = 8
EXPERT_DIM = 1408
EXPERT_ROWS = 512

ATT_BLOCK = 256
ATT_QBLOCK = 512
ATT_HEADS = 4
HEAD_SLOT = 128
V_AUG = C_V_DIM + 16
LOG2E = 1.4426950408889634
SM_ROWS = 32
LANES = 128
SUBLANES = 8
CONV_HALO = 32
CONV_ROWS = 64
LN_ROWS = 128

COL_GATE = 0
COL_AU = 4096
COL_AV = 4352
COL_BQ = 4608
COL_BK = 4864
COL_BV = 5120
COL_CQ = 5376
COL_CKV = 5632
COL_KR = 5760
COL_DA = 5888
COL_DG = 6144
PROJ_COLS = 6400

VMEM_LIMIT = 56 * 1024 * 1024

bf16 = jnp.bfloat16
f32 = jnp.float32


def _cparams(*sem):
    return pltpu.CompilerParams(dimension_semantics=sem, vmem_limit_bytes=VMEM_LIMIT)


def _rms(x, n):
    return x * lax.rsqrt(jnp.sum(x * x, axis=-1, keepdims=True) * (1.0 / n) + NORM_EPS)


def _layernorm(x, g, b):
    mu = jnp.mean(x, axis=-1, keepdims=True)
    xc = x - mu
    return xc * lax.rsqrt(jnp.mean(xc * xc, axis=-1, keepdims=True) + NORM_EPS) * g + b


def _sigmoid(x):
    return 0.5 * jnp.tanh(0.5 * x) + 0.5


def _iota(shape, dim):
    return lax.broadcasted_iota(jnp.int32, shape, dim)


def _rope_table_kernel(pos_ref, inv_ref, eb_ref, ec_ref, cb_ref, sb_ref, cc_ref, sc_ref):
    ang = pos_ref[0].astype(f32) * inv_ref[...]
    c = jnp.cos(ang)
    s = jnp.sin(ang)
    hi = lax.Precision.HIGHEST
    cb_ref[0] = jnp.dot(c, eb_ref[...], precision=hi, preferred_element_type=f32)
    sb_ref[0] = jnp.dot(s, eb_ref[...], precision=hi, preferred_element_type=f32)
    cc_ref[0] = jnp.dot(c, ec_ref[...], precision=hi, preferred_element_type=f32)
    sc_ref[0] = jnp.dot(s, ec_ref[...], precision=hi, preferred_element_type=f32)


def _rope_tables(positions):
    B, S = positions.shape
    R = min(512, S)
    nb = B_ROT_DIM // 2
    nc = C_ROPE_DIM // 2
    inv_b = f32(ROPE_THETA) ** (-jnp.arange(0, B_ROT_DIM, 2, dtype=f32) / B_ROT_DIM)
    inv_c = f32(ROPE_THETA) ** (-jnp.arange(0, C_ROPE_DIM, 2, dtype=f32) / C_ROPE_DIM)
    inv = jnp.zeros((1, LANES), f32).at[0, :nb].set(inv_b).at[0, nb:nb + nc].set(inv_c)
    zero_lane = LANES - 1
    ch = jnp.arange(WIDTH)
    d = ch % B_HEAD_DIM
    src_b = jnp.where(d < B_ROT_DIM, d % nb, zero_lane)
    eb = (jnp.arange(LANES)[:, None] == src_b[None, :]).astype(f32)
    dc = jnp.arange(C_HEAD_PAD)
    in_rope = (dc >= C_NOPE_DIM) & (dc < C_QK_DIM)
    src_c = jnp.where(in_rope, nb + (dc - C_NOPE_DIM) % nc, zero_lane)
    ec = (jnp.arange(LANES)[:, None] == src_c[None, :]).astype(f32)
    pos3 = positions.reshape(B, S, 1)
    out = pl.pallas_call(
        _rope_table_kernel,
        grid=(B, S // R),
        in_specs=[
            pl.BlockSpec((1, R, 1), lambda b, r: (b, r, 0)),
            pl.BlockSpec((1, LANES), lambda b, r: (0, 0)),
            pl.BlockSpec((LANES, WIDTH), lambda b, r: (0, 0)),
            pl.BlockSpec((LANES, C_HEAD_PAD), lambda b, r: (0, 0)),
        ],
        out_specs=[
            pl.BlockSpec((1, R, WIDTH), lambda b, r: (b, r, 0)),
            pl.BlockSpec((1, R, WIDTH), lambda b, r: (b, r, 0)),
            pl.BlockSpec((1, R, C_HEAD_PAD), lambda b, r: (b, r, 0)),
            pl.BlockSpec((1, R, C_HEAD_PAD), lambda b, r: (b, r, 0)),
        ],
        out_shape=[
            jax.ShapeDtypeStruct((B, S, WIDTH), f32),
            jax.ShapeDtypeStruct((B, S, WIDTH), f32),
            jax.ShapeDtypeStruct((B, S, C_HEAD_PAD), f32),
            jax.ShapeDtypeStruct((B, S, C_HEAD_PAD), f32),
        ],
        compiler_params=_cparams("parallel", "parallel"),
        name="rope_tables",
    )(pos3, inv, eb, ec)
    return out


def _inproj_kernel(x_ref, g_ref, w_ref, o_ref, *, col_chunk):
    h = (_rms(x_ref[...], D_MODEL) * g_ref[...]).astype(bf16)
    for c in range(PROJ_COLS // col_chunk):
        sl = slice(c * col_chunk, (c + 1) * col_chunk)
        o_ref[:, sl] = jnp.dot(h, w_ref[:, sl], preferred_element_type=f32).astype(bf16)


def _inproj(x2, g, w):
    T = x2.shape[0]
    tm = min(512, T)
    return pl.pallas_call(
        functools.partial(_inproj_kernel, col_chunk=1280),
        grid=(T // tm,),
        in_specs=[
            pl.BlockSpec((tm, D_MODEL), lambda t: (t, 0)),
            pl.BlockSpec((1, D_MODEL), lambda t: (0, 0)),
            pl.BlockSpec((D_MODEL, PROJ_COLS), lambda t: (0, 0)),
        ],
        out_specs=pl.BlockSpec((tm, PROJ_COLS), lambda t: (t, 0)),
        out_shape=jax.ShapeDtypeStruct((T, PROJ_COLS), bf16),
        compiler_params=_cparams("parallel"),
        name="inproj",
    )(x2, g, w)


def _sgu_kernel(u_ref, v_ref, g_ref, b_ref, ws_ref, bst_ref, o_ref, *, chunks):
    row = _iota((A_CHUNK, A_CHUNK), 0)
    col = _iota((A_CHUNK, A_CHUNK), 1)
    causal = col <= row
    group = _iota((A_CHUNK, WIDTH), 1) // A_GROUP_DIM
    wm = [jnp.where(causal, ws_ref[g], 0.0).astype(bf16) for g in range(A_GROUPS)]
    bias = jnp.zeros((A_CHUNK, WIDTH), f32)
    for g in range(A_GROUPS):
        bias = jnp.where(group == g, bst_ref[:, g:g + 1], bias)
    for c in range(chunks):
        rows = slice(c * A_CHUNK, (c + 1) * A_CHUNK)
        vn = _layernorm(v_ref[rows, :].astype(f32), g_ref[...], b_ref[...]).astype(bf16)
        z = bias
        for g in range(A_GROUPS):
            zg = jnp.dot(wm[g], vn, preferred_element_type=f32)
            z = z + jnp.where(group == g, zg, 0.0)
        o_ref[rows, :] = (u_ref[rows, :].astype(f32) * z).astype(bf16)


def _sgu(proj, ln_g, ln_b, ws, bs_t):
    T = proj.shape[0]
    ta = min(512, T)
    cu, cv = COL_AU // WIDTH, COL_AV // WIDTH
    return pl.pallas_call(
        functools.partial(_sgu_kernel, chunks=ta // A_CHUNK),
        grid=(T // ta,),
        in_specs=[
            pl.BlockSpec((ta, WIDTH), lambda t: (t, cu)),
            pl.BlockSpec((ta, WIDTH), lambda t: (t, cv)),
            pl.BlockSpec((1, WIDTH), lambda t: (0, 0)),
            pl.BlockSpec((1, WIDTH), lambda t: (0, 0)),
            pl.BlockSpec((A_GROUPS, A_CHUNK, A_CHUNK), lambda t: (0, 0, 0)),
            pl.BlockSpec((A_CHUNK, A_GROUPS), lambda t: (0, 0)),
        ],
        out_specs=pl.BlockSpec((ta, WIDTH), lambda t: (t, 0)),
        out_shape=jax.ShapeDtypeStruct((T, WIDTH), bf16),
        compiler_params=_cparams("parallel"),
        name="sgu",
    )(proj, proj, ln_g, ln_b, ws, bs_t)


def _conv_kernel(a_ref, g_ref, w_ref, cb_ref, lg_ref, lb_ref, o_ref, z_scr, *, seq):
    first = CONV_HALO - (D_CONV - 1)
    z_scr[0, 0:CONV_HALO, :] = jnp.zeros((CONV_HALO, WIDTH), f32)
    z_scr[0, CONV_HALO + seq:, :] = jnp.zeros((CONV_ROWS, WIDTH), f32)

    def glu(c, carry):
        base = pl.multiple_of(c * LN_ROWS, LN_ROWS)
        rows = pl.ds(base, LN_ROWS)
        z_scr[0, pl.ds(base + CONV_HALO, LN_ROWS), :] = (
            a_ref[rows, :].astype(f32) * _sigmoid(g_ref[rows, :].astype(f32)))
        return carry

    lax.fori_loop(0, seq // LN_ROWS, glu, 0)

    def shift(c, carry):
        base = pl.multiple_of(c * CONV_ROWS, CONV_ROWS)
        win = z_scr[0, pl.ds(base, CONV_ROWS + SUBLANES), :]
        for r in range(1, SUBLANES):
            rolled = pltpu.roll(win, shift=CONV_ROWS + SUBLANES - r, axis=0)
            z_scr[r, pl.ds(base, CONV_ROWS), :] = rolled[:CONV_ROWS]
        return carry

    lax.fori_loop(0, seq // CONV_ROWS + 1, shift, 0)

    def taps(c, carry):
        base = pl.multiple_of(c * CONV_ROWS, CONV_ROWS)
        acc = jnp.zeros((CONV_ROWS, WIDTH), f32) + cb_ref[...]
        for k in range(D_CONV):
            off = first + k
            lo = pl.multiple_of(base + off // SUBLANES * SUBLANES, SUBLANES)
            acc = acc + w_ref[k:k + 1, :] * z_scr[off % SUBLANES, pl.ds(lo, CONV_ROWS), :]
        z_scr[1, pl.ds(base, CONV_ROWS), :] = acc
        return carry

    lax.fori_loop(0, seq // CONV_ROWS, taps, 0)

    def norm(c, carry):
        base = pl.multiple_of(c * LN_ROWS, LN_ROWS)
        y = _layernorm(z_scr[1, pl.ds(base, LN_ROWS), :], lg_ref[...], lb_ref[...])
        o_ref[pl.ds(base, LN_ROWS), :] = (y * _sigmoid(y)).astype(bf16)
        return carry

    lax.fori_loop(0, seq // LN_ROWS, norm, 0, unroll=2)


def _conformer(proj, B, S, conv_w, conv_b, ln_g, ln_b):
    ca, cg = COL_DA // WIDTH, COL_DG // WIDTH
    return pl.pallas_call(
        functools.partial(_conv_kernel, seq=S),
        grid=(B,),
        in_specs=[
            pl.BlockSpec((S, WIDTH), lambda b: (b, ca)),
            pl.BlockSpec((S, WIDTH), lambda b: (b, cg)),
            pl.BlockSpec((D_CONV, WIDTH), lambda b: (0, 0)),
            pl.BlockSpec((1, WIDTH), lambda b: (0, 0)),
            pl.BlockSpec((1, WIDTH), lambda b: (0, 0)),
            pl.BlockSpec((1, WIDTH), lambda b: (0, 0)),
        ],
        out_specs=pl.BlockSpec((S, WIDTH), lambda b: (b, 0)),
        out_shape=jax.ShapeDtypeStruct((B * S, WIDTH), bf16),
        scratch_shapes=[pltpu.VMEM((SUBLANES, CONV_HALO + S + CONV_ROWS, WIDTH), f32)],
        compiler_params=_cparams("parallel"),
        name="conformer_conv",
    )(proj, proj, conv_w, conv_b, ln_g, ln_b)


def _value_rows(v_tok):
    vt = v_tok.T
    ones = jnp.ones((V_AUG - C_V_DIM, vt.shape[1]), f32)
    rows = []
    for h in range(ATT_HEADS):
        rows += [vt[h * C_V_DIM:(h + 1) * C_V_DIM], ones]
    return jnp.concatenate(rows, axis=0).astype(bf16)


def _moba_prep_kernel(q_ref, k_ref, v_ref, cos_ref, sin_ref, qg_ref, kg_ref, gsum_ref, rot_ref,
                      place_ref, qo_ref, ko_ref, vt_ref, kmean_scr, *, nblk):
    c = pl.program_id(1)

    @pl.when(c == 0)
    def _():
        kmean_scr[...] = jnp.zeros(kmean_scr.shape, f32)

    cos = cos_ref[0]
    sin = sin_ref[0]

    def norm_rope(x, gain):
        ss = jnp.dot((x * x).astype(bf16), gsum_ref[...], preferred_element_type=f32)
        xn = x * lax.rsqrt(ss + NORM_EPS) * gain
        xr = jnp.dot(xn.astype(bf16), rot_ref[...], preferred_element_type=f32)
        return xn * cos + xr * sin

    q = norm_rope(q_ref[...].astype(f32), qg_ref[...])
    k = norm_rope(k_ref[...].astype(f32), kg_ref[...])
    slot_lane = _iota((ATT_BLOCK, ATT_HEADS * HEAD_SLOT), 1) % HEAD_SLOT
    k_slots = jnp.dot(k.astype(bf16), place_ref[...], preferred_element_type=f32)
    ko_ref[0, 0] = jnp.where(slot_lane == B_HEAD_DIM + c, 1.0, k_slots).astype(bf16)
    vt_ref[0, 0] = _value_rows(v_ref[...].astype(f32))

    head = _iota((nblk, WIDTH), 1) // B_HEAD_DIM
    blk = _iota((nblk, ATT_BLOCK), 0)
    past = blk < c
    kmean = kmean_scr[...]
    zeros_lo = jnp.zeros((B_HEAD_DIM, ATT_BLOCK), f32)
    zeros_hi = jnp.zeros((HEAD_SLOT - B_HEAD_DIM - nblk, ATT_BLOCK), f32)
    bias_rows = []
    for h in range(B_HEADS):
        km = jnp.where(head == h, kmean, 0.0)
        sc = lax.dot_general(km, q, (((1,), (1,)), ((), ())), precision=lax.Precision.HIGHEST,
                             preferred_element_type=f32)
        sc = jnp.where(past, sc, NEG_INF)
        rank = jnp.zeros((nblk, ATT_BLOCK), jnp.int32)
        for j in range(nblk):
            other = sc[j:j + 1, :]
            ahead = (other > sc) | ((other == sc) & (j < blk))
            rank = rank + ahead.astype(jnp.int32)
        visible = (past & (rank < MOBA_TOPK)) | (blk == c)
        bias_rows += [zeros_lo, jnp.where(visible, 0.0, NEG_INF), zeros_hi]
    bias = jnp.concatenate(bias_rows, axis=0).T
    q_slots = jnp.dot((q * (B_HEAD_DIM ** -0.5 * LOG2E)).astype(bf16), place_ref[...],
                      preferred_element_type=f32)
    qo_ref[...] = (q_slots + bias).astype(bf16)

    kmean_scr[pl.ds(c, 1), :] = jnp.mean(k, axis=0, keepdims=True)


def _moba_prep(proj, B, S, cos_b, sin_b, qg, kg, gsum, rot, place):
    nblk = S // ATT_BLOCK
    assert nblk <= HEAD_SLOT - B_HEAD_DIM
    HD = ATT_HEADS * HEAD_SLOT
    cq, ck, cv = COL_BQ // WIDTH, COL_BK // WIDTH, COL_BV // WIDTH
    tile = lambda col: pl.BlockSpec((ATT_BLOCK, WIDTH), lambda b, c: (b * nblk + c, col))
    table = pl.BlockSpec((1, ATT_BLOCK, WIDTH), lambda b, c: (b, c, 0))
    const = lambda shape: pl.BlockSpec(shape, lambda b, c: (0,) * len(shape))
    return pl.pallas_call(
        functools.partial(_moba_prep_kernel, nblk=nblk),
        grid=(B, nblk),
        in_specs=[tile(cq), tile(ck), tile(cv), table, table,
                  const((1, WIDTH)), const((1, WIDTH)), const((WIDTH, WIDTH)), const((WIDTH, WIDTH)),
                  const((WIDTH, HD))],
        out_specs=[
            pl.BlockSpec((ATT_BLOCK, HD), lambda b, c: (b * nblk + c, 0)),
            pl.BlockSpec((1, 1, ATT_BLOCK, HD), lambda b, c: (b, c, 0, 0)),
            pl.BlockSpec((1, 1, ATT_HEADS * V_AUG, ATT_BLOCK), lambda b, c: (b, c, 0, 0)),
        ],
        out_shape=[
            jax.ShapeDtypeStruct((B * S, HD), bf16),
            jax.ShapeDtypeStruct((B, nblk, ATT_BLOCK, HD), bf16),
            jax.ShapeDtypeStruct((B, nblk, ATT_HEADS * V_AUG, ATT_BLOCK), bf16),
        ],
        scratch_shapes=[pltpu.VMEM((nblk, WIDTH), f32)],
        compiler_params=_cparams("parallel", "arbitrary"),
        name="moba_prep",
    )(proj, proj, proj, cos_b, sin_b, qg, kg, gsum, rot, place)


def _mla_prep_kernel(cq_ref, ckv_ref, kr_ref, cos_ref, sin_ref, cqg_ref, ckvg_ref, wuq_ref, wuk_ref,
                     wuv_ref, qg_ref, kg_ref, gsum_ref, rot_ref, qo_ref, ko_ref, vt_ref):
    cqn = (_rms(cq_ref[...].astype(f32), C_Q_RANK) * cqg_ref[...]).astype(bf16)
    ckvn = (_rms(ckv_ref[...].astype(f32), C_KV_RANK) * ckvg_ref[...]).astype(bf16)
    q = jnp.dot(cqn, wuq_ref[...], preferred_element_type=f32)
    k = jnp.dot(ckvn, wuk_ref[...], preferred_element_type=f32)
    v = jnp.dot(ckvn, wuv_ref[...], preferred_element_type=f32)
    kr = kr_ref[...].astype(f32)
    cos = cos_ref[0]
    sin = sin_ref[0]

    def norm_rope(x, gain):
        ss = jnp.dot((x * x).astype(bf16), gsum_ref[...], preferred_element_type=f32)
        xn = x * lax.rsqrt(ss + NORM_EPS) * gain
        xr = jnp.dot(xn.astype(bf16), rot_ref[...], preferred_element_type=f32)
        return xn * cos + xr * sin

    for h in range(C_HEADS):
        sl = slice(h * C_HEAD_PAD, (h + 1) * C_HEAD_PAD)
        qh = norm_rope(q[:, sl], qg_ref[...])
        kh = norm_rope(k[:, sl] + kr, kg_ref[...])
        qo_ref[:, sl] = (qh * (C_QK_DIM ** -0.5 * LOG2E)).astype(bf16)
        ko_ref[0, 0, :, sl] = kh.astype(bf16)
    vt_ref[0, 0] = _value_rows(v)


def _mla_prep(proj, B, S, cos_c, sin_c, cqg, ckvg, wuq, wuk, wuv, qg, kg, gsum, rot):
    nblk = S // ATT_BLOCK
    HD = C_HEADS * C_HEAD_PAD
    table = pl.BlockSpec((1, ATT_BLOCK, C_HEAD_PAD), lambda b, c: (b, c, 0))
    const = lambda shape: pl.BlockSpec(shape, lambda b, c: (0,) * len(shape))
    return pl.pallas_call(
        _mla_prep_kernel,
        grid=(B, nblk),
        in_specs=[
            pl.BlockSpec((ATT_BLOCK, WIDTH), lambda b, c: (b * nblk + c, COL_CQ // WIDTH)),
            pl.BlockSpec((ATT_BLOCK, LANES), lambda b, c: (b * nblk + c, COL_CKV // LANES)),
            pl.BlockSpec((ATT_BLOCK, LANES), lambda b, c: (b * nblk + c, COL_KR // LANES)),
            table, table,
            const((1, WIDTH)), const((1, C_KV_RANK)), const((WIDTH, HD)), const((C_KV_RANK, HD)),
            const((C_KV_RANK, WIDTH)), const((1, C_HEAD_PAD)), const((1, C_HEAD_PAD)),
            const((C_HEAD_PAD, C_HEAD_PAD)), const((C_HEAD_PAD, C_HEAD_PAD)),
        ],
        out_specs=[
            pl.BlockSpec((ATT_BLOCK, HD), lambda b, c: (b * nblk + c, 0)),
            pl.BlockSpec((1, 1, ATT_BLOCK, HD), lambda b, c: (b, c, 0, 0)),
            pl.BlockSpec((1, 1, ATT_HEADS * V_AUG, ATT_BLOCK), lambda b, c: (b, c, 0, 0)),
        ],
        out_shape=[
            jax.ShapeDtypeStruct((B * S, HD), bf16),
            jax.ShapeDtypeStruct((B, nblk, ATT_BLOCK, HD), bf16),
            jax.ShapeDtypeStruct((B, nblk, ATT_HEADS * V_AUG, ATT_BLOCK), bf16),
        ],
        compiler_params=_cparams("parallel", "parallel"),
        name="mla_prep",
    )(proj, proj, proj, cos_c, sin_c, cqg, ckvg, wuq, wuk, wuv, qg, kg, gsum, rot)


def _attn_kernel(q_ref, k_ref, vt_ref, o_ref, *scratch):
    H = ATT_HEADS
    s_scr, p_scr, acc_scr, m_scr, a_scr = (scratch[n * H:(n + 1) * H] for n in range(5))
    out_scr = scratch[5 * H]
    n_past = pl.program_id(1) * (ATT_QBLOCK // ATT_BLOCK)
    nt = (((1,), (1,)), ((), ()))
    chunks = ATT_BLOCK // SM_ROWS

    def scores(j):
        for h in range(H):
            lanes = slice(h * HEAD_SLOT, (h + 1) * HEAD_SLOT)
            s_scr[h][...] = lax.dot_general(k_ref[0, j, :, lanes], q_ref[:, lanes], nt,
                                            preferred_element_type=f32)

    def softmax(diag_offset):
        for h in range(H):
            def rows(r):
                s = s_scr[h][r * SM_ROWS:(r + 1) * SM_ROWS, :]
                if diag_offset is not None:
                    kpos = _iota((SM_ROWS, ATT_QBLOCK), 0) + (r * SM_ROWS + diag_offset)
                    s = jnp.where(kpos <= _iota((SM_ROWS, ATT_QBLOCK), 1), s, NEG_INF)
                return s
            mx = rows(0)
            for r in range(1, chunks):
                mx = jnp.maximum(mx, rows(r))
            m_old = m_scr[h][...]
            m_new = jnp.maximum(m_old, jnp.max(mx, axis=0, keepdims=True))
            a_scr[h][...] = jnp.exp2(m_old - m_new)
            m_scr[h][...] = m_new
            for r in range(chunks):
                p_scr[h][r * SM_ROWS:(r + 1) * SM_ROWS, :] = jnp.exp2(rows(r) - m_new).astype(bf16)

    def values(j):
        for h in range(H):
            pv = jnp.dot(vt_ref[0, j, h * V_AUG:(h + 1) * V_AUG, :], p_scr[h][...],
                         preferred_element_type=f32)
            acc_scr[h][...] = a_scr[h][...] * acc_scr[h][...] + pv

    for h in range(H):
        m_scr[h][...] = jnp.full(m_scr[h].shape, NEG_INF, f32)
        a_scr[h][...] = jnp.zeros(a_scr[h].shape, f32)
        acc_scr[h][...] = jnp.zeros(acc_scr[h].shape, f32)
        p_scr[h][...] = jnp.zeros(p_scr[h].shape, bf16)
    scores(0)

    def past_block(b, carry):
        values(jnp.maximum(b - 1, 0))
        softmax(None)
        scores(b + 1)
        return carry

    lax.fori_loop(0, n_past, past_block, 0)
    values(jnp.maximum(n_past - 1, 0))
    softmax(0)
    scores(n_past + 1)
    values(n_past)
    softmax(ATT_BLOCK)
    values(n_past + 1)

    for h in range(H):
        a = acc_scr[h][...]
        out_scr[h * C_V_DIM:(h + 1) * C_V_DIM, :] = a[:C_V_DIM] / a[C_V_DIM:C_V_DIM + 1]
    o_ref[...] = out_scr[...].T.astype(bf16)


def _attention(q, k, vt, B, S, name):
    nblk = S // ATT_BLOCK
    nq = S // ATT_QBLOCK
    HD = ATT_HEADS * HEAD_SLOT
    return pl.pallas_call(
        _attn_kernel,
        grid=(B, nq),
        in_specs=[
            pl.BlockSpec((ATT_QBLOCK, HD), lambda b, i: (b * nq + i, 0)),
            pl.BlockSpec((1, nblk, ATT_BLOCK, HD), lambda b, i: (b, 0, 0, 0)),
            pl.BlockSpec((1, nblk, ATT_HEADS * V_AUG, ATT_BLOCK), lambda b, i: (b, 0, 0, 0)),
        ],
        out_specs=pl.BlockSpec((ATT_QBLOCK, WIDTH), lambda b, i: (b * nq + i, 0)),
        out_shape=jax.ShapeDtypeStruct((B * S, WIDTH), bf16),
        scratch_shapes=(
            [pltpu.VMEM((ATT_BLOCK, ATT_QBLOCK), f32) for _ in range(ATT_HEADS)]
            + [pltpu.VMEM((ATT_BLOCK, ATT_QBLOCK), bf16) for _ in range(ATT_HEADS)]
            + [pltpu.VMEM((V_AUG, ATT_QBLOCK), f32) for _ in range(ATT_HEADS)]
            + [pltpu.VMEM((1, ATT_QBLOCK), f32) for _ in range(2 * ATT_HEADS)]
            + [pltpu.VMEM((WIDTH, ATT_QBLOCK), f32)]),
        compiler_params=_cparams("parallel", "arbitrary"),
        name=name,
    )(q, k, vt)


ROUTE_E1, ROUTE_E2, ROUTE_W1, ROUTE_W2, ROUTE_R1, ROUTE_R2 = range(6)


def _route_rows(h, wr_ref, br_ref, route_ref, carry_scr):
    tm = h.shape[0]

    @pl.when(pl.program_id(0) == 0)
    def _():
        carry_scr[...] = jnp.zeros(carry_scr.shape, f32)

    lane = _iota((tm, LANES), 1)
    h_hi = h.astype(bf16)
    h_lo = (h - h_hi.astype(f32)).astype(bf16)
    w = wr_ref[...]
    w_hi = w.astype(bf16)
    w_lo = (w - w_hi.astype(f32)).astype(bf16)
    logits = (jnp.dot(h_hi, w_hi, preferred_element_type=f32)
              + (jnp.dot(h_hi, w_lo, preferred_element_type=f32)
                 + jnp.dot(h_lo, w_hi, preferred_element_type=f32))) + br_ref[...]
    logits = jnp.where(lane < N_EXPERTS, logits, NEG_INF)
    m1 = jnp.max(logits, axis=-1, keepdims=True)
    e1 = jnp.min(jnp.where(logits == m1, lane, LANES), axis=-1, keepdims=True)
    rest = jnp.where(lane == e1, NEG_INF, logits)
    m2 = jnp.max(rest, axis=-1, keepdims=True)
    e2 = jnp.min(jnp.where(rest == m2, lane, LANES), axis=-1, keepdims=True)
    ex = jnp.exp(m2 - m1)
    w1 = 1.0 / (1.0 + ex)
    w2 = ex / (1.0 + ex)
    chosen = ((lane == e1) | (lane == e2)).astype(f32)
    earlier = (_iota((tm, tm), 1) < _iota((tm, tm), 0)).astype(bf16)
    before = jnp.dot(earlier, chosen.astype(bf16), preferred_element_type=f32) + carry_scr[...]
    r1 = jnp.sum(jnp.where(lane == e1, before, 0.0), axis=-1, keepdims=True)
    r2 = jnp.sum(jnp.where(lane == e2, before, 0.0), axis=-1, keepdims=True)
    out = jnp.zeros((tm, LANES), f32)
    for idx, val in ((ROUTE_E1, e1.astype(f32)), (ROUTE_E2, e2.astype(f32)), (ROUTE_W1, w1),
                     (ROUTE_W2, w2), (ROUTE_R1, r1), (ROUTE_R2, r2)):
        out = jnp.where(lane == idx, val, out)
    route_ref[...] = out
    carry_scr[...] += jnp.sum(chosen, axis=0, keepdims=True)


def _merge_kernel(x_ref, g0_ref, g1_ref, g2_ref, g3_ref, oa_ref, ob_ref, oc_ref, od_ref,
                  wb_ref, wo_ref, ng_ref, *rest, routed):
    y = None
    for n, (g_ref, o_ref) in enumerate(((g0_ref, oa_ref), (g1_ref, ob_ref),
                                        (g2_ref, oc_ref), (g3_ref, od_ref))):
        br = jnp.dot(o_ref[...], wb_ref[n], preferred_element_type=f32)
        t = _sigmoid(g_ref[...].astype(f32)) * br
        y = t if y is None else y + t
    x = x_ref[...] + jnp.dot(y.astype(bf16), wo_ref[...], preferred_element_type=f32)
    h = _rms(x, D_MODEL) * ng_ref[...]
    if routed:
        wr_ref, br_ref, xo_ref, ho_ref, route_ref, carry_scr = rest
        ho_ref[...] = h
        _route_rows(h, wr_ref, br_ref, route_ref, carry_scr)
    else:
        xo_ref, ho_ref = rest
        ho_ref[...] = h.astype(bf16)
    xo_ref[...] = x


def _merge(x2, proj, oa, ob, oc, od, wb, wo, ng, router=None):
    T = x2.shape[0]
    tm = min(512, T)
    routed = router is not None
    gate = lambda n: pl.BlockSpec((tm, D_MODEL), lambda t: (t, n))
    branch = pl.BlockSpec((tm, WIDTH), lambda t: (t, 0))
    rows = pl.BlockSpec((tm, D_MODEL), lambda t: (t, 0))
    in_specs = [
        rows, gate(0), gate(1), gate(2), gate(3), branch, branch, branch, branch,
        pl.BlockSpec((N_BRANCH, WIDTH, D_MODEL), lambda t: (0, 0, 0)),
        pl.BlockSpec((D_MODEL, D_MODEL), lambda t: (0, 0)),
        pl.BlockSpec((1, D_MODEL), lambda t: (0, 0)),
    ]
    args = [x2, proj, proj, proj, proj, oa, ob, oc, od, wb, wo, ng]
    out_specs = [rows, rows]
    out_shape = [jax.ShapeDtypeStruct((T, D_MODEL), f32),
                 jax.ShapeDtypeStruct((T, D_MODEL), f32 if routed else bf16)]
    scratch = []
    if routed:
        in_specs += [pl.BlockSpec((D_MODEL, LANES), lambda t: (0, 0)),
                     pl.BlockSpec((1, LANES), lambda t: (0, 0))]
        args += list(router)
        out_specs += [pl.BlockSpec((tm, LANES), lambda t: (t, 0))]
        out_shape += [jax.ShapeDtypeStruct((T, LANES), f32)]
        scratch = [pltpu.VMEM((1, LANES), f32)]
    return pl.pallas_call(
        functools.partial(_merge_kernel, routed=routed),
        grid=(T // tm,),
        in_specs=in_specs,
        out_specs=out_specs,
        out_shape=out_shape,
        scratch_shapes=scratch,
        compiler_params=_cparams("arbitrary" if routed else "parallel"),
        name="merge_route" if routed else "merge",
    )(*args)


def _swiglu_block(h, wg, wu, wd):
    a = jnp.dot(h, wg, preferred_element_type=f32)
    b = jnp.dot(h, wu, preferred_element_type=f32)
    t = (a * _sigmoid(a) * b).astype(bf16)
    return jnp.dot(t, wd, preferred_element_type=f32)


def _ffn_kernel(x_ref, h_ref, wg_ref, wu_ref, wd_ref, o_ref, acc_ref):
    f = pl.program_id(1)
    y = _swiglu_block(h_ref[...], wg_ref[...], wu_ref[...], wd_ref[...])

    @pl.when(f == 0)
    def _():
        acc_ref[...] = x_ref[...] + y

    @pl.when(f > 0)
    def _():
        acc_ref[...] += y

    @pl.when(f == pl.num_programs(1) - 1)
    def _():
        o_ref[...] = acc_ref[...]


def _ffn(x2, h2, wg, wu, wd):
    T = x2.shape[0]
    tm = min(1024, T)
    tf = EXPERT_DIM
    return pl.pallas_call(
        _ffn_kernel,
        grid=(T // tm, FFN_DIM // tf),
        in_specs=[
            pl.BlockSpec((tm, D_MODEL), lambda t, f: (t, 0)),
            pl.BlockSpec((tm, D_MODEL), lambda t, f: (t, 0)),
            pl.BlockSpec((D_MODEL, tf), lambda t, f: (0, f)),
            pl.BlockSpec((D_MODEL, tf), lambda t, f: (0, f)),
            pl.BlockSpec((tf, D_MODEL), lambda t, f: (f, 0)),
        ],
        out_specs=pl.BlockSpec((tm, D_MODEL), lambda t, f: (t, 0)),
        out_shape=jax.ShapeDtypeStruct((T, D_MODEL), f32),
        scratch_shapes=[pltpu.VMEM((tm, D_MODEL), f32)],
        compiler_params=_cparams("parallel", "arbitrary"),
        name="dense_swiglu",
    )(x2, h2, wg, wu, wd)


MOE_TILE = 256
RUN_ALIGN = SUBLANES
TILE_SLOTS = 2 * MOE_TILE + N_EXPERTS * RUN_ALIGN


def _pow2_pieces(limit):
    out, b = [], RUN_ALIGN
    while b <= limit:
        out.append(b)
        b *= 2
    return tuple(reversed(out))


def _pieces(value, limit):
    sizes = _pow2_pieces(limit)
    for b in sizes:
        done = value & (~(2 * b - 1) & (2 * sizes[0] - 1))
        yield (value & b) != 0, done, b


def _expert_runs(cnt_ref, off_ref, row_ref, tile):
    for e in range(N_EXPERTS):
        o = off_ref[tile * N_EXPERTS + e]
        d = row_ref[tile * N_EXPERTS + e]
        for needed, done, size in _pieces(cnt_ref[tile * N_EXPERTS + e], MOE_TILE):
            yield (needed, pl.multiple_of(o + done, RUN_ALIGN), pl.multiple_of(d + done, RUN_ALIGN),
                   size)


def _dispatch_kernel(cnt_ref, off_ref, row_ref, total_ref, pends_ref, slots_ref, h_ref, xs_ref,
                     xc_scr, zero_scr, sems, zero_sem, *, spare_blocks):
    i = pl.program_id(0)
    n = pl.num_programs(0)
    buf = i % 2

    def zero_fill(start):
        start = pl.multiple_of(start, EXPERT_ROWS)
        return pltpu.make_async_copy(zero_scr, xs_ref.at[pl.ds(start, EXPERT_ROWS)], zero_sem)

    @pl.when(i == 0)
    def _():
        zero_scr[...] = jnp.zeros(zero_scr.shape, f32)
        used = pends_ref[N_EXPERTS - 1]
        fills = [(jnp.maximum(pends_ref[e] - EXPERT_ROWS, 0), None) for e in range(N_EXPERTS)]
        fills += [(used + e * EXPERT_ROWS, used + e * EXPERT_ROWS < xs_ref.shape[0])
                  for e in range(spare_blocks)]
        for action in ("start", "wait"):
            for start, needed in fills:
                if needed is None:
                    getattr(zero_fill(start), action)()
                else:
                    @pl.when(needed)
                    def _():
                        getattr(zero_fill(start), action)()

    def tile_done(tile, b):
        for needed, _, size in _pieces(total_ref[tile], 2 * MOE_TILE):
            @pl.when(needed)
            def _():
                pltpu.make_async_copy(xc_scr.at[b, pl.ds(0, size)], xs_ref.at[pl.ds(0, size)],
                                      sems.at[b]).wait()

    @pl.when(i >= 2)
    def _():
        tile_done(i - 2, buf)

    place = _iota((TILE_SLOTS, MOE_TILE), 0)
    onehot = (place == slots_ref[0, 0:1, :]) | (place == slots_ref[0, 1:2, :])
    xc_scr[buf] = jnp.dot(jnp.where(onehot, 1.0, 0.0).astype(bf16), h_ref[...].astype(bf16),
                          preferred_element_type=f32)
    for needed, local, row, size in _expert_runs(cnt_ref, off_ref, row_ref, i):
        @pl.when(needed)
        def _():
            pltpu.make_async_copy(xc_scr.at[buf, pl.ds(local, size)], xs_ref.at[pl.ds(row, size)],
                                  sems.at[buf]).start()

    @pl.when(i == n - 1)
    def _():
        tile_done(i, buf)

        @pl.when(i >= 1)
        def _():
            tile_done(i - 1, 1 - buf)


def _dispatch(h2, slots_t, cnt, off, row, total, pends, rows):
    T = h2.shape[0]
    grid_spec = pltpu.PrefetchScalarGridSpec(
        num_scalar_prefetch=5,
        grid=(T // MOE_TILE,),
        in_specs=[
            pl.BlockSpec((1, 2, MOE_TILE), lambda t, *_: (t, 0, 0)),
            pl.BlockSpec((MOE_TILE, D_MODEL), lambda t, *_: (t, 0)),
        ],
        out_specs=pl.BlockSpec(memory_space=pl.ANY),
        scratch_shapes=[pltpu.VMEM((2, TILE_SLOTS, D_MODEL), f32),
                        pltpu.VMEM((EXPERT_ROWS, D_MODEL), f32),
                        pltpu.SemaphoreType.DMA((2,)), pltpu.SemaphoreType.DMA(())],
    )
    spare_blocks = rows // EXPERT_ROWS - 2 * T // EXPERT_ROWS
    return pl.pallas_call(
        functools.partial(_dispatch_kernel, spare_blocks=spare_blocks),
        grid_spec=grid_spec,
        out_shape=jax.ShapeDtypeStruct((rows, D_MODEL), f32),
        compiler_params=_cparams("arbitrary"),
        name="moe_dispatch",
    )(cnt, off, row, total, pends, slots_t, h2)


def _expert_kernel(be_ref, nv_ref, xs_ref, wg_ref, wu_ref, wd_ref, ys_ref):
    j = pl.program_id(0)

    @pl.when(j < nv_ref[0])
    def _():
        ys_ref[...] = _swiglu_block(xs_ref[...].astype(bf16), wg_ref[0], wu_ref[0], wd_ref[0])

    @pl.when(j >= nv_ref[0])
    def _():
        ys_ref[...] = jnp.zeros(ys_ref.shape, f32)


def _experts(block_e, n_valid, xs, wg, wu, wd):
    rows = xs.shape[0]
    grid_spec = pltpu.PrefetchScalarGridSpec(
        num_scalar_prefetch=2,
        grid=(rows // EXPERT_ROWS,),
        in_specs=[
            pl.BlockSpec((EXPERT_ROWS, D_MODEL), lambda j, be, nv: (jnp.minimum(j, nv[0] - 1), 0)),
            pl.BlockSpec((1, D_MODEL, EXPERT_DIM), lambda j, be, nv: (be[j], 0, 0)),
            pl.BlockSpec((1, D_MODEL, EXPERT_DIM), lambda j, be, nv: (be[j], 0, 0)),
            pl.BlockSpec((1, EXPERT_DIM, D_MODEL), lambda j, be, nv: (be[j], 0, 0)),
        ],
        out_specs=pl.BlockSpec((EXPERT_ROWS, D_MODEL), lambda j, be, nv: (j, 0)),
    )
    return pl.pallas_call(
        _expert_kernel,
        grid_spec=grid_spec,
        out_shape=jax.ShapeDtypeStruct((rows, D_MODEL), f32),
        compiler_params=_cparams("arbitrary"),
        name="moe_experts",
    )(block_e, n_valid, xs, wg, wu, wd)


def _combine_kernel(cnt_ref, off_ref, row_ref, total_ref, slots_ref, x_ref, route_ref, ys_ref,
                    o_ref, yc_scr, sems):
    i = pl.program_id(0)
    n = pl.num_programs(0)
    buf = i % 2

    def fetch(tile, b):
        for needed, local, row, size in _expert_runs(cnt_ref, off_ref, row_ref, tile):
            @pl.when(needed)
            def _():
                pltpu.make_async_copy(ys_ref.at[pl.ds(row, size)], yc_scr.at[b, pl.ds(local, size)],
                                      sems.at[b]).start()

    @pl.when(i == 0)
    def _():
        yc_scr[...] = jnp.zeros(yc_scr.shape, f32)
        fetch(0, 0)

    @pl.when(i + 1 < n)
    def _():
        fetch(i + 1, 1 - buf)

    for needed, _, size in _pieces(total_ref[i], 2 * MOE_TILE):
        @pl.when(needed)
        def _():
            pltpu.make_async_copy(ys_ref.at[pl.ds(0, size)], yc_scr.at[buf, pl.ds(0, size)],
                                  sems.at[buf]).wait()
    y = yc_scr[buf]
    y_hi = y.astype(bf16)
    y_lo = (y - y_hi.astype(f32)).astype(bf16)
    place = _iota((MOE_TILE, TILE_SLOTS), 1)
    route = route_ref[...]
    out = x_ref[...]
    for k, w_lane in ((0, ROUTE_W1), (1, ROUTE_W2)):
        pick = jnp.where(place == slots_ref[:, k:k + 1], 1.0, 0.0).astype(bf16)
        yk = (jnp.dot(pick, y_hi, preferred_element_type=f32)
              + jnp.dot(pick, y_lo, preferred_element_type=f32))
        out = out + route[:, w_lane:w_lane + 1] * yk
    o_ref[...] = out


def _combine(x2, route, slots, cnt, off, row, total, ys):
    T = x2.shape[0]
    grid_spec = pltpu.PrefetchScalarGridSpec(
        num_scalar_prefetch=4,
        grid=(T // MOE_TILE,),
        in_specs=[
            pl.BlockSpec((MOE_TILE, 2), lambda t, *_: (t, 0)),
            pl.BlockSpec((MOE_TILE, D_MODEL), lambda t, *_: (t, 0)),
            pl.BlockSpec((MOE_TILE, LANES), lambda t, *_: (t, 0)),
            pl.BlockSpec(memory_space=pl.ANY),
        ],
        out_specs=pl.BlockSpec((MOE_TILE, D_MODEL), lambda t, *_: (t, 0)),
        scratch_shapes=[pltpu.VMEM((2, TILE_SLOTS, D_MODEL), f32), pltpu.SemaphoreType.DMA((2,))],
    )
    return pl.pallas_call(
        _combine_kernel,
        grid_spec=grid_spec,
        out_shape=jax.ShapeDtypeStruct((T, D_MODEL), f32),
        compiler_params=_cparams("arbitrary"),
        name="moe_combine",
    )(cnt, off, row, total, slots, x2, route, ys)


def _moe(x2, h2, route, wg, wu, wd):
    T = x2.shape[0]
    nt = T // MOE_TILE
    e12 = route[:, ROUTE_E1:ROUTE_E2 + 1].astype(jnp.int32)
    r12 = route[:, ROUTE_R1:ROUTE_R2 + 1].astype(jnp.int32)
    chosen = (e12[:, :, None] == jnp.arange(N_EXPERTS, dtype=jnp.int32)).astype(jnp.int32)
    cnt = chosen.sum(axis=1).reshape(nt, MOE_TILE, N_EXPERTS).sum(axis=1)
    run = (cnt + RUN_ALIGN - 1) // RUN_ALIGN * RUN_ALIGN
    off = jnp.cumsum(run, axis=1) - run
    before = jnp.cumsum(run, axis=0) - run
    padded = (run.sum(axis=0) + EXPERT_ROWS - 1) // EXPERT_ROWS * EXPERT_ROWS
    pends = jnp.cumsum(padded)
    row = (pends - padded)[None, :] + before
    n_blocks = -(-(2 * T + nt * N_EXPERTS * (RUN_ALIGN - 1)) // EXPERT_ROWS) + N_EXPERTS
    block_e = jnp.minimum(
        jnp.searchsorted(pends, jnp.arange(n_blocks, dtype=jnp.int32) * EXPERT_ROWS, side='right'),
        N_EXPERTS - 1).astype(jnp.int32)
    n_valid = (pends[-1:] // EXPERT_ROWS).astype(jnp.int32)
    first_rank = jnp.cumsum(cnt, axis=0) - cnt
    shift = jnp.repeat(off - first_rank, MOE_TILE, axis=0)
    slots = r12 + jnp.take_along_axis(shift, e12, axis=1)
    slots_t = slots.reshape(nt, MOE_TILE, 2).transpose(0, 2, 1)
    flat = lambda a: a.reshape(-1).astype(jnp.int32)
    tables = (flat(run), flat(off), flat(row), flat(run.sum(axis=1)))
    xs = _dispatch(h2, slots_t, *tables, pends.astype(jnp.int32), n_blocks * EXPERT_ROWS)
    ys = _experts(block_e, n_valid, xs, wg, wu, wd)
    return _combine(x2, route, slots, *tables, ys)


def _pad_cols(w, n):
    return jnp.pad(w, ((0, 0), (0, n - w.shape[1])))


def _layer_weights(w_in, c_w_uq, c_w_ukv):
    gate_end = N_BRANCH * D_MODEL
    a_end = gate_end + 2 * WIDTH
    b_end = a_end + 3 * WIDTH
    cq = w_in[:, b_end:b_end + C_Q_RANK]
    ckv = w_in[:, b_end + C_Q_RANK:b_end + C_Q_RANK + C_KV_RANK]
    kr = w_in[:, b_end + C_Q_RANK + C_KV_RANK:b_end + C_Q_RANK + C_KV_RANK + C_ROPE_DIM]
    d_start = b_end + C_Q_RANK + C_KV_RANK + C_ROPE_DIM
    kr_block = jnp.pad(kr, ((0, 0), (C_NOPE_DIM, C_HEAD_PAD - C_QK_DIM)))
    w = jnp.concatenate([w_in[:, :b_end], _pad_cols(cq, WIDTH), ckv, kr_block, w_in[:, d_start:]],
                        axis=1).astype(bf16)
    wuq = c_w_uq.reshape(C_Q_RANK, C_HEADS, C_QK_DIM)
    wuq = jnp.pad(wuq, ((0, WIDTH - C_Q_RANK), (0, 0), (0, C_HEAD_PAD - C_QK_DIM)))
    wuq = wuq.reshape(WIDTH, C_HEADS * C_HEAD_PAD).astype(bf16)
    wukv = c_w_ukv.reshape(C_KV_RANK, C_HEADS, C_NOPE_DIM + C_V_DIM)
    wuk = jnp.pad(wukv[:, :, :C_NOPE_DIM], ((0, 0), (0, 0), (0, C_HEAD_PAD - C_NOPE_DIM)))
    wuk = wuk.reshape(C_KV_RANK, C_HEADS * C_HEAD_PAD).astype(bf16)
    wuv = wukv[:, :, C_NOPE_DIM:].reshape(C_KV_RANK, C_HEADS * C_V_DIM).astype(bf16)
    return w, wuq, wuk, wuv


def _rotation_constants():
    ch = jnp.arange(WIDTH)
    gsum_b = ((ch[:, None] // B_HEAD_DIM) == (ch[None, :] // B_HEAD_DIM)).astype(f32) / B_HEAD_DIM
    half_b = B_ROT_DIM // 2
    src, dst = ch[:, None], ch[None, :]
    same = (src // B_HEAD_DIM) == (dst // B_HEAD_DIM)
    ds_, dd = src % B_HEAD_DIM, dst % B_HEAD_DIM
    rot_b = jnp.where(same & (dd < half_b) & (ds_ == dd + half_b), -1.0,
                      jnp.where(same & (dd >= half_b) & (dd < B_ROT_DIM) & (ds_ == dd - half_b), 1.0, 0.0))
    lc = jnp.arange(C_HEAD_PAD)
    real = lc < C_QK_DIM
    gsum_c = (real[:, None] & real[None, :]).astype(f32) / C_QK_DIM
    half_c = C_ROPE_DIM // 2
    s, d = lc[:, None], lc[None, :]
    lo = (d >= C_NOPE_DIM) & (d < C_NOPE_DIM + half_c)
    hi = (d >= C_NOPE_DIM + half_c) & (d < C_QK_DIM)
    rot_c = jnp.where(lo & (s == d + half_c), -1.0, jnp.where(hi & (s == d - half_c), 1.0, 0.0))
    slot = jnp.arange(ATT_HEADS * HEAD_SLOT)
    place_b = ((slot[None, :] // HEAD_SLOT == ch[:, None] // B_HEAD_DIM)
               & (slot[None, :] % HEAD_SLOT == ch[:, None] % B_HEAD_DIM))
    return (gsum_b.astype(bf16), rot_b.astype(bf16), place_b.astype(bf16),
            gsum_c.astype(bf16), rot_c.astype(bf16))


def kernel(x, positions, attn_norm, ffn_norm, w_in, a_ln_g, a_ln_b, a_ws, a_bs, b_qn, b_kn, c_cq_norm, c_ckv_norm, c_w_uq, c_w_ukv, c_qn, c_kn, d_conv_w, d_conv_b, d_ln_g, d_ln_b, w_branch, w_out, ffn_w_gate, ffn_w_up, ffn_w_down, moe_router, moe_router_b, moe_w_gate, moe_w_up, moe_w_down):
    B, S, _ = x.shape
    T = B * S
    assert S % ATT_BLOCK == 0 and T % EXPERT_ROWS == 0
    x2 = x.reshape(T, D_MODEL)
    cos_b, sin_b, cos_c, sin_c = _rope_tables(positions)
    gsum_b, rot_b, place_b, gsum_c, rot_c = _rotation_constants()
    row = lambda v: v.reshape(1, -1)

    for l in range(DEPTH):
        w, wuq, wuk, wuv = _layer_weights(w_in[l], c_w_uq[l], c_w_ukv[l])
        proj = _inproj(x2, row(attn_norm[l]), w)

        o_a = _sgu(proj, row(a_ln_g[l]), row(a_ln_b[l]), a_ws[l], a_bs[l].T)

        qb, kb, vtb = _moba_prep(proj, B, S, cos_b, sin_b,
                                 row(jnp.tile(b_qn[l], B_HEADS)), row(jnp.tile(b_kn[l], B_HEADS)),
                                 gsum_b, rot_b, place_b)
        o_b = _attention(qb, kb, vtb, B, S, "moba_attention")

        pad_gain = lambda g, n: row(jnp.pad(g, (0, n - g.shape[0])))
        qc, kc, vtc = _mla_prep(proj, B, S, cos_c, sin_c, pad_gain(c_cq_norm[l], WIDTH),
                                row(c_ckv_norm[l]), wuq, wuk, wuv,
                                pad_gain(c_qn[l], C_HEAD_PAD), pad_gain(c_kn[l], C_HEAD_PAD),
                                gsum_c, rot_c)
        o_c = _attention(qc, kc, vtc, B, S, "mla_attention")

        o_d = _conformer(proj, B, S, d_conv_w[l], row(d_conv_b[l]), row(d_ln_g[l]), row(d_ln_b[l]))

        merge_args = (x2, proj, o_a, o_b, o_c, o_d, w_branch[l].astype(bf16),
                      w_out[l].astype(bf16), row(ffn_norm[l]))
        i = l // 2
        if l % 2 == 0:
            x2, h2 = _merge(*merge_args)
            x2 = _ffn(x2, h2, ffn_w_gate[i].astype(bf16), ffn_w_up[i].astype(bf16),
                      ffn_w_down[i].astype(bf16))
        else:
            router = (_pad_cols(moe_router[i], LANES), _pad_cols(row(moe_router_b[i]), LANES))
            x2, h2, route = _merge(*merge_args, router=router)
            x2 = _moe(x2, h2, route, moe_w_gate[i].astype(bf16),
                      moe_w_up[i].astype(bf16), moe_w_down[i].astype(bf16))
    return x2.reshape(B, S, D_MODEL)
```

```python
import functools

import jax
import jax.numpy as jnp
from jax import lax
from jax.experimental import pallas as pl
from jax.experimental.pallas import tpu as pltpu

D_MODEL = 1024
DEPTH = 4
N_BRANCH = 4
ROPE_THETA = 500000.0
NORM_EPS = 1e-6
NEG_INF = -1e30

WIDTH = 256
A_GROUPS = 4
A_GROUP_DIM = 64
A_CHUNK = 128
B_HEADS = 4
B_HEAD_DIM = 64
B_ROT_DIM = 16
MOBA_BLOCK = 256
MOBA_TOPK = 3
C_HEADS = 4
C_Q_RANK = 192
C_KV_RANK = 128
C_NOPE_DIM = 64
C_ROPE_DIM = 32
C_V_DIM = 64
C_QK_DIM = C_NOPE_DIM + C_ROPE_DIM
C_HEAD_PAD = 128
D_CONV = 31
FFN_DIM = 2816
N_EXPERTS = 8
EXPERT_DIM = 1408
EXPERT_ROWS = 512

ATT_BLOCK = 256
ATT_QBLOCK = 512
ATT_HEADS = 4
HEAD_SLOT = 128
V_AUG = C_V_DIM + 16
LOG2E = 1.4426950408889634
SM_ROWS = 32
LANES = 128
SUBLANES = 8
CONV_HALO = 32
CONV_ROWS = 64
LN_ROWS = 128

COL_GATE = 0
COL_AU = 4096
COL_AV = 4352
COL_BQ = 4608
COL_BK = 4864
COL_BV = 5120
COL_CQ = 5376
COL_CKV = 5632
COL_KR = 5760
COL_DA = 5888
COL_DG = 6144
PROJ_COLS = 6400

VMEM_LIMIT = 56 * 1024 * 1024

bf16 = jnp.bfloat16
f32 = jnp.float32


def _cparams(*sem):
    return pltpu.CompilerParams(dimension_semantics=sem, vmem_limit_bytes=VMEM_LIMIT)


def _rms(x, n):
    return x * lax.rsqrt(jnp.sum(x * x, axis=-1, keepdims=True) * (1.0 / n) + NORM_EPS)


def _layernorm(x, g, b):
    mu = jnp.mean(x, axis=-1, keepdims=True)
    xc = x - mu
    return xc * lax.rsqrt(jnp.mean(xc * xc, axis=-1, keepdims=True) + NORM_EPS) * g + b


def _sigmoid(x):
    return 0.5 * jnp.tanh(0.5 * x) + 0.5


def _iota(shape, dim):
    return lax.broadcasted_iota(jnp.int32, shape, dim)


def _rope_table_kernel(pos_ref, inv_ref, eb_ref, ec_ref, cb_ref, sb_ref, cc_ref, sc_ref):
    ang = pos_ref[0].astype(f32) * inv_ref[...]
    c = jnp.cos(ang)
    s = jnp.sin(ang)
    hi = lax.Precision.HIGHEST
    cb_ref[0] = jnp.dot(c, eb_ref[...], precision=hi, preferred_element_type=f32)
    sb_ref[0] = jnp.dot(s, eb_ref[...], precision=hi, preferred_element_type=f32)
    cc_ref[0] = jnp.dot(c, ec_ref[...], precision=hi, preferred_element_type=f32)
    sc_ref[0] = jnp.dot(s, ec_ref[...], precision=hi, preferred_element_type=f32)


def _rope_tables(positions):
    B, S = positions.shape
    R = min(512, S)
    nb = B_ROT_DIM // 2
    nc = C_ROPE_DIM // 2
    inv_b = f32(ROPE_THETA) ** (-jnp.arange(0, B_ROT_DIM, 2, dtype=f32) / B_ROT_DIM)
    inv_c = f32(ROPE_THETA) ** (-jnp.arange(0, C_ROPE_DIM, 2, dtype=f32) / C_ROPE_DIM)
    inv = jnp.zeros((1, LANES), f32).at[0, :nb].set(inv_b).at[0, nb:nb + nc].set(inv_c)
    zero_lane = LANES - 1
    ch = jnp.arange(WIDTH)
    d = ch % B_HEAD_DIM
    src_b = jnp.where(d < B_ROT_DIM, d % nb, zero_lane)
    eb = (jnp.arange(LANES)[:, None] == src_b[None, :]).astype(f32)
    dc = jnp.arange(C_HEAD_PAD)
    in_rope = (dc >= C_NOPE_DIM) & (dc < C_QK_DIM)
    src_c = jnp.where(in_rope, nb + (dc - C_NOPE_DIM) % nc, zero_lane)
    ec = (jnp.arange(LANES)[:, None] == src_c[None, :]).astype(f32)
    pos3 = positions.reshape(B, S, 1)
    out = pl.pallas_call(
        _rope_table_kernel,
        grid=(B, S // R),
        in_specs=[
            pl.BlockSpec((1, R, 1), lambda b, r: (b, r, 0)),
            pl.BlockSpec((1, LANES), lambda b, r: (0, 0)),
            pl.BlockSpec((LANES, WIDTH), lambda b, r: (0, 0)),
            pl.BlockSpec((LANES, C_HEAD_PAD), lambda b, r: (0, 0)),
        ],
        out_specs=[
            pl.BlockSpec((1, R, WIDTH), lambda b, r: (b, r, 0)),
            pl.BlockSpec((1, R, WIDTH), lambda b, r: (b, r, 0)),
            pl.BlockSpec((1, R, C_HEAD_PAD), lambda b, r: (b, r, 0)),
            pl.BlockSpec((1, R, C_HEAD_PAD), lambda b, r: (b, r, 0)),
        ],
        out_shape=[
            jax.ShapeDtypeStruct((B, S, WIDTH), f32),
            jax.ShapeDtypeStruct((B, S, WIDTH), f32),
            jax.ShapeDtypeStruct((B, S, C_HEAD_PAD), f32),
            jax.ShapeDtypeStruct((B, S, C_HEAD_PAD), f32),
        ],
        compiler_params=_cparams("parallel", "parallel"),
        name="rope_tables",
    )(pos3, inv, eb, ec)
    return out


def _inproj_kernel(x_ref, g_ref, w_ref, o_ref, *, col_chunk):
    h = (_rms(x_ref[...], D_MODEL) * g_ref[...]).astype(bf16)
    for c in range(PROJ_COLS // col_chunk):
        sl = slice(c * col_chunk, (c + 1) * col_chunk)
        o_ref[:, sl] = jnp.dot(h, w_ref[:, sl], preferred_element_type=f32).astype(bf16)


def _inproj(x2, g, w):
    T = x2.shape[0]
    tm = min(512, T)
    return pl.pallas_call(
        functools.partial(_inproj_kernel, col_chunk=1280),
        grid=(T // tm,),
        in_specs=[
            pl.BlockSpec((tm, D_MODEL), lambda t: (t, 0)),
            pl.BlockSpec((1, D_MODEL), lambda t: (0, 0)),
            pl.BlockSpec((D_MODEL, PROJ_COLS), lambda t: (0, 0)),
        ],
        out_specs=pl.BlockSpec((tm, PROJ_COLS), lambda t: (t, 0)),
        out_shape=jax.ShapeDtypeStruct((T, PROJ_COLS), bf16),
        compiler_params=_cparams("parallel"),
        name="inproj",
    )(x2, g, w)


def _sgu_kernel(u_ref, v_ref, g_ref, b_ref, ws_ref, bst_ref, o_ref, *, chunks):
    row = _iota((A_CHUNK, A_CHUNK), 0)
    col = _iota((A_CHUNK, A_CHUNK), 1)
    causal = col <= row
    group = _iota((A_CHUNK, WIDTH), 1) // A_GROUP_DIM
    wm = [jnp.where(causal, ws_ref[g], 0.0).astype(bf16) for g in range(A_GROUPS)]
    bias = jnp.zeros((A_CHUNK, WIDTH), f32)
    for g in range(A_GROUPS):
        bias = jnp.where(group == g, bst_ref[:, g:g + 1], bias)
    for c in range(chunks):
        rows = slice(c * A_CHUNK, (c + 1) * A_CHUNK)
        vn = _layernorm(v_ref[rows, :].astype(f32), g_ref[...], b_ref[...]).astype(bf16)
        z = bias
        for g in range(A_GROUPS):
            zg = jnp.dot(wm[g], vn, preferred_element_type=f32)
            z = z + jnp.where(group == g, zg, 0.0)
        o_ref[rows, :] = (u_ref[rows, :].astype(f32) * z).astype(bf16)


def _sgu(proj, ln_g, ln_b, ws, bs_t):
    T = proj.shape[0]
    ta = min(512, T)
    cu, cv = COL_AU // WIDTH, COL_AV // WIDTH
    return pl.pallas_call(
        functools.partial(_sgu_kernel, chunks=ta // A_CHUNK),
        grid=(T // ta,),
        in_specs=[
            pl.BlockSpec((ta, WIDTH), lambda t: (t, cu)),
            pl.BlockSpec((ta, WIDTH), lambda t: (t, cv)),
            pl.BlockSpec((1, WIDTH), lambda t: (0, 0)),
            pl.BlockSpec((1, WIDTH), lambda t: (0, 0)),
            pl.BlockSpec((A_GROUPS, A_CHUNK, A_CHUNK), lambda t: (0, 0, 0)),
            pl.BlockSpec((A_CHUNK, A_GROUPS), lambda t: (0, 0)),
        ],
        out_specs=pl.BlockSpec((ta, WIDTH), lambda t: (t, 0)),
        out_shape=jax.ShapeDtypeStruct((T, WIDTH), bf16),
        compiler_params=_cparams("parallel"),
        name="sgu",
    )(proj, proj, ln_g, ln_b, ws, bs_t)


def _conv_kernel(a_ref, g_ref, w_ref, cb_ref, lg_ref, lb_ref, o_ref, z_scr, *, seq):
    first = CONV_HALO - (D_CONV - 1)
    z_scr[0, 0:CONV_HALO, :] = jnp.zeros((CONV_HALO, WIDTH), f32)
    z_scr[0, CONV_HALO + seq:, :] = jnp.zeros((CONV_ROWS, WIDTH), f32)

    def glu(c, carry):
        base = pl.multiple_of(c * LN_ROWS, LN_ROWS)
        rows = pl.ds(base, LN_ROWS)
        z_scr[0, pl.ds(base + CONV_HALO, LN_ROWS), :] = (
            a_ref[rows, :].astype(f32) * _sigmoid(g_ref[rows, :].astype(f32)))
        return carry

    lax.fori_loop(0, seq // LN_ROWS, glu, 0)

    def shift(c, carry):
        base = pl.multiple_of(c * CONV_ROWS, CONV_ROWS)
        win = z_scr[0, pl.ds(base, CONV_ROWS + SUBLANES), :]
        for r in range(1, SUBLANES):
            rolled = pltpu.roll(win, shift=CONV_ROWS + SUBLANES - r, axis=0)
            z_scr[r, pl.ds(base, CONV_ROWS), :] = rolled[:CONV_ROWS]
        return carry

    lax.fori_loop(0, seq // CONV_ROWS + 1, shift, 0)

    def taps(c, carry):
        base = pl.multiple_of(c * CONV_ROWS, CONV_ROWS)
        acc = jnp.zeros((CONV_ROWS, WIDTH), f32) + cb_ref[...]
        for k in range(D_CONV):
            off = first + k
            lo = pl.multiple_of(base + off // SUBLANES * SUBLANES, SUBLANES)
            acc = acc + w_ref[k:k + 1, :] * z_scr[off % SUBLANES, pl.ds(lo, CONV_ROWS), :]
        z_scr[1, pl.ds(base, CONV_ROWS), :] = acc
        return carry

    lax.fori_loop(0, seq // CONV_ROWS, taps, 0)

    def norm(c, carry):
        base = pl.multiple_of(c * LN_ROWS, LN_ROWS)
        y = _layernorm(z_scr[1, pl.ds(base, LN_ROWS), :], lg_ref[...], lb_ref[...])
        o_ref[pl.ds(base, LN_ROWS), :] = (y * _sigmoid(y)).astype(bf16)
        return carry

    lax.fori_loop(0, seq // LN_ROWS, norm, 0, unroll=2)


def _conformer(proj, B, S, conv_w, conv_b, ln_g, ln_b):
    ca, cg = COL_DA // WIDTH, COL_DG // WIDTH
    return pl.pallas_call(
        functools.partial(_conv_kernel, seq=S),
        grid=(B,),
        in_specs=[
            pl.BlockSpec((S, WIDTH), lambda b: (b, ca)),
            pl.BlockSpec((S, WIDTH), lambda b: (b, cg)),
            pl.BlockSpec((D_CONV, WIDTH), lambda b: (0, 0)),
            pl.BlockSpec((1, WIDTH), lambda b: (0, 0)),
            pl.BlockSpec((1, WIDTH), lambda b: (0, 0)),
            pl.BlockSpec((1, WIDTH), lambda b: (0, 0)),
        ],
        out_specs=pl.BlockSpec((S, WIDTH), lambda b: (b, 0)),
        out_shape=jax.ShapeDtypeStruct((B * S, WIDTH), bf16),
        scratch_shapes=[pltpu.VMEM((SUBLANES, CONV_HALO + S + CONV_ROWS, WIDTH), f32)],
        compiler_params=_cparams("parallel"),
        name="conformer_conv",
    )(proj, proj, conv_w, conv_b, ln_g, ln_b)


def _value_rows(v_tok):
    vt = v_tok.T
    ones = jnp.ones((V_AUG - C_V_DIM, vt.shape[1]), f32)
    rows = []
    for h in range(ATT_HEADS):
        rows += [vt[h * C_V_DIM:(h + 1) * C_V_DIM], ones]
    return jnp.concatenate(rows, axis=0).astype(bf16)


def _moba_prep_kernel(q_ref, k_ref, v_ref, cos_ref, sin_ref, qg_ref, kg_ref, gsum_ref, rot_ref,
                      place_ref, qo_ref, ko_ref, vt_ref, kmean_scr, *, nblk):
    c = pl.program_id(1)

    @pl.when(c == 0)
    def _():
        kmean_scr[...] = jnp.zeros(kmean_scr.shape, f32)

    cos = cos_ref[0]
    sin = sin_ref[0]

    def norm_rope(x, gain):
        ss = jnp.dot((x * x).astype(bf16), gsum_ref[...], preferred_element_type=f32)
        xn = x * lax.rsqrt(ss + NORM_EPS) * gain
        xr = jnp.dot(xn.astype(bf16), rot_ref[...], preferred_element_type=f32)
        return xn * cos + xr * sin

    q = norm_rope(q_ref[...].astype(f32), qg_ref[...])
    k = norm_rope(k_ref[...].astype(f32), kg_ref[...])
    slot_lane = _iota((ATT_BLOCK, ATT_HEADS * HEAD_SLOT), 1) % HEAD_SLOT
    k_slots = jnp.dot(k.astype(bf16), place_ref[...], preferred_element_type=f32)
    ko_ref[0, 0] = jnp.where(slot_lane == B_HEAD_DIM + c, 1.0, k_slots).astype(bf16)
    vt_ref[0, 0] = _value_rows(v_ref[...].astype(f32))

    head = _iota((nblk, WIDTH), 1) // B_HEAD_DIM
    blk = _iota((nblk, ATT_BLOCK), 0)
    past = blk < c
    kmean = kmean_scr[...]
    zeros_lo = jnp.zeros((B_HEAD_DIM, ATT_BLOCK), f32)
    zeros_hi = jnp.zeros((HEAD_SLOT - B_HEAD_DIM - nblk, ATT_BLOCK), f32)
    bias_rows = []
    for h in range(B_HEADS):
        km = jnp.where(head == h, kmean, 0.0)
        sc = lax.dot_general(km, q, (((1,), (1,)), ((), ())), precision=lax.Precision.HIGHEST,
                             preferred_element_type=f32)
        sc = jnp.where(past, sc, NEG_INF)
        rank = jnp.zeros((nblk, ATT_BLOCK), jnp.int32)
        for j in range(nblk):
            other = sc[j:j + 1, :]
            ahead = (other > sc) | ((other == sc) & (j < blk))
            rank = rank + ahead.astype(jnp.int32)
        visible = (past & (rank < MOBA_TOPK)) | (blk == c)
        bias_rows += [zeros_lo, jnp.where(visible, 0.0, NEG_INF), zeros_hi]
    bias = jnp.concatenate(bias_rows, axis=0)
    q_slots = jnp.dot((q * (B_HEAD_DIM ** -0.5 * LOG2E)).astype(bf16), place_ref[...],
                      preferred_element_type=f32)
    qo_ref[0] = (q_slots.T + bias).astype(bf16)

    kmean_scr[pl.ds(c, 1), :] = jnp.mean(k, axis=0, keepdims=True)


def _moba_prep(proj, B, S, cos_b, sin_b, qg, kg, gsum, rot, place):
    nblk = S // ATT_BLOCK
    assert nblk <= HEAD_SLOT - B_HEAD_DIM
    HD = ATT_HEADS * HEAD_SLOT
    cq, ck, cv = COL_BQ // WIDTH, COL_BK // WIDTH, COL_BV // WIDTH
    tile = lambda col: pl.BlockSpec((ATT_BLOCK, WIDTH), lambda b, c: (b * nblk + c, col))
    table = pl.BlockSpec((1, ATT_BLOCK, WIDTH), lambda b, c: (b, c, 0))
    const = lambda shape: pl.BlockSpec(shape, lambda b, c: (0,) * len(shape))
    return pl.pallas_call(
        functools.partial(_moba_prep_kernel, nblk=nblk),
        grid=(B, nblk),
        in_specs=[tile(cq), tile(ck), tile(cv), table, table,
                  const((1, WIDTH)), const((1, WIDTH)), const((WIDTH, WIDTH)), const((WIDTH, WIDTH)),
                  const((WIDTH, HD))],
        out_specs=[
            pl.BlockSpec((1, HD, ATT_BLOCK), lambda b, c: (b, 0, c)),
            pl.BlockSpec((1, 1, ATT_BLOCK, HD), lambda b, c: (b, c, 0, 0)),
            pl.BlockSpec((1, 1, ATT_HEADS * V_AUG, ATT_BLOCK), lambda b, c: (b, c, 0, 0)),
        ],
        out_shape=[
            jax.ShapeDtypeStruct((B, HD, S), bf16),
            jax.ShapeDtypeStruct((B, nblk, ATT_BLOCK, HD), bf16),
            jax.ShapeDtypeStruct((B, nblk, ATT_HEADS * V_AUG, ATT_BLOCK), bf16),
        ],
        scratch_shapes=[pltpu.VMEM((nblk, WIDTH), f32)],
        compiler_params=_cparams("parallel", "arbitrary"),
        name="moba_prep",
    )(proj, proj, proj, cos_b, sin_b, qg, kg, gsum, rot, place)


def _mla_prep_kernel(cq_ref, ckv_ref, kr_ref, cos_ref, sin_ref, cqg_ref, ckvg_ref, wuq_ref, wuk_ref,
                     wuv_ref, qg_ref, kg_ref, gsum_ref, rot_ref, qo_ref, ko_ref, vt_ref):
    cqn = (_rms(cq_ref[...].astype(f32), C_Q_RANK) * cqg_ref[...]).astype(bf16)
    ckvn = (_rms(ckv_ref[...].astype(f32), C_KV_RANK) * ckvg_ref[...]).astype(bf16)
    q = jnp.dot(cqn, wuq_ref[...], preferred_element_type=f32)
    k = jnp.dot(ckvn, wuk_ref[...], preferred_element_type=f32)
    v = jnp.dot(ckvn, wuv_ref[...], preferred_element_type=f32)
    kr = kr_ref[...].astype(f32)
    cos = cos_ref[0]
    sin = sin_ref[0]

    def norm_rope(x, gain):
        ss = jnp.dot((x * x).astype(bf16), gsum_ref[...], preferred_element_type=f32)
        xn = x * lax.rsqrt(ss + NORM_EPS) * gain
        xr = jnp.dot(xn.astype(bf16), rot_ref[...], preferred_element_type=f32)
        return xn * cos + xr * sin

    for h in range(C_HEADS):
        sl = slice(h * C_HEAD_PAD, (h + 1) * C_HEAD_PAD)
        qh = norm_rope(q[:, sl], qg_ref[...])
        kh = norm_rope(k[:, sl] + kr, kg_ref[...])
        qo_ref[0, sl, :] = (qh * (C_QK_DIM ** -0.5 * LOG2E)).T.astype(bf16)
        ko_ref[0, 0, :, sl] = kh.astype(bf16)
    vt_ref[0, 0] = _value_rows(v)


def _mla_prep(proj, B, S, cos_c, sin_c, cqg, ckvg, wuq, wuk, wuv, qg, kg, gsum, rot):
    nblk = S // ATT_BLOCK
    HD = C_HEADS * C_HEAD_PAD
    table = pl.BlockSpec((1, ATT_BLOCK, C_HEAD_PAD), lambda b, c: (b, c, 0))
    const = lambda shape: pl.BlockSpec(shape, lambda b, c: (0,) * len(shape))
    return pl.pallas_call(
        _mla_prep_kernel,
        grid=(B, nblk),
        in_specs=[
            pl.BlockSpec((ATT_BLOCK, WIDTH), lambda b, c: (b * nblk + c, COL_CQ // WIDTH)),
            pl.BlockSpec((ATT_BLOCK, LANES), lambda b, c: (b * nblk + c, COL_CKV // LANES)),
            pl.BlockSpec((ATT_BLOCK, LANES), lambda b, c: (b * nblk + c, COL_KR // LANES)),
            table, table,
            const((1, WIDTH)), const((1, C_KV_RANK)), const((WIDTH, HD)), const((C_KV_RANK, HD)),
            const((C_KV_RANK, WIDTH)), const((1, C_HEAD_PAD)), const((1, C_HEAD_PAD)),
            const((C_HEAD_PAD, C_HEAD_PAD)), const((C_HEAD_PAD, C_HEAD_PAD)),
        ],
        out_specs=[
            pl.BlockSpec((1, HD, ATT_BLOCK), lambda b, c: (b, 0, c)),
            pl.BlockSpec((1, 1, ATT_BLOCK, HD), lambda b, c: (b, c, 0, 0)),
            pl.BlockSpec((1, 1, ATT_HEADS * V_AUG, ATT_BLOCK), lambda b, c: (b, c, 0, 0)),
        ],
        out_shape=[
            jax.ShapeDtypeStruct((B, HD, S), bf16),
            jax.ShapeDtypeStruct((B, nblk, ATT_BLOCK, HD), bf16),
            jax.ShapeDtypeStruct((B, nblk, ATT_HEADS * V_AUG, ATT_BLOCK), bf16),
        ],
        compiler_params=_cparams("parallel", "parallel"),
        name="mla_prep",
    )(proj, proj, proj, cos_c, sin_c, cqg, ckvg, wuq, wuk, wuv, qg, kg, gsum, rot)


def _attn_kernel(q_ref, k_ref, vt_ref, o_ref, *scratch):
    H = ATT_HEADS
    s_scr, p_scr, a_scr = ([scratch[(2 * n + par) * H:(2 * n + par + 1) * H] for par in range(2)]
                           for n in range(3))
    acc_scr, m_scr = (scratch[n * H:(n + 1) * H] for n in (6, 7))
    out_scr = scratch[8 * H]
    pairs = pl.program_id(1) * (ATT_QBLOCK // ATT_BLOCK // 2)
    chunks = ATT_BLOCK // SM_ROWS

    def scores(j, par):
        for h in range(H):
            slot = slice(h * HEAD_SLOT, (h + 1) * HEAD_SLOT)
            s_scr[par][h][...] = jnp.dot(k_ref[0, j, :, slot], q_ref[0, slot, :],
                                         preferred_element_type=f32)

    def softmax(par, diag_offset=None):
        for h in range(H):
            def rows(r):
                s = s_scr[par][h][r * SM_ROWS:(r + 1) * SM_ROWS, :]
                if diag_offset is not None:
                    kpos = _iota((SM_ROWS, ATT_QBLOCK), 0) + (r * SM_ROWS + diag_offset)
                    s = jnp.where(kpos <= _iota((SM_ROWS, ATT_QBLOCK), 1), s, NEG_INF)
                return s
            mx = rows(0)
            for r in range(1, chunks):
                mx = jnp.maximum(mx, rows(r))
            m_old = m_scr[h][...]
            m_new = jnp.maximum(m_old, jnp.max(mx, axis=0, keepdims=True))
            a_scr[par][h][...] = jnp.exp2(m_old - m_new)
            m_scr[h][...] = m_new
            for r in range(chunks):
                p_scr[par][h][r * SM_ROWS:(r + 1) * SM_ROWS, :] = (
                    jnp.exp2((rows(r) - m_new).astype(bf16)))

    def values(j, par):
        for h in range(H):
            pv = jnp.dot(vt_ref[0, j, h * V_AUG:(h + 1) * V_AUG, :], p_scr[par][h][...],
                         preferred_element_type=f32)
            acc_scr[h][...] = a_scr[par][h][...] * acc_scr[h][...] + pv

    for h in range(H):
        m_scr[h][...] = jnp.full(m_scr[h].shape, NEG_INF, f32)
        a_scr[1][h][...] = jnp.zeros(a_scr[1][h].shape, f32)
        acc_scr[h][...] = jnp.zeros(acc_scr[h].shape, f32)
        p_scr[1][h][...] = jnp.zeros(p_scr[1][h].shape, bf16)
    scores(0, 0)

    def step(b, par, diag_offset=None, last=False):
        values(jnp.maximum(b - 1, 0), 1 - par)
        softmax(par, diag_offset)
        if not last:
            scores(b + 1, 1 - par)

    def past_pair(t, carry):
        step(2 * t, 0)
        step(2 * t + 1, 1)
        return carry

    lax.fori_loop(0, pairs, past_pair, 0)
    step(2 * pairs, 0, diag_offset=0)
    step(2 * pairs + 1, 1, diag_offset=ATT_BLOCK, last=True)
    values(2 * pairs + 1, 1)

    for h in range(H):
        a = acc_scr[h][...]
        out_scr[h * C_V_DIM:(h + 1) * C_V_DIM, :] = a[:C_V_DIM] / a[C_V_DIM:C_V_DIM + 1]
    o_ref[...] = out_scr[...].T.astype(bf16)


def _attention(q, k, vt, B, S, name):
    nblk = S // ATT_BLOCK
    nq = S // ATT_QBLOCK
    HD = ATT_HEADS * HEAD_SLOT
    return pl.pallas_call(
        _attn_kernel,
        grid=(B, nq),
        in_specs=[
            pl.BlockSpec((1, HD, ATT_QBLOCK), lambda b, i: (b, 0, i)),
            pl.BlockSpec((1, nblk, ATT_BLOCK, HD), lambda b, i: (b, 0, 0, 0)),
            pl.BlockSpec((1, nblk, ATT_HEADS * V_AUG, ATT_BLOCK), lambda b, i: (b, 0, 0, 0)),
        ],
        out_specs=pl.BlockSpec((ATT_QBLOCK, WIDTH), lambda b, i: (b * nq + i, 0)),
        out_shape=jax.ShapeDtypeStruct((B * S, WIDTH), bf16),
        scratch_shapes=(
            [pltpu.VMEM((ATT_BLOCK, ATT_QBLOCK), f32) for _ in range(2 * ATT_HEADS)]
            + [pltpu.VMEM((ATT_BLOCK, ATT_QBLOCK), bf16) for _ in range(2 * ATT_HEADS)]
            + [pltpu.VMEM((1, ATT_QBLOCK), f32) for _ in range(2 * ATT_HEADS)]
            + [pltpu.VMEM((V_AUG, ATT_QBLOCK), f32) for _ in range(ATT_HEADS)]
            + [pltpu.VMEM((1, ATT_QBLOCK), f32) for _ in range(ATT_HEADS)]
            + [pltpu.VMEM((WIDTH, ATT_QBLOCK), f32)]),
        compiler_params=_cparams("parallel", "arbitrary"),
        name=name,
    )(q, k, vt)


ROUTE_E1, ROUTE_E2, ROUTE_W1, ROUTE_W2, ROUTE_R1, ROUTE_R2 = range(6)


def _route_rows(h, wr_ref, br_ref, route_ref, carry_scr):
    tm = h.shape[0]

    @pl.when(pl.program_id(0) == 0)
    def _():
        carry_scr[...] = jnp.zeros(carry_scr.shape, f32)

    lane = _iota((tm, LANES), 1)
    h_hi = h.astype(bf16)
    h_lo = (h - h_hi.astype(f32)).astype(bf16)
    w = wr_ref[...]
    w_hi = w.astype(bf16)
    w_lo = (w - w_hi.astype(f32)).astype(bf16)
    logits = (jnp.dot(h_hi, w_hi, preferred_element_type=f32)
              + (jnp.dot(h_hi, w_lo, preferred_element_type=f32)
                 + jnp.dot(h_lo, w_hi, preferred_element_type=f32))) + br_ref[...]
    logits = jnp.where(lane < N_EXPERTS, logits, NEG_INF)
    m1 = jnp.max(logits, axis=-1, keepdims=True)
    e1 = jnp.min(jnp.where(logits == m1, lane, LANES), axis=-1, keepdims=True)
    rest = jnp.where(lane == e1, NEG_INF, logits)
    m2 = jnp.max(rest, axis=-1, keepdims=True)
    e2 = jnp.min(jnp.where(rest == m2, lane, LANES), axis=-1, keepdims=True)
    ex = jnp.exp(m2 - m1)
    w1 = 1.0 / (1.0 + ex)
    w2 = ex / (1.0 + ex)
    chosen = ((lane == e1) | (lane == e2)).astype(f32)
    earlier = (_iota((tm, tm), 1) < _iota((tm, tm), 0)).astype(bf16)
    before = jnp.dot(earlier, chosen.astype(bf16), preferred_element_type=f32) + carry_scr[...]
    r1 = jnp.sum(jnp.where(lane == e1, before, 0.0), axis=-1, keepdims=True)
    r2 = jnp.sum(jnp.where(lane == e2, before, 0.0), axis=-1, keepdims=True)
    out = jnp.zeros((tm, LANES), f32)
    for idx, val in ((ROUTE_E1, e1.astype(f32)), (ROUTE_E2, e2.astype(f32)), (ROUTE_W1, w1),
                     (ROUTE_W2, w2), (ROUTE_R1, r1), (ROUTE_R2, r2)):
        out = jnp.where(lane == idx, val, out)
    route_ref[...] = out
    carry_scr[...] += jnp.sum(chosen, axis=0, keepdims=True)


def _merge_kernel(x_ref, g0_ref, g1_ref, g2_ref, g3_ref, oa_ref, ob_ref, oc_ref, od_ref,
                  wb_ref, wo_ref, ng_ref, *rest, routed):
    y = None
    for n, (g_ref, o_ref) in enumerate(((g0_ref, oa_ref), (g1_ref, ob_ref),
                                        (g2_ref, oc_ref), (g3_ref, od_ref))):
        br = jnp.dot(o_ref[...], wb_ref[n], preferred_element_type=f32)
        t = _sigmoid(g_ref[...].astype(f32)) * br
        y = t if y is None else y + t
    x = x_ref[...] + jnp.dot(y.astype(bf16), wo_ref[...], preferred_element_type=f32)
    h = _rms(x, D_MODEL) * ng_ref[...]
    if routed:
        wr_ref, br_ref, xo_ref, ho_ref, route_ref, carry_scr = rest
        ho_ref[...] = h
        _route_rows(h, wr_ref, br_ref, route_ref, carry_scr)
    else:
        xo_ref, ho_ref = rest
        ho_ref[...] = h.astype(bf16)
    xo_ref[...] = x


def _merge(x2, proj, oa, ob, oc, od, wb, wo, ng, router=None):
    T = x2.shape[0]
    tm = min(512, T)
    routed = router is not None
    gate = lambda n: pl.BlockSpec((tm, D_MODEL), lambda t: (t, n))
    branch = pl.BlockSpec((tm, WIDTH), lambda t: (t, 0))
    rows = pl.BlockSpec((tm, D_MODEL), lambda t: (t, 0))
    in_specs = [
        rows, gate(0), gate(1), gate(2), gate(3), branch, branch, branch, branch,
        pl.BlockSpec((N_BRANCH, WIDTH, D_MODEL), lambda t: (0, 0, 0)),
        pl.BlockSpec((D_MODEL, D_MODEL), lambda t: (0, 0)),
        pl.BlockSpec((1, D_MODEL), lambda t: (0, 0)),
    ]
    args = [x2, proj, proj, proj, proj, oa, ob, oc, od, wb, wo, ng]
    out_specs = [rows, rows]
    out_shape = [jax.ShapeDtypeStruct((T, D_MODEL), f32),
                 jax.ShapeDtypeStruct((T, D_MODEL), f32 if routed else bf16)]
    scratch = []
    if routed:
        in_specs += [pl.BlockSpec((D_MODEL, LANES), lambda t: (0, 0)),
                     pl.BlockSpec((1, LANES), lambda t: (0, 0))]
        args += list(router)
        out_specs += [pl.BlockSpec((tm, LANES), lambda t: (t, 0))]
        out_shape += [jax.ShapeDtypeStruct((T, LANES), f32)]
        scratch = [pltpu.VMEM((1, LANES), f32)]
    return pl.pallas_call(
        functools.partial(_merge_kernel, routed=routed),
        grid=(T // tm,),
        in_specs=in_specs,
        out_specs=out_specs,
        out_shape=out_shape,
        scratch_shapes=scratch,
        compiler_params=_cparams("arbitrary" if routed else "parallel"),
        name="merge_route" if routed else "merge",
    )(*args)


def _swiglu_block(h, wg, wu, wd):
    a = jnp.dot(h, wg, preferred_element_type=f32)
    b = jnp.dot(h, wu, preferred_element_type=f32)
    t = (a * _sigmoid(a) * b).astype(bf16)
    return jnp.dot(t, wd, preferred_element_type=f32)


FFN_CHUNKS = ((0, 1536), (1536, FFN_DIM))


def _ffn_kernel(x_ref, h_ref, wg_ref, wu_ref, wd_ref, o_ref):
    h = h_ref[...]
    acc = x_ref[...]
    for lo, hi in FFN_CHUNKS:
        acc = acc + _swiglu_block(h, wg_ref[:, lo:hi], wu_ref[:, lo:hi], wd_ref[lo:hi, :])
    o_ref[...] = acc


def _ffn(x2, h2, wg, wu, wd):
    T = x2.shape[0]
    tm = min(512, T)
    rows = pl.BlockSpec((tm, D_MODEL), lambda t: (t, 0))
    resident = lambda shape: pl.BlockSpec(shape, lambda t: (0, 0), pipeline_mode=pl.Buffered(1))
    return pl.pallas_call(
        _ffn_kernel,
        grid=(T // tm,),
        in_specs=[rows, rows, resident((D_MODEL, FFN_DIM)), resident((D_MODEL, FFN_DIM)),
                  resident((FFN_DIM, D_MODEL))],
        out_specs=rows,
        out_shape=jax.ShapeDtypeStruct((T, D_MODEL), f32),
        compiler_params=_cparams("parallel"),
        name="dense_swiglu",
    )(x2, h2, wg, wu, wd)


MOE_TILE = 256
RUN_ALIGN = SUBLANES
TILE_SLOTS = 2 * MOE_TILE + N_EXPERTS * RUN_ALIGN


def _pow2_pieces(limit):
    out, b = [], RUN_ALIGN
    while b <= limit:
        out.append(b)
        b *= 2
    return tuple(reversed(out))


def _pieces(value, limit):
    sizes = _pow2_pieces(limit)
    for b in sizes:
        done = value & (~(2 * b - 1) & (2 * sizes[0] - 1))
        yield (value & b) != 0, done, b


def _expert_runs(cnt_ref, off_ref, row_ref, tile):
    for e in range(N_EXPERTS):
        o = off_ref[tile * N_EXPERTS + e]
        d = row_ref[tile * N_EXPERTS + e]
        for needed, done, size in _pieces(cnt_ref[tile * N_EXPERTS + e], MOE_TILE):
            yield (needed, pl.multiple_of(o + done, RUN_ALIGN), pl.multiple_of(d + done, RUN_ALIGN),
                   size)


def _dispatch_kernel(cnt_ref, off_ref, row_ref, total_ref, pends_ref, slots_ref, h_ref, xs_ref,
                     xc_scr, zero_scr, sems, zero_sem, *, spare_blocks):
    i = pl.program_id(0)
    n = pl.num_programs(0)
    buf = i % 2

    def zero_fill(start):
        start = pl.multiple_of(start, EXPERT_ROWS)
        return pltpu.make_async_copy(zero_scr, xs_ref.at[pl.ds(start, EXPERT_ROWS)], zero_sem)

    @pl.when(i == 0)
    def _():
        zero_scr[...] = jnp.zeros(zero_scr.shape, f32)
        used = pends_ref[N_EXPERTS - 1]
        fills = [(jnp.maximum(pends_ref[e] - EXPERT_ROWS, 0), None) for e in range(N_EXPERTS)]
        fills += [(used + e * EXPERT_ROWS, used + e * EXPERT_ROWS < xs_ref.shape[0])
                  for e in range(spare_blocks)]
        for action in ("start", "wait"):
            for start, needed in fills:
                if needed is None:
                    getattr(zero_fill(start), action)()
                else:
                    @pl.when(needed)
                    def _():
                        getattr(zero_fill(start), action)()

    def tile_done(tile, b):
        for needed, _, size in _pieces(total_ref[tile], 2 * MOE_TILE):
            @pl.when(needed)
            def _():
                pltpu.make_async_copy(xc_scr.at[b, pl.ds(0, size)], xs_ref.at[pl.ds(0, size)],
                                      sems.at[b]).wait()

    @pl.when(i >= 2)
    def _():
        tile_done(i - 2, buf)

    place = _iota((TILE_SLOTS, MOE_TILE), 0)
    onehot = (place == slots_ref[0, 0:1, :]) | (place == slots_ref[0, 1:2, :])
    xc_scr[buf] = jnp.dot(jnp.where(onehot, 1.0, 0.0).astype(bf16), h_ref[...].astype(bf16),
                          preferred_element_type=f32)
    for needed, local, row, size in _expert_runs(cnt_ref, off_ref, row_ref, i):
        @pl.when(needed)
        def _():
            pltpu.make_async_copy(xc_scr.at[buf, pl.ds(local, size)], xs_ref.at[pl.ds(row, size)],
                                  sems.at[buf]).start()

    @pl.when(i == n - 1)
    def _():
        tile_done(i, buf)

        @pl.when(i >= 1)
        def _():
            tile_done(i - 1, 1 - buf)


def _dispatch(h2, slots_t, cnt, off, row, total, pends, rows):
    T = h2.shape[0]
    grid_spec = pltpu.PrefetchScalarGridSpec(
        num_scalar_prefetch=5,
        grid=(T // MOE_TILE,),
        in_specs=[
            pl.BlockSpec((1, 2, MOE_TILE), lambda t, *_: (t, 0, 0)),
            pl.BlockSpec((MOE_TILE, D_MODEL), lambda t, *_: (t, 0)),
        ],
        out_specs=pl.BlockSpec(memory_space=pl.ANY),
        scratch_shapes=[pltpu.VMEM((2, TILE_SLOTS, D_MODEL), f32),
                        pltpu.VMEM((EXPERT_ROWS, D_MODEL), f32),
                        pltpu.SemaphoreType.DMA((2,)), pltpu.SemaphoreType.DMA(())],
    )
    spare_blocks = rows // EXPERT_ROWS - 2 * T // EXPERT_ROWS
    return pl.pallas_call(
        functools.partial(_dispatch_kernel, spare_blocks=spare_blocks),
        grid_spec=grid_spec,
        out_shape=jax.ShapeDtypeStruct((rows, D_MODEL), f32),
        compiler_params=_cparams("arbitrary"),
        name="moe_dispatch",
    )(cnt, off, row, total, pends, slots_t, h2)


def _expert_kernel(be_ref, nv_ref, xs_ref, wg_ref, wu_ref, wd_ref, ys_ref):
    j = pl.program_id(0)

    @pl.when(j < nv_ref[0])
    def _():
        ys_ref[...] = _swiglu_block(xs_ref[...].astype(bf16), wg_ref[0], wu_ref[0], wd_ref[0])

    @pl.when(j >= nv_ref[0])
    def _():
        ys_ref[...] = jnp.zeros(ys_ref.shape, f32)


def _experts(block_e, n_valid, xs, wg, wu, wd):
    rows = xs.shape[0]
    grid_spec = pltpu.PrefetchScalarGridSpec(
        num_scalar_prefetch=2,
        grid=(rows // EXPERT_ROWS,),
        in_specs=[
            pl.BlockSpec((EXPERT_ROWS, D_MODEL), lambda j, be, nv: (jnp.minimum(j, nv[0] - 1), 0)),
            pl.BlockSpec((1, D_MODEL, EXPERT_DIM), lambda j, be, nv: (be[j], 0, 0)),
            pl.BlockSpec((1, D_MODEL, EXPERT_DIM), lambda j, be, nv: (be[j], 0, 0)),
            pl.BlockSpec((1, EXPERT_DIM, D_MODEL), lambda j, be, nv: (be[j], 0, 0)),
        ],
        out_specs=pl.BlockSpec((EXPERT_ROWS, D_MODEL), lambda j, be, nv: (j, 0)),
    )
    return pl.pallas_call(
        _expert_kernel,
        grid_spec=grid_spec,
        out_shape=jax.ShapeDtypeStruct((rows, D_MODEL), f32),
        compiler_params=_cparams("arbitrary"),
        name="moe_experts",
    )(block_e, n_valid, xs, wg, wu, wd)


def _combine_kernel(cnt_ref, off_ref, row_ref, total_ref, slots_ref, x_ref, route_ref, ys_ref,
                    o_ref, yc_scr, sems):
    i = pl.program_id(0)
    n = pl.num_programs(0)
    buf = i % 2

    def fetch(tile, b):
        for needed, local, row, size in _expert_runs(cnt_ref, off_ref, row_ref, tile):
            @pl.when(needed)
            def _():
                pltpu.make_async_copy(ys_ref.at[pl.ds(row, size)], yc_scr.at[b, pl.ds(local, size)],
                                      sems.at[b]).start()

    @pl.when(i == 0)
    def _():
        yc_scr[...] = jnp.zeros(yc_scr.shape, f32)
        fetch(0, 0)

    @pl.when(i + 1 < n)
    def _():
        fetch(i + 1, 1 - buf)

    for needed, _, size in _pieces(total_ref[i], 2 * MOE_TILE):
        @pl.when(needed)
        def _():
            pltpu.make_async_copy(ys_ref.at[pl.ds(0, size)], yc_scr.at[buf, pl.ds(0, size)],
                                  sems.at[buf]).wait()
    y = yc_scr[buf]
    y_hi = y.astype(bf16)
    y_lo = (y - y_hi.astype(f32)).astype(bf16)
    place = _iota((MOE_TILE, TILE_SLOTS), 1)
    route = route_ref[...]
    out = x_ref[...]
    for k, w_lane in ((0, ROUTE_W1), (1, ROUTE_W2)):
        pick = jnp.where(place == slots_ref[:, k:k + 1], 1.0, 0.0).astype(bf16)
        yk = (jnp.dot(pick, y_hi, preferred_element_type=f32)
              + jnp.dot(pick, y_lo, preferred_element_type=f32))
        out = out + route[:, w_lane:w_lane + 1] * yk
    o_ref[...] = out


def _combine(x2, route, slots, cnt, off, row, total, ys):
    T = x2.shape[0]
    grid_spec = pltpu.PrefetchScalarGridSpec(
        num_scalar_prefetch=4,
        grid=(T // MOE_TILE,),
        in_specs=[
            pl.BlockSpec((MOE_TILE, 2), lambda t, *_: (t, 0)),
            pl.BlockSpec((MOE_TILE, D_MODEL), lambda t, *_: (t, 0)),
            pl.BlockSpec((MOE_TILE, LANES), lambda t, *_: (t, 0)),
            pl.BlockSpec(memory_space=pl.ANY),
        ],
        out_specs=pl.BlockSpec((MOE_TILE, D_MODEL), lambda t, *_: (t, 0)),
        scratch_shapes=[pltpu.VMEM((2, TILE_SLOTS, D_MODEL), f32), pltpu.SemaphoreType.DMA((2,))],
    )
    return pl.pallas_call(
        _combine_kernel,
        grid_spec=grid_spec,
        out_shape=jax.ShapeDtypeStruct((T, D_MODEL), f32),
        compiler_params=_cparams("arbitrary"),
        name="moe_combine",
    )(cnt, off, row, total, slots, x2, route, ys)


def _moe(x2, h2, route, wg, wu, wd):
    T = x2.shape[0]
    nt = T // MOE_TILE
    e12 = route[:, ROUTE_E1:ROUTE_E2 + 1].astype(jnp.int32)
    r12 = route[:, ROUTE_R1:ROUTE_R2 + 1].astype(jnp.int32)
    chosen = (e12[:, :, None] == jnp.arange(N_EXPERTS, dtype=jnp.int32)).astype(jnp.int32)
    cnt = chosen.sum(axis=1).reshape(nt, MOE_TILE, N_EXPERTS).sum(axis=1)
    run = (cnt + RUN_ALIGN - 1) // RUN_ALIGN * RUN_ALIGN
    off = jnp.cumsum(run, axis=1) - run
    before = jnp.cumsum(run, axis=0) - run
    padded = (run.sum(axis=0) + EXPERT_ROWS - 1) // EXPERT_ROWS * EXPERT_ROWS
    pends = jnp.cumsum(padded)
    row = (pends - padded)[None, :] + before
    n_blocks = -(-(2 * T + nt * N_EXPERTS * (RUN_ALIGN - 1)) // EXPERT_ROWS) + N_EXPERTS
    block_e = jnp.minimum(
        jnp.searchsorted(pends, jnp.arange(n_blocks, dtype=jnp.int32) * EXPERT_ROWS, side='right'),
        N_EXPERTS - 1).astype(jnp.int32)
    n_valid = (pends[-1:] // EXPERT_ROWS).astype(jnp.int32)
    first_rank = jnp.cumsum(cnt, axis=0) - cnt
    shift = jnp.repeat(off - first_rank, MOE_TILE, axis=0)
    slots = r12 + jnp.take_along_axis(shift, e12, axis=1)
    slots_t = slots.reshape(nt, MOE_TILE, 2).transpose(0, 2, 1)
    flat = lambda a: a.reshape(-1).astype(jnp.int32)
    tables = (flat(run), flat(off), flat(row), flat(run.sum(axis=1)))
    xs = _dispatch(h2, slots_t, *tables, pends.astype(jnp.int32), n_blocks * EXPERT_ROWS)
    ys = _experts(block_e, n_valid, xs, wg, wu, wd)
    return _combine(x2, route, slots, *tables, ys)


def _pad_cols(w, n):
    return jnp.pad(w, ((0, 0), (0, n - w.shape[1])))


def _layer_weights(w_in, c_w_uq, c_w_ukv):
    gate_end = N_BRANCH * D_MODEL
    a_end = gate_end + 2 * WIDTH
    b_end = a_end + 3 * WIDTH
    cq = w_in[:, b_end:b_end + C_Q_RANK]
    ckv = w_in[:, b_end + C_Q_RANK:b_end + C_Q_RANK + C_KV_RANK]
    kr = w_in[:, b_end + C_Q_RANK + C_KV_RANK:b_end + C_Q_RANK + C_KV_RANK + C_ROPE_DIM]
    d_start = b_end + C_Q_RANK + C_KV_RANK + C_ROPE_DIM
    kr_block = jnp.pad(kr, ((0, 0), (C_NOPE_DIM, C_HEAD_PAD - C_QK_DIM)))
    w = jnp.concatenate([w_in[:, :b_end], _pad_cols(cq, WIDTH), ckv, kr_block, w_in[:, d_start:]],
                        axis=1).astype(bf16)
    wuq = c_w_uq.reshape(C_Q_RANK, C_HEADS, C_QK_DIM)
    wuq = jnp.pad(wuq, ((0, WIDTH - C_Q_RANK), (0, 0), (0, C_HEAD_PAD - C_QK_DIM)))
    wuq = wuq.reshape(WIDTH, C_HEADS * C_HEAD_PAD).astype(bf16)
    wukv = c_w_ukv.reshape(C_KV_RANK, C_HEADS, C_NOPE_DIM + C_V_DIM)
    wuk = jnp.pad(wukv[:, :, :C_NOPE_DIM], ((0, 0), (0, 0), (0, C_HEAD_PAD - C_NOPE_DIM)))
    wuk = wuk.reshape(C_KV_RANK, C_HEADS * C_HEAD_PAD).astype(bf16)
    wuv = wukv[:, :, C_NOPE_DIM:].reshape(C_KV_RANK, C_HEADS * C_V_DIM).astype(bf16)
    return w, wuq, wuk, wuv


def _rotation_constants():
    ch = jnp.arange(WIDTH)
    gsum_b = ((ch[:, None] // B_HEAD_DIM) == (ch[None, :] // B_HEAD_DIM)).astype(f32) / B_HEAD_DIM
    half_b = B_ROT_DIM // 2
    src, dst = ch[:, None], ch[None, :]
    same = (src // B_HEAD_DIM) == (dst // B_HEAD_DIM)
    ds_, dd = src % B_HEAD_DIM, dst % B_HEAD_DIM
    rot_b = jnp.where(same & (dd < half_b) & (ds_ == dd + half_b), -1.0,
                      jnp.where(same & (dd >= half_b) & (dd < B_ROT_DIM) & (ds_ == dd - half_b), 1.0, 0.0))
    lc = jnp.arange(C_HEAD_PAD)
    real = lc < C_QK_DIM
    gsum_c = (real[:, None] & real[None, :]).astype(f32) / C_QK_DIM
    half_c = C_ROPE_DIM // 2
    s, d = lc[:, None], lc[None, :]
    lo = (d >= C_NOPE_DIM) & (d < C_NOPE_DIM + half_c)
    hi = (d >= C_NOPE_DIM + half_c) & (d < C_QK_DIM)
    rot_c = jnp.where(lo & (s == d + half_c), -1.0, jnp.where(hi & (s == d - half_c), 1.0, 0.0))
    slot = jnp.arange(ATT_HEADS * HEAD_SLOT)
    place_b = ((slot[None, :] // HEAD_SLOT == ch[:, None] // B_HEAD_DIM)
               & (slot[None, :] % HEAD_SLOT == ch[:, None] % B_HEAD_DIM))
    return (gsum_b.astype(bf16), rot_b.astype(bf16), place_b.astype(bf16),
            gsum_c.astype(bf16), rot_c.astype(bf16))


def kernel(x, positions, attn_norm, ffn_norm, w_in, a_ln_g, a_ln_b, a_ws, a_bs, b_qn, b_kn, c_cq_norm, c_ckv_norm, c_w_uq, c_w_ukv, c_qn, c_kn, d_conv_w, d_conv_b, d_ln_g, d_ln_b, w_branch, w_out, ffn_w_gate, ffn_w_up, ffn_w_down, moe_router, moe_router_b, moe_w_gate, moe_w_up, moe_w_down):
    B, S, _ = x.shape
    T = B * S
    assert S % ATT_BLOCK == 0 and T % EXPERT_ROWS == 0
    x2 = x.reshape(T, D_MODEL)
    cos_b, sin_b, cos_c, sin_c = _rope_tables(positions)
    gsum_b, rot_b, place_b, gsum_c, rot_c = _rotation_constants()
    row = lambda v: v.reshape(1, -1)

    for l in range(DEPTH):
        w, wuq, wuk, wuv = _layer_weights(w_in[l], c_w_uq[l], c_w_ukv[l])
        proj = _inproj(x2, row(attn_norm[l]), w)

        o_a = _sgu(proj, row(a_ln_g[l]), row(a_ln_b[l]), a_ws[l], a_bs[l].T)

        qb, kb, vtb = _moba_prep(proj, B, S, cos_b, sin_b,
                                 row(jnp.tile(b_qn[l], B_HEADS)), row(jnp.tile(b_kn[l], B_HEADS)),
                                 gsum_b, rot_b, place_b)
        o_b = _attention(qb, kb, vtb, B, S, "moba_attention")

        pad_gain = lambda g, n: row(jnp.pad(g, (0, n - g.shape[0])))
        qc, kc, vtc = _mla_prep(proj, B, S, cos_c, sin_c, pad_gain(c_cq_norm[l], WIDTH),
                                row(c_ckv_norm[l]), wuq, wuk, wuv,
                                pad_gain(c_qn[l], C_HEAD_PAD), pad_gain(c_kn[l], C_HEAD_PAD),
                                gsum_c, rot_c)
        o_c = _attention(qc, kc, vtc, B, S, "mla_attention")

        o_d = _conformer(proj, B, S, d_conv_w[l], row(d_conv_b[l]), row(d_ln_g[l]), row(d_ln_b[l]))

        merge_args = (x2, proj, o_a, o_b, o_c, o_d, w_branch[l].astype(bf16),
                      w_out[l].astype(bf16), row(ffn_norm[l]))
        i = l // 2
        if l % 2 == 0:
            x2, h2 = _merge(*merge_args)
            x2 = _ffn(x2, h2, ffn_w_gate[i].astype(bf16), ffn_w_up[i].astype(bf16),
                      ffn_w_down[i].astype(bf16))
        else:
            router = (_pad_cols(moe_router[i], LANES), _pad_cols(row(moe_router_b[i]), LANES))
            x2, h2, route = _merge(*merge_args, router=router)
            x2 = _moe(x2, h2, route, moe_w_gate[i].astype(bf16),
                      moe_w_up[i].astype(bf16), moe_w_down[i].astype(bf16))
    return x2.reshape(B, S, D_MODEL)
```

```python
import functools

import jax
import jax.numpy as jnp
from jax import lax
from jax.experimental import pallas as pl
from jax.experimental.pallas import tpu as pltpu

D_MODEL = 1024
DEPTH = 4
N_BRANCH = 4
ROPE_THETA = 500000.0
NORM_EPS = 1e-6
NEG_INF = -1e30

WIDTH = 256
A_GROUPS = 4
A_GROUP_DIM = 64
A_CHUNK = 128
B_HEADS = 4
B_HEAD_DIM = 64
B_ROT_DIM = 16
MOBA_BLOCK = 256
MOBA_TOPK = 3
C_HEADS = 4
C_Q_RANK = 192
C_KV_RANK = 128
C_NOPE_DIM = 64
C_ROPE_DIM = 32
C_V_DIM = 64
C_QK_DIM = C_NOPE_DIM + C_ROPE_DIM
C_HEAD_PAD = 128
D_CONV = 31
FFN_DIM = 2816
N_EXPERTS = 8
EXPERT_DIM = 1408
EXPERT_ROWS = 512

ATT_BLOCK = 256
ATT_QBLOCK = 512
ATT_HEADS = 4
HEAD_SLOT = 128
V_AUG = C_V_DIM + 16
LOG2E = 1.4426950408889634
SM_ROWS = 32
LANES = 128
SUBLANES = 8
CONV_HALO = 32
CONV_ROWS = 64
LN_ROWS = 128

COL_GATE = 0
COL_AU = 4096
COL_AV = 4352
COL_BQ = 4608
COL_BK = 4864
COL_BV = 5120
COL_CQ = 5376
COL_CKV = 5632
COL_KR = 5760
COL_DA = 5888
COL_DG = 6144
PROJ_COLS = 6400

VMEM_LIMIT = 56 * 1024 * 1024

bf16 = jnp.bfloat16
f32 = jnp.float32


def _cparams(*sem):
    return pltpu.CompilerParams(dimension_semantics=sem, vmem_limit_bytes=VMEM_LIMIT)


def _rms(x, n):
    return x * lax.rsqrt(jnp.sum(x * x, axis=-1, keepdims=True) * (1.0 / n) + NORM_EPS)


def _layernorm(x, g, b):
    mu = jnp.mean(x, axis=-1, keepdims=True)
    xc = x - mu
    return xc * lax.rsqrt(jnp.mean(xc * xc, axis=-1, keepdims=True) + NORM_EPS) * g + b


def _sigmoid(x):
    return 0.5 * jnp.tanh(0.5 * x) + 0.5


def _iota(shape, dim):
    return lax.broadcasted_iota(jnp.int32, shape, dim)


def _rope_table_kernel(pos_ref, inv_ref, eb_ref, ec_ref, cb_ref, sb_ref, cc_ref, sc_ref):
    ang = pos_ref[0].astype(f32) * inv_ref[...]
    c = jnp.cos(ang)
    s = jnp.sin(ang)
    hi = lax.Precision.HIGHEST
    cb_ref[0] = jnp.dot(c, eb_ref[...], precision=hi, preferred_element_type=f32)
    sb_ref[0] = jnp.dot(s, eb_ref[...], precision=hi, preferred_element_type=f32)
    cc_ref[0] = jnp.dot(c, ec_ref[...], precision=hi, preferred_element_type=f32)
    sc_ref[0] = jnp.dot(s, ec_ref[...], precision=hi, preferred_element_type=f32)


def _rope_tables(positions):
    B, S = positions.shape
    R = min(512, S)
    nb = B_ROT_DIM // 2
    nc = C_ROPE_DIM // 2
    inv_b = f32(ROPE_THETA) ** (-jnp.arange(0, B_ROT_DIM, 2, dtype=f32) / B_ROT_DIM)
    inv_c = f32(ROPE_THETA) ** (-jnp.arange(0, C_ROPE_DIM, 2, dtype=f32) / C_ROPE_DIM)
    inv = jnp.zeros((1, LANES), f32).at[0, :nb].set(inv_b).at[0, nb:nb + nc].set(inv_c)
    zero_lane = LANES - 1
    ch = jnp.arange(WIDTH)
    d = ch % B_HEAD_DIM
    src_b = jnp.where(d < B_ROT_DIM, d % nb, zero_lane)
    eb = (jnp.arange(LANES)[:, None] == src_b[None, :]).astype(f32)
    dc = jnp.arange(C_HEAD_PAD)
    in_rope = (dc >= C_NOPE_DIM) & (dc < C_QK_DIM)
    src_c = jnp.where(in_rope, nb + (dc - C_NOPE_DIM) % nc, zero_lane)
    ec = (jnp.arange(LANES)[:, None] == src_c[None, :]).astype(f32)
    pos3 = positions.reshape(B, S, 1)
    out = pl.pallas_call(
        _rope_table_kernel,
        grid=(B, S // R),
        in_specs=[
            pl.BlockSpec((1, R, 1), lambda b, r: (b, r, 0)),
            pl.BlockSpec((1, LANES), lambda b, r: (0, 0)),
            pl.BlockSpec((LANES, WIDTH), lambda b, r: (0, 0)),
            pl.BlockSpec((LANES, C_HEAD_PAD), lambda b, r: (0, 0)),
        ],
        out_specs=[
            pl.BlockSpec((1, R, WIDTH), lambda b, r: (b, r, 0)),
            pl.BlockSpec((1, R, WIDTH), lambda b, r: (b, r, 0)),
            pl.BlockSpec((1, R, C_HEAD_PAD), lambda b, r: (b, r, 0)),
            pl.BlockSpec((1, R, C_HEAD_PAD), lambda b, r: (b, r, 0)),
        ],
        out_shape=[
            jax.ShapeDtypeStruct((B, S, WIDTH), f32),
            jax.ShapeDtypeStruct((B, S, WIDTH), f32),
            jax.ShapeDtypeStruct((B, S, C_HEAD_PAD), f32),
            jax.ShapeDtypeStruct((B, S, C_HEAD_PAD), f32),
        ],
        compiler_params=_cparams("parallel", "parallel"),
        name="rope_tables",
    )(pos3, inv, eb, ec)
    return out


PROJ_CHUNKS = tuple((lo, lo + D_MODEL) for lo in range(0, COL_AU, D_MODEL)) + (
    (COL_AU, COL_CQ), (COL_CQ, PROJ_COLS))


def _inproj_kernel(x_ref, g_ref, w_ref, o_ref):
    h = (_rms(x_ref[...], D_MODEL) * g_ref[...]).astype(bf16)
    for lo, hi in PROJ_CHUNKS:
        y = jnp.dot(h, w_ref[:, lo:hi], preferred_element_type=f32)
        if hi <= COL_AU:
            y = _sigmoid(y)
        o_ref[:, lo:hi] = y.astype(bf16)


def _inproj(x2, g, w):
    T = x2.shape[0]
    tm = min(512, T)
    return pl.pallas_call(
        _inproj_kernel,
        grid=(T // tm,),
        in_specs=[
            pl.BlockSpec((tm, D_MODEL), lambda t: (t, 0)),
            pl.BlockSpec((1, D_MODEL), lambda t: (0, 0)),
            pl.BlockSpec((D_MODEL, PROJ_COLS), lambda t: (0, 0)),
        ],
        out_specs=pl.BlockSpec((tm, PROJ_COLS), lambda t: (t, 0)),
        out_shape=jax.ShapeDtypeStruct((T, PROJ_COLS), bf16),
        compiler_params=_cparams("parallel"),
        name="inproj",
    )(x2, g, w)


def _sgu_kernel(u_ref, v_ref, g_ref, b_ref, ws_ref, bst_ref, o_ref, *, chunks):
    row = _iota((A_CHUNK, A_CHUNK), 0)
    col = _iota((A_CHUNK, A_CHUNK), 1)
    causal = col <= row
    group = _iota((A_CHUNK, WIDTH), 1) // A_GROUP_DIM
    wm = [jnp.where(causal, ws_ref[g], 0.0).astype(bf16) for g in range(A_GROUPS)]
    bias = jnp.zeros((A_CHUNK, WIDTH), f32)
    for g in range(A_GROUPS):
        bias = jnp.where(group == g, bst_ref[:, g:g + 1], bias)
    for c in range(chunks):
        rows = slice(c * A_CHUNK, (c + 1) * A_CHUNK)
        vn = _layernorm(v_ref[rows, :].astype(f32), g_ref[...], b_ref[...]).astype(bf16)
        z = bias
        for g in range(A_GROUPS):
            zg = jnp.dot(wm[g], vn, preferred_element_type=f32)
            z = z + jnp.where(group == g, zg, 0.0)
        o_ref[rows, :] = (u_ref[rows, :].astype(f32) * z).astype(bf16)


def _sgu(proj, ln_g, ln_b, ws, bs_t):
    T = proj.shape[0]
    ta = min(512, T)
    cu, cv = COL_AU // WIDTH, COL_AV // WIDTH
    return pl.pallas_call(
        functools.partial(_sgu_kernel, chunks=ta // A_CHUNK),
        grid=(T // ta,),
        in_specs=[
            pl.BlockSpec((ta, WIDTH), lambda t: (t, cu)),
            pl.BlockSpec((ta, WIDTH), lambda t: (t, cv)),
            pl.BlockSpec((1, WIDTH), lambda t: (0, 0)),
            pl.BlockSpec((1, WIDTH), lambda t: (0, 0)),
            pl.BlockSpec((A_GROUPS, A_CHUNK, A_CHUNK), lambda t: (0, 0, 0)),
            pl.BlockSpec((A_CHUNK, A_GROUPS), lambda t: (0, 0)),
        ],
        out_specs=pl.BlockSpec((ta, WIDTH), lambda t: (t, 0)),
        out_shape=jax.ShapeDtypeStruct((T, WIDTH), bf16),
        compiler_params=_cparams("parallel"),
        name="sgu",
    )(proj, proj, ln_g, ln_b, ws, bs_t)


def _conv_kernel(a_ref, g_ref, w_ref, cb_ref, lg_ref, lb_ref, o_ref, z_scr, *, seq):
    first = CONV_HALO - (D_CONV - 1)
    z_scr[0, 0:CONV_HALO, :] = jnp.zeros((CONV_HALO, WIDTH), f32)
    z_scr[0, CONV_HALO + seq:, :] = jnp.zeros((CONV_ROWS, WIDTH), f32)

    def glu(c, carry):
        base = pl.multiple_of(c * LN_ROWS, LN_ROWS)
        rows = pl.ds(base, LN_ROWS)
        z_scr[0, pl.ds(base + CONV_HALO, LN_ROWS), :] = (
            a_ref[rows, :].astype(f32) * _sigmoid(g_ref[rows, :].astype(f32)))
        return carry

    lax.fori_loop(0, seq // LN_ROWS, glu, 0)

    def shift(c, carry):
        base = pl.multiple_of(c * CONV_ROWS, CONV_ROWS)
        win = z_scr[0, pl.ds(base, CONV_ROWS + SUBLANES), :]
        for r in range(1, SUBLANES):
            rolled = pltpu.roll(win, shift=CONV_ROWS + SUBLANES - r, axis=0)
            z_scr[r, pl.ds(base, CONV_ROWS), :] = rolled[:CONV_ROWS]
        return carry

    lax.fori_loop(0, seq // CONV_ROWS + 1, shift, 0)

    def taps(c, carry):
        base = pl.multiple_of(c * CONV_ROWS, CONV_ROWS)
        acc = jnp.zeros((CONV_ROWS, WIDTH), f32) + cb_ref[...]
        for k in range(D_CONV):
            off = first + k
            lo = pl.multiple_of(base + off // SUBLANES * SUBLANES, SUBLANES)
            acc = acc + w_ref[k:k + 1, :] * z_scr[off % SUBLANES, pl.ds(lo, CONV_ROWS), :]
        z_scr[1, pl.ds(base, CONV_ROWS), :] = acc
        return carry

    lax.fori_loop(0, seq // CONV_ROWS, taps, 0)

    def norm(c, carry):
        base = pl.multiple_of(c * LN_ROWS, LN_ROWS)
        y = _layernorm(z_scr[1, pl.ds(base, LN_ROWS), :], lg_ref[...], lb_ref[...])
        o_ref[pl.ds(base, LN_ROWS), :] = (y * _sigmoid(y)).astype(bf16)
        return carry

    lax.fori_loop(0, seq // LN_ROWS, norm, 0, unroll=2)


def _conformer(proj, B, S, conv_w, conv_b, ln_g, ln_b):
    ca, cg = COL_DA // WIDTH, COL_DG // WIDTH
    return pl.pallas_call(
        functools.partial(_conv_kernel, seq=S),
        grid=(B,),
        in_specs=[
            pl.BlockSpec((S, WIDTH), lambda b: (b, ca)),
            pl.BlockSpec((S, WIDTH), lambda b: (b, cg)),
            pl.BlockSpec((D_CONV, WIDTH), lambda b: (0, 0)),
            pl.BlockSpec((1, WIDTH), lambda b: (0, 0)),
            pl.BlockSpec((1, WIDTH), lambda b: (0, 0)),
            pl.BlockSpec((1, WIDTH), lambda b: (0, 0)),
        ],
        out_specs=pl.BlockSpec((S, WIDTH), lambda b: (b, 0)),
        out_shape=jax.ShapeDtypeStruct((B * S, WIDTH), bf16),
        scratch_shapes=[pltpu.VMEM((SUBLANES, CONV_HALO + S + CONV_ROWS, WIDTH), f32)],
        compiler_params=_cparams("parallel"),
        name="conformer_conv",
    )(proj, proj, conv_w, conv_b, ln_g, ln_b)


def _value_rows(v_tok):
    vt = v_tok.T
    ones = jnp.ones((V_AUG - C_V_DIM, vt.shape[1]), f32)
    rows = []
    for h in range(ATT_HEADS):
        rows += [vt[h * C_V_DIM:(h + 1) * C_V_DIM], ones]
    return jnp.concatenate(rows, axis=0).astype(bf16)


def _moba_prep_kernel(q_ref, k_ref, v_ref, cos_ref, sin_ref, qg_ref, kg_ref, gsum_ref, rot_ref,
                      place_ref, qo_ref, ko_ref, vt_ref, kmean_scr, *, nblk):
    c = pl.program_id(1)

    @pl.when(c == 0)
    def _():
        kmean_scr[...] = jnp.zeros(kmean_scr.shape, f32)

    cos = cos_ref[0]
    sin = sin_ref[0]

    def norm_rope(x, gain):
        ss = jnp.dot((x * x).astype(bf16), gsum_ref[...], preferred_element_type=f32)
        xn = x * lax.rsqrt(ss + NORM_EPS) * gain
        xr = jnp.dot(xn.astype(bf16), rot_ref[...], preferred_element_type=f32)
        return xn * cos + xr * sin

    q = norm_rope(q_ref[...].astype(f32), qg_ref[...])
    k = norm_rope(k_ref[...].astype(f32), kg_ref[...])
    slot_lane = _iota((ATT_BLOCK, ATT_HEADS * HEAD_SLOT), 1) % HEAD_SLOT
    k_slots = jnp.dot(k.astype(bf16), place_ref[...], preferred_element_type=f32)
    ko_ref[0, 0] = jnp.where(slot_lane == B_HEAD_DIM + c, 1.0, k_slots).astype(bf16)
    vt_ref[0, 0] = _value_rows(v_ref[...].astype(f32))

    head = _iota((nblk, WIDTH), 1) // B_HEAD_DIM
    blk = _iota((nblk, ATT_BLOCK), 0)
    past = blk < c
    kmean = kmean_scr[...]
    zeros_lo = jnp.zeros((B_HEAD_DIM, ATT_BLOCK), f32)
    zeros_hi = jnp.zeros((HEAD_SLOT - B_HEAD_DIM - nblk, ATT_BLOCK), f32)
    bias_rows = []
    for h in range(B_HEADS):
        km = jnp.where(head == h, kmean, 0.0)
        sc = lax.dot_general(km, q, (((1,), (1,)), ((), ())), precision=lax.Precision.HIGHEST,
                             preferred_element_type=f32)
        sc = jnp.where(past, sc, NEG_INF)
        rank = jnp.zeros((nblk, ATT_BLOCK), jnp.int32)
        for j in range(nblk):
            other = sc[j:j + 1, :]
            ahead = (other > sc) | ((other == sc) & (j < blk))
            rank = rank + ahead.astype(jnp.int32)
        visible = (past & (rank < MOBA_TOPK)) | (blk == c)
        bias_rows += [zeros_lo, jnp.where(visible, 0.0, NEG_INF), zeros_hi]
    bias = jnp.concatenate(bias_rows, axis=0)
    q_slots = jnp.dot((q * (B_HEAD_DIM ** -0.5 * LOG2E)).astype(bf16), place_ref[...],
                      preferred_element_type=f32)
    qo_ref[0] = (q_slots.T + bias).astype(bf16)

    kmean_scr[pl.ds(c, 1), :] = jnp.mean(k, axis=0, keepdims=True)


def _moba_prep(proj, B, S, cos_b, sin_b, qg, kg, gsum, rot, place):
    nblk = S // ATT_BLOCK
    assert nblk <= HEAD_SLOT - B_HEAD_DIM
    HD = ATT_HEADS * HEAD_SLOT
    cq, ck, cv = COL_BQ // WIDTH, COL_BK // WIDTH, COL_BV // WIDTH
    tile = lambda col: pl.BlockSpec((ATT_BLOCK, WIDTH), lambda b, c: (b * nblk + c, col))
    table = pl.BlockSpec((1, ATT_BLOCK, WIDTH), lambda b, c: (b, c, 0))
    const = lambda shape: pl.BlockSpec(shape, lambda b, c: (0,) * len(shape))
    return pl.pallas_call(
        functools.partial(_moba_prep_kernel, nblk=nblk),
        grid=(B, nblk),
        in_specs=[tile(cq), tile(ck), tile(cv), table, table,
                  const((1, WIDTH)), const((1, WIDTH)), const((WIDTH, WIDTH)), const((WIDTH, WIDTH)),
                  const((WIDTH, HD))],
        out_specs=[
            pl.BlockSpec((1, HD, ATT_BLOCK), lambda b, c: (b, 0, c)),
            pl.BlockSpec((1, 1, ATT_BLOCK, HD), lambda b, c: (b, c, 0, 0)),
            pl.BlockSpec((1, 1, ATT_HEADS * V_AUG, ATT_BLOCK), lambda b, c: (b, c, 0, 0)),
        ],
        out_shape=[
            jax.ShapeDtypeStruct((B, HD, S), bf16),
            jax.ShapeDtypeStruct((B, nblk, ATT_BLOCK, HD), bf16),
            jax.ShapeDtypeStruct((B, nblk, ATT_HEADS * V_AUG, ATT_BLOCK), bf16),
        ],
        scratch_shapes=[pltpu.VMEM((nblk, WIDTH), f32)],
        compiler_params=_cparams("parallel", "arbitrary"),
        name="moba_prep",
    )(proj, proj, proj, cos_b, sin_b, qg, kg, gsum, rot, place)


def _mla_prep_kernel(cq_ref, ckv_ref, kr_ref, cos_ref, sin_ref, cqg_ref, ckvg_ref, wuq_ref, wuk_ref,
                     wuv_ref, qg_ref, kg_ref, gsum_ref, rot_ref, qo_ref, ko_ref, vt_ref):
    cqn = (_rms(cq_ref[...].astype(f32), C_Q_RANK) * cqg_ref[...]).astype(bf16)
    ckvn = (_rms(ckv_ref[...].astype(f32), C_KV_RANK) * ckvg_ref[...]).astype(bf16)
    q = jnp.dot(cqn, wuq_ref[...], preferred_element_type=f32)
    k = jnp.dot(ckvn, wuk_ref[...], preferred_element_type=f32)
    v = jnp.dot(ckvn, wuv_ref[...], preferred_element_type=f32)
    kr = kr_ref[...].astype(f32)
    cos = cos_ref[0]
    sin = sin_ref[0]

    def norm_rope(x, gain):
        ss = jnp.dot((x * x).astype(bf16), gsum_ref[...], preferred_element_type=f32)
        xn = x * lax.rsqrt(ss + NORM_EPS) * gain
        xr = jnp.dot(xn.astype(bf16), rot_ref[...], preferred_element_type=f32)
        return xn * cos + xr * sin

    for h in range(C_HEADS):
        sl = slice(h * C_HEAD_PAD, (h + 1) * C_HEAD_PAD)
        qh = norm_rope(q[:, sl], qg_ref[...])
        kh = norm_rope(k[:, sl] + kr, kg_ref[...])
        qo_ref[0, sl, :] = (qh * (C_QK_DIM ** -0.5 * LOG2E)).T.astype(bf16)
        ko_ref[0, 0, :, sl] = kh.astype(bf16)
    vt_ref[0, 0] = _value_rows(v)


def _mla_prep(proj, B, S, cos_c, sin_c, cqg, ckvg, wuq, wuk, wuv, qg, kg, gsum, rot):
    nblk = S // ATT_BLOCK
    HD = C_HEADS * C_HEAD_PAD
    table = pl.BlockSpec((1, ATT_BLOCK, C_HEAD_PAD), lambda b, c: (b, c, 0))
    const = lambda shape: pl.BlockSpec(shape, lambda b, c: (0,) * len(shape))
    return pl.pallas_call(
        _mla_prep_kernel,
        grid=(B, nblk),
        in_specs=[
            pl.BlockSpec((ATT_BLOCK, WIDTH), lambda b, c: (b * nblk + c, COL_CQ // WIDTH)),
            pl.BlockSpec((ATT_BLOCK, LANES), lambda b, c: (b * nblk + c, COL_CKV // LANES)),
            pl.BlockSpec((ATT_BLOCK, LANES), lambda b, c: (b * nblk + c, COL_KR // LANES)),
            table, table,
            const((1, WIDTH)), const((1, C_KV_RANK)), const((WIDTH, HD)), const((C_KV_RANK, HD)),
            const((C_KV_RANK, WIDTH)), const((1, C_HEAD_PAD)), const((1, C_HEAD_PAD)),
            const((C_HEAD_PAD, C_HEAD_PAD)), const((C_HEAD_PAD, C_HEAD_PAD)),
        ],
        out_specs=[
            pl.BlockSpec((1, HD, ATT_BLOCK), lambda b, c: (b, 0, c)),
            pl.BlockSpec((1, 1, ATT_BLOCK, HD), lambda b, c: (b, c, 0, 0)),
            pl.BlockSpec((1, 1, ATT_HEADS * V_AUG, ATT_BLOCK), lambda b, c: (b, c, 0, 0)),
        ],
        out_shape=[
            jax.ShapeDtypeStruct((B, HD, S), bf16),
            jax.ShapeDtypeStruct((B, nblk, ATT_BLOCK, HD), bf16),
            jax.ShapeDtypeStruct((B, nblk, ATT_HEADS * V_AUG, ATT_BLOCK), bf16),
        ],
        compiler_params=_cparams("parallel", "parallel"),
        name="mla_prep",
    )(proj, proj, proj, cos_c, sin_c, cqg, ckvg, wuq, wuk, wuv, qg, kg, gsum, rot)


def _attn_kernel(q_ref, k_ref, vt_ref, o_ref, *scratch):
    H = ATT_HEADS
    s_scr, p_scr, a_scr = ([scratch[(2 * n + par) * H:(2 * n + par + 1) * H] for par in range(2)]
                           for n in range(3))
    acc_scr, m_scr = (scratch[n * H:(n + 1) * H] for n in (6, 7))
    out_scr = scratch[8 * H]
    top_scr = [scratch[8 * H + 1 + par * H:8 * H + 1 + (par + 1) * H] for par in range(2)]
    pairs = pl.program_id(1) * (ATT_QBLOCK // ATT_BLOCK // 2)
    chunks = ATT_BLOCK // SM_ROWS

    def scores(j, par, h):
        slot = slice(h * HEAD_SLOT, (h + 1) * HEAD_SLOT)
        s = jnp.dot(k_ref[0, j, :, slot], q_ref[0, slot, :],
                    preferred_element_type=f32)
        s_scr[par][h][...] = s
        top_scr[par][h][...] = jnp.max(s, axis=0, keepdims=True)

    def softmax(par, h, diag_offset=None):
        def rows(r):
            s = s_scr[par][h][r * SM_ROWS:(r + 1) * SM_ROWS, :]
            if diag_offset is not None:
                kpos = _iota((SM_ROWS, ATT_QBLOCK), 0) + (r * SM_ROWS + diag_offset)
                s = jnp.where(kpos <= _iota((SM_ROWS, ATT_QBLOCK), 1), s, NEG_INF)
            return s
        if diag_offset is None:
            top = top_scr[par][h][...]
        else:
            mx = rows(0)
            for r in range(1, chunks):
                mx = jnp.maximum(mx, rows(r))
            top = jnp.max(mx, axis=0, keepdims=True)
        m_old = m_scr[h][...]
        m_new = jnp.maximum(m_old, top)
        a_scr[par][h][...] = jnp.exp2(m_old - m_new)
        m_scr[h][...] = m_new
        for r in range(chunks):
            p_scr[par][h][r * SM_ROWS:(r + 1) * SM_ROWS, :] = (
                jnp.exp2((rows(r) - m_new).astype(bf16)))

    def values(j, par, h):
        pv = jnp.dot(vt_ref[0, j, h * V_AUG:(h + 1) * V_AUG, :], p_scr[par][h][...],
                     preferred_element_type=f32)
        acc_scr[h][...] = a_scr[par][h][...] * acc_scr[h][...] + pv

    for h in range(H):
        m_scr[h][...] = jnp.full(m_scr[h].shape, NEG_INF, f32)
        a_scr[1][h][...] = jnp.zeros(a_scr[1][h].shape, f32)
        acc_scr[h][...] = jnp.zeros(acc_scr[h].shape, f32)
        p_scr[1][h][...] = jnp.zeros(p_scr[1][h].shape, bf16)
        scores(0, 0, h)

    def step(b, par, diag_offset=None, last=False):
        for h in range(H):
            if not last:
                scores(b + 1, 1 - par, h)
            values(jnp.maximum(b - 1, 0), 1 - par, h)
            softmax(par, h, diag_offset)

    def past_pair(t, carry):
        step(2 * t, 0)
        step(2 * t + 1, 1)
        return carry

    lax.fori_loop(0, pairs, past_pair, 0)
    step(2 * pairs, 0, diag_offset=0)
    step(2 * pairs + 1, 1, diag_offset=ATT_BLOCK, last=True)

    for h in range(H):
        values(2 * pairs + 1, 1, h)
        a = acc_scr[h][...]
        out_scr[h * C_V_DIM:(h + 1) * C_V_DIM, :] = a[:C_V_DIM] / a[C_V_DIM:C_V_DIM + 1]
    o_ref[...] = out_scr[...].T.astype(bf16)


def _attention(q, k, vt, B, S, name):
    nblk = S // ATT_BLOCK
    nq = S // ATT_QBLOCK
    HD = ATT_HEADS * HEAD_SLOT
    return pl.pallas_call(
        _attn_kernel,
        grid=(B, nq),
        in_specs=[
            pl.BlockSpec((1, HD, ATT_QBLOCK), lambda b, i: (b, 0, i)),
            pl.BlockSpec((1, nblk, ATT_BLOCK, HD), lambda b, i: (b, 0, 0, 0)),
            pl.BlockSpec((1, nblk, ATT_HEADS * V_AUG, ATT_BLOCK), lambda b, i: (b, 0, 0, 0)),
        ],
        out_specs=pl.BlockSpec((ATT_QBLOCK, WIDTH), lambda b, i: (b * nq + i, 0)),
        out_shape=jax.ShapeDtypeStruct((B * S, WIDTH), bf16),
        scratch_shapes=(
            [pltpu.VMEM((ATT_BLOCK, ATT_QBLOCK), f32) for _ in range(2 * ATT_HEADS)]
            + [pltpu.VMEM((ATT_BLOCK, ATT_QBLOCK), bf16) for _ in range(2 * ATT_HEADS)]
            + [pltpu.VMEM((1, ATT_QBLOCK), f32) for _ in range(2 * ATT_HEADS)]
            + [pltpu.VMEM((V_AUG, ATT_QBLOCK), f32) for _ in range(ATT_HEADS)]
            + [pltpu.VMEM((1, ATT_QBLOCK), f32) for _ in range(ATT_HEADS)]
            + [pltpu.VMEM((WIDTH, ATT_QBLOCK), f32)]
            + [pltpu.VMEM((1, ATT_QBLOCK), f32) for _ in range(2 * ATT_HEADS)]),
        compiler_params=_cparams("parallel", "arbitrary"),
        name=name,
    )(q, k, vt)


ROUTE_E1, ROUTE_E2, ROUTE_W1, ROUTE_W2, ROUTE_R1, ROUTE_R2 = range(6)


def _route_rows(h, wr_ref, br_ref, route_ref, carry_scr):
    tm = h.shape[0]

    @pl.when(pl.program_id(0) == 0)
    def _():
        carry_scr[...] = jnp.zeros(carry_scr.shape, f32)

    lane = _iota((tm, LANES), 1)
    h_hi = h.astype(bf16)
    h_lo = (h - h_hi.astype(f32)).astype(bf16)
    w = wr_ref[...]
    w_hi = w.astype(bf16)
    w_lo = (w - w_hi.astype(f32)).astype(bf16)
    logits = (jnp.dot(h_hi, w_hi, preferred_element_type=f32)
              + (jnp.dot(h_hi, w_lo, preferred_element_type=f32)
                 + jnp.dot(h_lo, w_hi, preferred_element_type=f32))) + br_ref[...]
    logits = jnp.where(lane < N_EXPERTS, logits, NEG_INF)
    m1 = jnp.max(logits, axis=-1, keepdims=True)
    e1 = jnp.min(jnp.where(logits == m1, lane, LANES), axis=-1, keepdims=True)
    rest = jnp.where(lane == e1, NEG_INF, logits)
    m2 = jnp.max(rest, axis=-1, keepdims=True)
    e2 = jnp.min(jnp.where(rest == m2, lane, LANES), axis=-1, keepdims=True)
    ex = jnp.exp(m2 - m1)
    w1 = 1.0 / (1.0 + ex)
    w2 = ex / (1.0 + ex)
    chosen = ((lane == e1) | (lane == e2)).astype(f32)
    earlier = (_iota((tm, tm), 1) < _iota((tm, tm), 0)).astype(bf16)
    before = jnp.dot(earlier, chosen.astype(bf16), preferred_element_type=f32) + carry_scr[...]
    r1 = jnp.sum(jnp.where(lane == e1, before, 0.0), axis=-1, keepdims=True)
    r2 = jnp.sum(jnp.where(lane == e2, before, 0.0), axis=-1, keepdims=True)
    out = jnp.zeros((tm, LANES), f32)
    for idx, val in ((ROUTE_E1, e1.astype(f32)), (ROUTE_E2, e2.astype(f32)), (ROUTE_W1, w1),
                     (ROUTE_W2, w2), (ROUTE_R1, r1), (ROUTE_R2, r2)):
        out = jnp.where(lane == idx, val, out)
    route_ref[...] = out
    carry_scr[...] += jnp.sum(chosen, axis=0, keepdims=True)


def _merge_kernel(x_ref, g0_ref, g1_ref, g2_ref, g3_ref, oa_ref, ob_ref, oc_ref, od_ref,
                  wb_ref, wo_ref, ng_ref, *rest, routed):
    y = None
    for n, (g_ref, o_ref) in enumerate(((g0_ref, oa_ref), (g1_ref, ob_ref),
                                        (g2_ref, oc_ref), (g3_ref, od_ref))):
        br = jnp.dot(o_ref[...], wb_ref[n], preferred_element_type=f32)
        t = g_ref[...].astype(f32) * br
        y = t if y is None else y + t
    x = x_ref[...] + jnp.dot(y.astype(bf16), wo_ref[...], preferred_element_type=f32)
    h = _rms(x, D_MODEL) * ng_ref[...]
    if routed:
        wr_ref, br_ref, xo_ref, ho_ref, route_ref, carry_scr = rest
        ho_ref[...] = h
        _route_rows(h, wr_ref, br_ref, route_ref, carry_scr)
    else:
        xo_ref, ho_ref = rest
        ho_ref[...] = h.astype(bf16)
    xo_ref[...] = x


def _merge(x2, proj, oa, ob, oc, od, wb, wo, ng, router=None):
    T = x2.shape[0]
    tm = min(512, T)
    routed = router is not None
    gate = lambda n: pl.BlockSpec((tm, D_MODEL), lambda t: (t, n))
    branch = pl.BlockSpec((tm, WIDTH), lambda t: (t, 0))
    rows = pl.BlockSpec((tm, D_MODEL), lambda t: (t, 0))
    in_specs = [
        rows, gate(0), gate(1), gate(2), gate(3), branch, branch, branch, branch,
        pl.BlockSpec((N_BRANCH, WIDTH, D_MODEL), lambda t: (0, 0, 0)),
        pl.BlockSpec((D_MODEL, D_MODEL), lambda t: (0, 0)),
        pl.BlockSpec((1, D_MODEL), lambda t: (0, 0)),
    ]
    args = [x2, proj, proj, proj, proj, oa, ob, oc, od, wb, wo, ng]
    out_specs = [rows, rows]
    out_shape = [jax.ShapeDtypeStruct((T, D_MODEL), f32),
                 jax.ShapeDtypeStruct((T, D_MODEL), f32 if routed else bf16)]
    scratch = []
    if routed:
        in_specs += [pl.BlockSpec((D_MODEL, LANES), lambda t: (0, 0)),
                     pl.BlockSpec((1, LANES), lambda t: (0, 0))]
        args += list(router)
        out_specs += [pl.BlockSpec((tm, LANES), lambda t: (t, 0))]
        out_shape += [jax.ShapeDtypeStruct((T, LANES), f32)]
        scratch = [pltpu.VMEM((1, LANES), f32)]
    return pl.pallas_call(
        functools.partial(_merge_kernel, routed=routed),
        grid=(T // tm,),
        in_specs=in_specs,
        out_specs=out_specs,
        out_shape=out_shape,
        scratch_shapes=scratch,
        compiler_params=_cparams("arbitrary" if routed else "parallel"),
        name="merge_route" if routed else "merge",
    )(*args)


def _swiglu_block(h, wg, wu, wd):
    a = jnp.dot(h, wg, preferred_element_type=f32)
    b = jnp.dot(h, wu, preferred_element_type=f32)
    t = (a * _sigmoid(a) * b).astype(bf16)
    return jnp.dot(t, wd, preferred_element_type=f32)


FFN_CHUNKS = ((0, 1536), (1536, FFN_DIM))


def _ffn_kernel(x_ref, h_ref, wg_ref, wu_ref, wd_ref, o_ref):
    h = h_ref[...]
    acc = x_ref[...]
    for lo, hi in FFN_CHUNKS:
        acc = acc + _swiglu_block(h, wg_ref[:, lo:hi], wu_ref[:, lo:hi], wd_ref[lo:hi, :])
    o_ref[...] = acc


def _ffn(x2, h2, wg, wu, wd):
    T = x2.shape[0]
    tm = min(512, T)
    rows = pl.BlockSpec((tm, D_MODEL), lambda t: (t, 0))
    resident = lambda shape: pl.BlockSpec(shape, lambda t: (0, 0), pipeline_mode=pl.Buffered(1))
    return pl.pallas_call(
        _ffn_kernel,
        grid=(T // tm,),
        in_specs=[rows, rows, resident((D_MODEL, FFN_DIM)), resident((D_MODEL, FFN_DIM)),
                  resident((FFN_DIM, D_MODEL))],
        out_specs=rows,
        out_shape=jax.ShapeDtypeStruct((T, D_MODEL), f32),
        compiler_params=_cparams("parallel"),
        name="dense_swiglu",
    )(x2, h2, wg, wu, wd)


MOE_TILE = 256
RUN_ALIGN = SUBLANES
TILE_SLOTS = 2 * MOE_TILE + N_EXPERTS * RUN_ALIGN


def _pow2_pieces(limit):
    out, b = [], RUN_ALIGN
    while b <= limit:
        out.append(b)
        b *= 2
    return tuple(reversed(out))


def _pieces(value, limit):
    sizes = _pow2_pieces(limit)
    for b in sizes:
        done = value & (~(2 * b - 1) & (2 * sizes[0] - 1))
        yield (value & b) != 0, done, b


def _expert_runs(cnt_ref, off_ref, row_ref, tile):
    for e in range(N_EXPERTS):
        o = off_ref[tile * N_EXPERTS + e]
        d = row_ref[tile * N_EXPERTS + e]
        for needed, done, size in _pieces(cnt_ref[tile * N_EXPERTS + e], MOE_TILE):
            yield (needed, pl.multiple_of(o + done, RUN_ALIGN), pl.multiple_of(d + done, RUN_ALIGN),
                   size)


def _dispatch_kernel(cnt_ref, off_ref, row_ref, total_ref, pends_ref, slots_ref, h_ref, xs_ref,
                     xc_scr, zero_scr, sems, zero_sem, *, spare_blocks):
    i = pl.program_id(0)
    n = pl.num_programs(0)
    buf = i % 2

    def zero_fill(start):
        start = pl.multiple_of(start, EXPERT_ROWS)
        return pltpu.make_async_copy(zero_scr, xs_ref.at[pl.ds(start, EXPERT_ROWS)], zero_sem)

    @pl.when(i == 0)
    def _():
        zero_scr[...] = jnp.zeros(zero_scr.shape, f32)
        used = pends_ref[N_EXPERTS - 1]
        fills = [(jnp.maximum(pends_ref[e] - EXPERT_ROWS, 0), None) for e in range(N_EXPERTS)]
        fills += [(used + e * EXPERT_ROWS, used + e * EXPERT_ROWS < xs_ref.shape[0])
                  for e in range(spare_blocks)]
        for action in ("start", "wait"):
            for start, needed in fills:
                if needed is None:
                    getattr(zero_fill(start), action)()
                else:
                    @pl.when(needed)
                    def _():
                        getattr(zero_fill(start), action)()

    def tile_done(tile, b):
        for needed, _, size in _pieces(total_ref[tile], 2 * MOE_TILE):
            @pl.when(needed)
            def _():
                pltpu.make_async_copy(xc_scr.at[b, pl.ds(0, size)], xs_ref.at[pl.ds(0, size)],
                                      sems.at[b]).wait()

    @pl.when(i >= 2)
    def _():
        tile_done(i - 2, buf)

    place = _iota((TILE_SLOTS, MOE_TILE), 0)
    onehot = (place == slots_ref[0, 0:1, :]) | (place == slots_ref[0, 1:2, :])
    xc_scr[buf] = jnp.dot(jnp.where(onehot, 1.0, 0.0).astype(bf16), h_ref[...].astype(bf16),
                          preferred_element_type=f32)
    for needed, local, row, size in _expert_runs(cnt_ref, off_ref, row_ref, i):
        @pl.when(needed)
        def _():
            pltpu.make_async_copy(xc_scr.at[buf, pl.ds(local, size)], xs_ref.at[pl.ds(row, size)],
                                  sems.at[buf]).start()

    @pl.when(i == n - 1)
    def _():
        tile_done(i, buf)

        @pl.when(i >= 1)
        def _():
            tile_done(i - 1, 1 - buf)


def _dispatch(h2, slots_t, cnt, off, row, total, pends, rows):
    T = h2.shape[0]
    grid_spec = pltpu.PrefetchScalarGridSpec(
        num_scalar_prefetch=5,
        grid=(T // MOE_TILE,),
        in_specs=[
            pl.BlockSpec((1, 2, MOE_TILE), lambda t, *_: (t, 0, 0)),
            pl.BlockSpec((MOE_TILE, D_MODEL), lambda t, *_: (t, 0)),
        ],
        out_specs=pl.BlockSpec(memory_space=pl.ANY),
        scratch_shapes=[pltpu.VMEM((2, TILE_SLOTS, D_MODEL), f32),
                        pltpu.VMEM((EXPERT_ROWS, D_MODEL), f32),
                        pltpu.SemaphoreType.DMA((2,)), pltpu.SemaphoreType.DMA(())],
    )
    spare_blocks = rows // EXPERT_ROWS - 2 * T // EXPERT_ROWS
    return pl.pallas_call(
        functools.partial(_dispatch_kernel, spare_blocks=spare_blocks),
        grid_spec=grid_spec,
        out_shape=jax.ShapeDtypeStruct((rows, D_MODEL), f32),
        compiler_params=_cparams("arbitrary"),
        name="moe_dispatch",
    )(cnt, off, row, total, pends, slots_t, h2)


def _expert_kernel(be_ref, nv_ref, xs_ref, wg_ref, wu_ref, wd_ref, ys_ref):
    j = pl.program_id(0)

    @pl.when(j < nv_ref[0])
    def _():
        ys_ref[...] = _swiglu_block(xs_ref[...].astype(bf16), wg_ref[0], wu_ref[0], wd_ref[0])

    @pl.when(j >= nv_ref[0])
    def _():
        ys_ref[...] = jnp.zeros(ys_ref.shape, f32)


def _experts(block_e, n_valid, xs, wg, wu, wd):
    rows = xs.shape[0]
    grid_spec = pltpu.PrefetchScalarGridSpec(
        num_scalar_prefetch=2,
        grid=(rows // EXPERT_ROWS,),
        in_specs=[
            pl.BlockSpec((EXPERT_ROWS, D_MODEL), lambda j, be, nv: (jnp.minimum(j, nv[0] - 1), 0)),
            pl.BlockSpec((1, D_MODEL, EXPERT_DIM), lambda j, be, nv: (be[j], 0, 0)),
            pl.BlockSpec((1, D_MODEL, EXPERT_DIM), lambda j, be, nv: (be[j], 0, 0)),
            pl.BlockSpec((1, EXPERT_DIM, D_MODEL), lambda j, be, nv: (be[j], 0, 0)),
        ],
        out_specs=pl.BlockSpec((EXPERT_ROWS, D_MODEL), lambda j, be, nv: (j, 0)),
    )
    return pl.pallas_call(
        _expert_kernel,
        grid_spec=grid_spec,
        out_shape=jax.ShapeDtypeStruct((rows, D_MODEL), f32),
        compiler_params=_cparams("arbitrary"),
        name="moe_experts",
    )(block_e, n_valid, xs, wg, wu, wd)


def _combine_kernel(cnt_ref, off_ref, row_ref, total_ref, slots_ref, x_ref, route_ref, ys_ref,
                    o_ref, yc_scr, sems):
    i = pl.program_id(0)
    n = pl.num_programs(0)
    buf = i % 2

    def fetch(tile, b):
        for needed, local, row, size in _expert_runs(cnt_ref, off_ref, row_ref, tile):
            @pl.when(needed)
            def _():
                pltpu.make_async_copy(ys_ref.at[pl.ds(row, size)], yc_scr.at[b, pl.ds(local, size)],
                                      sems.at[b]).start()

    @pl.when(i == 0)
    def _():
        yc_scr[...] = jnp.zeros(yc_scr.shape, f32)
        fetch(0, 0)

    @pl.when(i + 1 < n)
    def _():
        fetch(i + 1, 1 - buf)

    for needed, _, size in _pieces(total_ref[i], 2 * MOE_TILE):
        @pl.when(needed)
        def _():
            pltpu.make_async_copy(ys_ref.at[pl.ds(0, size)], yc_scr.at[buf, pl.ds(0, size)],
                                  sems.at[buf]).wait()
    y = yc_scr[buf]
    y_hi = y.astype(bf16)
    y_lo = (y - y_hi.astype(f32)).astype(bf16)
    place = _iota((MOE_TILE, TILE_SLOTS), 1)
    route = route_ref[...]
    out = x_ref[...]
    for k, w_lane in ((0, ROUTE_W1), (1, ROUTE_W2)):
        pick = jnp.where(place == slots_ref[:, k:k + 1], 1.0, 0.0).astype(bf16)
        yk = (jnp.dot(pick, y_hi, preferred_element_type=f32)
              + jnp.dot(pick, y_lo, preferred_element_type=f32))
        out = out + route[:, w_lane:w_lane + 1] * yk
    o_ref[...] = out


def _combine(x2, route, slots, cnt, off, row, total, ys):
    T = x2.shape[0]
    grid_spec = pltpu.PrefetchScalarGridSpec(
        num_scalar_prefetch=4,
        grid=(T // MOE_TILE,),
        in_specs=[
            pl.BlockSpec((MOE_TILE, 2), lambda t, *_: (t, 0)),
            pl.BlockSpec((MOE_TILE, D_MODEL), lambda t, *_: (t, 0)),
            pl.BlockSpec((MOE_TILE, LANES), lambda t, *_: (t, 0)),
            pl.BlockSpec(memory_space=pl.ANY),
        ],
        out_specs=pl.BlockSpec((MOE_TILE, D_MODEL), lambda t, *_: (t, 0)),
        scratch_shapes=[pltpu.VMEM((2, TILE_SLOTS, D_MODEL), f32), pltpu.SemaphoreType.DMA((2,))],
    )
    return pl.pallas_call(
        _combine_kernel,
        grid_spec=grid_spec,
        out_shape=jax.ShapeDtypeStruct((T, D_MODEL), f32),
        compiler_params=_cparams("arbitrary"),
        name="moe_combine",
    )(cnt, off, row, total, slots, x2, route, ys)


def _moe(x2, h2, route, wg, wu, wd):
    T = x2.shape[0]
    nt = T // MOE_TILE
    e12 = route[:, ROUTE_E1:ROUTE_E2 + 1].astype(jnp.int32)
    r12 = route[:, ROUTE_R1:ROUTE_R2 + 1].astype(jnp.int32)
    chosen = (e12[:, :, None] == jnp.arange(N_EXPERTS, dtype=jnp.int32)).astype(jnp.int32)
    cnt = chosen.sum(axis=1).reshape(nt, MOE_TILE, N_EXPERTS).sum(axis=1)
    run = (cnt + RUN_ALIGN - 1) // RUN_ALIGN * RUN_ALIGN
    off = jnp.cumsum(run, axis=1) - run
    before = jnp.cumsum(run, axis=0) - run
    padded = (run.sum(axis=0) + EXPERT_ROWS - 1) // EXPERT_ROWS * EXPERT_ROWS
    pends = jnp.cumsum(padded)
    row = (pends - padded)[None, :] + before
    n_blocks = -(-(2 * T + nt * N_EXPERTS * (RUN_ALIGN - 1)) // EXPERT_ROWS) + N_EXPERTS
    block_e = jnp.minimum(
        jnp.searchsorted(pends, jnp.arange(n_blocks, dtype=jnp.int32) * EXPERT_ROWS, side='right'),
        N_EXPERTS - 1).astype(jnp.int32)
    n_valid = (pends[-1:] // EXPERT_ROWS).astype(jnp.int32)
    first_rank = jnp.cumsum(cnt, axis=0) - cnt
    shift = jnp.repeat(off - first_rank, MOE_TILE, axis=0)
    slots = r12 + jnp.take_along_axis(shift, e12, axis=1)
    slots_t = slots.reshape(nt, MOE_TILE, 2).transpose(0, 2, 1)
    flat = lambda a: a.reshape(-1).astype(jnp.int32)
    tables = (flat(run), flat(off), flat(row), flat(run.sum(axis=1)))
    xs = _dispatch(h2, slots_t, *tables, pends.astype(jnp.int32), n_blocks * EXPERT_ROWS)
    ys = _experts(block_e, n_valid, xs, wg, wu, wd)
    return _combine(x2, route, slots, *tables, ys)


def _pad_cols(w, n):
    return jnp.pad(w, ((0, 0), (0, n - w.shape[1])))


def _layer_weights(w_in, c_w_uq, c_w_ukv):
    gate_end = N_BRANCH * D_MODEL
    a_end = gate_end + 2 * WIDTH
    b_end = a_end + 3 * WIDTH
    cq = w_in[:, b_end:b_end + C_Q_RANK]
    ckv = w_in[:, b_end + C_Q_RANK:b_end + C_Q_RANK + C_KV_RANK]
    kr = w_in[:, b_end + C_Q_RANK + C_KV_RANK:b_end + C_Q_RANK + C_KV_RANK + C_ROPE_DIM]
    d_start = b_end + C_Q_RANK + C_KV_RANK + C_ROPE_DIM
    kr_block = jnp.pad(kr, ((0, 0), (C_NOPE_DIM, C_HEAD_PAD - C_QK_DIM)))
    w = jnp.concatenate([w_in[:, :b_end], _pad_cols(cq, WIDTH), ckv, kr_block, w_in[:, d_start:]],
                        axis=1).astype(bf16)
    wuq = c_w_uq.reshape(C_Q_RANK, C_HEADS, C_QK_DIM)
    wuq = jnp.pad(wuq, ((0, WIDTH - C_Q_RANK), (0, 0), (0, C_HEAD_PAD - C_QK_DIM)))
    wuq = wuq.reshape(WIDTH, C_HEADS * C_HEAD_PAD).astype(bf16)
    wukv = c_w_ukv.reshape(C_KV_RANK, C_HEADS, C_NOPE_DIM + C_V_DIM)
    wuk = jnp.pad(wukv[:, :, :C_NOPE_DIM], ((0, 0), (0, 0), (0, C_HEAD_PAD - C_NOPE_DIM)))
    wuk = wuk.reshape(C_KV_RANK, C_HEADS * C_HEAD_PAD).astype(bf16)
    wuv = wukv[:, :, C_NOPE_DIM:].reshape(C_KV_RANK, C_HEADS * C_V_DIM).astype(bf16)
    return w, wuq, wuk, wuv


def _rotation_constants():
    ch = jnp.arange(WIDTH)
    gsum_b = ((ch[:, None] // B_HEAD_DIM) == (ch[None, :] // B_HEAD_DIM)).astype(f32) / B_HEAD_DIM
    half_b = B_ROT_DIM // 2
    src, dst = ch[:, None], ch[None, :]
    same = (src // B_HEAD_DIM) == (dst // B_HEAD_DIM)
    ds_, dd = src % B_HEAD_DIM, dst % B_HEAD_DIM
    rot_b = jnp.where(same & (dd < half_b) & (ds_ == dd + half_b), -1.0,
                      jnp.where(same & (dd >= half_b) & (dd < B_ROT_DIM) & (ds_ == dd - half_b), 1.0, 0.0))
    lc = jnp.arange(C_HEAD_PAD)
    real = lc < C_QK_DIM
    gsum_c = (real[:, None] & real[None, :]).astype(f32) / C_QK_DIM
    half_c = C_ROPE_DIM // 2
    s, d = lc[:, None], lc[None, :]
    lo = (d >= C_NOPE_DIM) & (d < C_NOPE_DIM + half_c)
    hi = (d >= C_NOPE_DIM + half_c) & (d < C_QK_DIM)
    rot_c = jnp.where(lo & (s == d + half_c), -1.0, jnp.where(hi & (s == d - half_c), 1.0, 0.0))
    slot = jnp.arange(ATT_HEADS * HEAD_SLOT)
    place_b = ((slot[None, :] // HEAD_SLOT == ch[:, None] // B_HEAD_DIM)
               & (slot[None, :] % HEAD_SLOT == ch[:, None] % B_HEAD_DIM))
    return (gsum_b.astype(bf16), rot_b.astype(bf16), place_b.astype(bf16),
            gsum_c.astype(bf16), rot_c.astype(bf16))


def kernel(x, positions, attn_norm, ffn_norm, w_in, a_ln_g, a_ln_b, a_ws, a_bs, b_qn, b_kn, c_cq_norm, c_ckv_norm, c_w_uq, c_w_ukv, c_qn, c_kn, d_conv_w, d_conv_b, d_ln_g, d_ln_b, w_branch, w_out, ffn_w_gate, ffn_w_up, ffn_w_down, moe_router, moe_router_b, moe_w_gate, moe_w_up, moe_w_down):
    B, S, _ = x.shape
    T = B * S
    assert S % ATT_BLOCK == 0 and T % EXPERT_ROWS == 0
    x2 = x.reshape(T, D_MODEL)
    cos_b, sin_b, cos_c, sin_c = _rope_tables(positions)
    gsum_b, rot_b, place_b, gsum_c, rot_c = _rotation_constants()
    row = lambda v: v.reshape(1, -1)

    for l in range(DEPTH):
        w, wuq, wuk, wuv = _layer_weights(w_in[l], c_w_uq[l], c_w_ukv[l])
        proj = _inproj(x2, row(attn_norm[l]), w)

        o_a = _sgu(proj, row(a_ln_g[l]), row(a_ln_b[l]), a_ws[l], a_bs[l].T)

        qb, kb, vtb = _moba_prep(proj, B, S, cos_b, sin_b,
                                 row(jnp.tile(b_qn[l], B_HEADS)), row(jnp.tile(b_kn[l], B_HEADS)),
                                 gsum_b, rot_b, place_b)
        o_b = _attention(qb, kb, vtb, B, S, "moba_attention")

        pad_gain = lambda g, n: row(jnp.pad(g, (0, n - g.shape[0])))
        qc, kc, vtc = _mla_prep(proj, B, S, cos_c, sin_c, pad_gain(c_cq_norm[l], WIDTH),
                                row(c_ckv_norm[l]), wuq, wuk, wuv,
                                pad_gain(c_qn[l], C_HEAD_PAD), pad_gain(c_kn[l], C_HEAD_PAD),
                                gsum_c, rot_c)
        o_c = _attention(qc, kc, vtc, B, S, "mla_attention")

        o_d = _conformer(proj, B, S, d_conv_w[l], row(d_conv_b[l]), row(d_ln_g[l]), row(d_ln_b[l]))

        merge_args = (x2, proj, o_a, o_b, o_c, o_d, w_branch[l].astype(bf16),
                      w_out[l].astype(bf16), row(ffn_norm[l]))
        i = l // 2
        if l % 2 == 0:
            x2, h2 = _merge(*merge_args)
            x2 = _ffn(x2, h2, ffn_w_gate[i].astype(bf16), ffn_w_up[i].astype(bf16),
                      ffn_w_down[i].astype(bf16))
        else:
            router = (_pad_cols(moe_router[i], LANES), _pad_cols(row(moe_router_b[i]), LANES))
            x2, h2, route = _merge(*merge_args, router=router)
            x2 = _moe(x2, h2, route, moe_w_gate[i].astype(bf16),
                      moe_w_up[i].astype(bf16), moe_w_down[i].astype(bf16))
    return x2.reshape(B, S, D_MODEL)
```

```python
import functools

import jax
import jax.numpy as jnp
from jax import lax
from jax.experimental import pallas as pl
from jax.experimental.pallas import tpu as pltpu

D_MODEL = 1024
DEPTH = 4
N_BRANCH = 4
ROPE_THETA = 500000.0
NORM_EPS = 1e-6
NEG_INF = -1e30

WIDTH = 256
A_GROUPS = 4
A_GROUP_DIM = 64
A_CHUNK = 128
B_HEADS = 4
B_HEAD_DIM = 64
B_ROT_DIM = 16
MOBA_BLOCK = 256
MOBA_TOPK = 3
C_HEADS = 4
C_Q_RANK = 192
C_KV_RANK = 128
C_NOPE_DIM = 64
C_ROPE_DIM = 32
C_V_DIM = 64
C_QK_DIM = C_NOPE_DIM + C_ROPE_DIM
C_HEAD_PAD = 128
D_CONV = 31
FFN_DIM = 2816
N_EXPERTS = 8
EXPERT_DIM = 1408
EXPERT_ROWS = 512

ATT_BLOCK = 256
ATT_QBLOCK = 512
ATT_HEADS = 4
HEAD_SLOT = 128
V_AUG = C_V_DIM + 16
LOG2E = 1.4426950408889634
SM_ROWS = 32
LANES = 128
SUBLANES = 8
CONV_HALO = 32
CONV_ROWS = 64
LN_ROWS = 128

COL_GATE = 0
COL_AU = 4096
COL_AV = 4352
COL_BQ = 4608
COL_BK = 4864
COL_BV = 5120
COL_CQ = 5376
COL_CKV = 5632
COL_KR = 5760
COL_DA = 5888
COL_DG = 6144
PROJ_COLS = 6400

VMEM_LIMIT = 56 * 1024 * 1024

bf16 = jnp.bfloat16
f32 = jnp.float32


def _cparams(*sem):
    return pltpu.CompilerParams(dimension_semantics=sem, vmem_limit_bytes=VMEM_LIMIT)


def _rms(x, n):
    return x * lax.rsqrt(jnp.sum(x * x, axis=-1, keepdims=True) * (1.0 / n) + NORM_EPS)


def _layernorm(x, g, b):
    mu = jnp.mean(x, axis=-1, keepdims=True)
    xc = x - mu
    return xc * lax.rsqrt(jnp.mean(xc * xc, axis=-1, keepdims=True) + NORM_EPS) * g + b


def _sigmoid(x):
    return 0.5 * jnp.tanh(0.5 * x) + 0.5


def _iota(shape, dim):
    return lax.broadcasted_iota(jnp.int32, shape, dim)


def _rope_table_kernel(pos_ref, inv_ref, eb_ref, ec_ref, cb_ref, sb_ref, cc_ref, sc_ref):
    ang = pos_ref[0].astype(f32) * inv_ref[...]
    c = jnp.cos(ang)
    s = jnp.sin(ang)
    hi = lax.Precision.HIGHEST
    cb_ref[0] = jnp.dot(c, eb_ref[...], precision=hi, preferred_element_type=f32)
    sb_ref[0] = jnp.dot(s, eb_ref[...], precision=hi, preferred_element_type=f32)
    cc_ref[0] = jnp.dot(c, ec_ref[...], precision=hi, preferred_element_type=f32)
    sc_ref[0] = jnp.dot(s, ec_ref[...], precision=hi, preferred_element_type=f32)


def _rope_tables(positions):
    B, S = positions.shape
    R = min(512, S)
    nb = B_ROT_DIM // 2
    nc = C_ROPE_DIM // 2
    inv_b = f32(ROPE_THETA) ** (-jnp.arange(0, B_ROT_DIM, 2, dtype=f32) / B_ROT_DIM)
    inv_c = f32(ROPE_THETA) ** (-jnp.arange(0, C_ROPE_DIM, 2, dtype=f32) / C_ROPE_DIM)
    inv = jnp.zeros((1, LANES), f32).at[0, :nb].set(inv_b).at[0, nb:nb + nc].set(inv_c)
    zero_lane = LANES - 1
    ch = jnp.arange(WIDTH)
    d = ch % B_HEAD_DIM
    src_b = jnp.where(d < B_ROT_DIM, d % nb, zero_lane)
    eb = (jnp.arange(LANES)[:, None] == src_b[None, :]).astype(f32)
    dc = jnp.arange(C_HEAD_PAD)
    in_rope = (dc >= C_NOPE_DIM) & (dc < C_QK_DIM)
    src_c = jnp.where(in_rope, nb + (dc - C_NOPE_DIM) % nc, zero_lane)
    ec = (jnp.arange(LANES)[:, None] == src_c[None, :]).astype(f32)
    pos3 = positions.reshape(B, S, 1)
    out = pl.pallas_call(
        _rope_table_kernel,
        grid=(B, S // R),
        in_specs=[
            pl.BlockSpec((1, R, 1), lambda b, r: (b, r, 0)),
            pl.BlockSpec((1, LANES), lambda b, r: (0, 0)),
            pl.BlockSpec((LANES, WIDTH), lambda b, r: (0, 0)),
            pl.BlockSpec((LANES, C_HEAD_PAD), lambda b, r: (0, 0)),
        ],
        out_specs=[
            pl.BlockSpec((1, R, WIDTH), lambda b, r: (b, r, 0)),
            pl.BlockSpec((1, R, WIDTH), lambda b, r: (b, r, 0)),
            pl.BlockSpec((1, R, C_HEAD_PAD), lambda b, r: (b, r, 0)),
            pl.BlockSpec((1, R, C_HEAD_PAD), lambda b, r: (b, r, 0)),
        ],
        out_shape=[
            jax.ShapeDtypeStruct((B, S, WIDTH), f32),
            jax.ShapeDtypeStruct((B, S, WIDTH), f32),
            jax.ShapeDtypeStruct((B, S, C_HEAD_PAD), f32),
            jax.ShapeDtypeStruct((B, S, C_HEAD_PAD), f32),
        ],
        compiler_params=_cparams("parallel", "parallel"),
        name="rope_tables",
    )(pos3, inv, eb, ec)
    return out


PROJ_CHUNKS = tuple((lo, lo + D_MODEL) for lo in range(0, COL_AU, D_MODEL)) + (
    (COL_AU, COL_CQ), (COL_CQ, PROJ_COLS))


def _inproj_kernel(x_ref, g_ref, w_ref, o_ref):
    h = (_rms(x_ref[...], D_MODEL) * g_ref[...]).astype(bf16)
    for lo, hi in PROJ_CHUNKS:
        y = jnp.dot(h, w_ref[:, lo:hi], preferred_element_type=f32)
        if hi <= COL_AU:
            y = _sigmoid(y)
        o_ref[:, lo:hi] = y.astype(bf16)


def _inproj(x2, g, w):
    T = x2.shape[0]
    tm = min(512, T)
    return pl.pallas_call(
        _inproj_kernel,
        grid=(T // tm,),
        in_specs=[
            pl.BlockSpec((tm, D_MODEL), lambda t: (t, 0)),
            pl.BlockSpec((1, D_MODEL), lambda t: (0, 0)),
            pl.BlockSpec((D_MODEL, PROJ_COLS), lambda t: (0, 0)),
        ],
        out_specs=pl.BlockSpec((tm, PROJ_COLS), lambda t: (t, 0)),
        out_shape=jax.ShapeDtypeStruct((T, PROJ_COLS), bf16),
        compiler_params=_cparams("parallel"),
        name="inproj",
    )(x2, g, w)


def _sgu_kernel(u_ref, v_ref, g_ref, b_ref, ws_ref, bst_ref, o_ref, *, chunks):
    row = _iota((A_CHUNK, A_CHUNK), 0)
    col = _iota((A_CHUNK, A_CHUNK), 1)
    causal = col <= row
    group = _iota((A_CHUNK, WIDTH), 1) // A_GROUP_DIM
    wm = [jnp.where(causal, ws_ref[g], 0.0).astype(bf16) for g in range(A_GROUPS)]
    bias = jnp.zeros((A_CHUNK, WIDTH), f32)
    for g in range(A_GROUPS):
        bias = jnp.where(group == g, bst_ref[:, g:g + 1], bias)
    for c in range(chunks):
        rows = slice(c * A_CHUNK, (c + 1) * A_CHUNK)
        vn = _layernorm(v_ref[rows, :].astype(f32), g_ref[...], b_ref[...]).astype(bf16)
        z = bias
        for g in range(A_GROUPS):
            zg = jnp.dot(wm[g], vn, preferred_element_type=f32)
            z = z + jnp.where(group == g, zg, 0.0)
        o_ref[rows, :] = (u_ref[rows, :].astype(f32) * z).astype(bf16)


def _sgu(proj, ln_g, ln_b, ws, bs_t):
    T = proj.shape[0]
    ta = min(512, T)
    cu, cv = COL_AU // WIDTH, COL_AV // WIDTH
    return pl.pallas_call(
        functools.partial(_sgu_kernel, chunks=ta // A_CHUNK),
        grid=(T // ta,),
        in_specs=[
            pl.BlockSpec((ta, WIDTH), lambda t: (t, cu)),
            pl.BlockSpec((ta, WIDTH), lambda t: (t, cv)),
            pl.BlockSpec((1, WIDTH), lambda t: (0, 0)),
            pl.BlockSpec((1, WIDTH), lambda t: (0, 0)),
            pl.BlockSpec((A_GROUPS, A_CHUNK, A_CHUNK), lambda t: (0, 0, 0)),
            pl.BlockSpec((A_CHUNK, A_GROUPS), lambda t: (0, 0)),
        ],
        out_specs=pl.BlockSpec((ta, WIDTH), lambda t: (t, 0)),
        out_shape=jax.ShapeDtypeStruct((T, WIDTH), bf16),
        compiler_params=_cparams("parallel"),
        name="sgu",
    )(proj, proj, ln_g, ln_b, ws, bs_t)


def _conv_kernel(a_ref, g_ref, w_ref, cb_ref, lg_ref, lb_ref, o_ref, z_scr, *, seq):
    first = CONV_HALO - (D_CONV - 1)
    z_scr[0, 0:CONV_HALO, :] = jnp.zeros((CONV_HALO, WIDTH), f32)
    z_scr[0, CONV_HALO + seq:, :] = jnp.zeros((CONV_ROWS, WIDTH), f32)

    def glu(c, carry):
        base = pl.multiple_of(c * LN_ROWS, LN_ROWS)
        rows = pl.ds(base, LN_ROWS)
        z_scr[0, pl.ds(base + CONV_HALO, LN_ROWS), :] = (
            a_ref[rows, :].astype(f32) * _sigmoid(g_ref[rows, :].astype(f32)))
        return carry

    lax.fori_loop(0, seq // LN_ROWS, glu, 0)

    def shift(c, carry):
        base = pl.multiple_of(c * CONV_ROWS, CONV_ROWS)
        win = z_scr[0, pl.ds(base, CONV_ROWS + SUBLANES), :]
        for r in range(1, SUBLANES):
            rolled = pltpu.roll(win, shift=CONV_ROWS + SUBLANES - r, axis=0)
            z_scr[r, pl.ds(base, CONV_ROWS), :] = rolled[:CONV_ROWS]
        return carry

    lax.fori_loop(0, seq // CONV_ROWS + 1, shift, 0)

    def taps(c, carry):
        base = pl.multiple_of(c * CONV_ROWS, CONV_ROWS)
        acc = jnp.zeros((CONV_ROWS, WIDTH), f32) + cb_ref[...]
        for k in range(D_CONV):
            off = first + k
            lo = pl.multiple_of(base + off // SUBLANES * SUBLANES, SUBLANES)
            acc = acc + w_ref[k:k + 1, :] * z_scr[off % SUBLANES, pl.ds(lo, CONV_ROWS), :]
        z_scr[1, pl.ds(base, CONV_ROWS), :] = acc
        return carry

    lax.fori_loop(0, seq // CONV_ROWS, taps, 0)

    def norm(c, carry):
        base = pl.multiple_of(c * LN_ROWS, LN_ROWS)
        y = _layernorm(z_scr[1, pl.ds(base, LN_ROWS), :], lg_ref[...], lb_ref[...])
        o_ref[pl.ds(base, LN_ROWS), :] = (y * _sigmoid(y)).astype(bf16)
        return carry

    lax.fori_loop(0, seq // LN_ROWS, norm, 0, unroll=2)


def _conformer(proj, B, S, conv_w, conv_b, ln_g, ln_b):
    ca, cg = COL_DA // WIDTH, COL_DG // WIDTH
    return pl.pallas_call(
        functools.partial(_conv_kernel, seq=S),
        grid=(B,),
        in_specs=[
            pl.BlockSpec((S, WIDTH), lambda b: (b, ca)),
            pl.BlockSpec((S, WIDTH), lambda b: (b, cg)),
            pl.BlockSpec((D_CONV, WIDTH), lambda b: (0, 0)),
            pl.BlockSpec((1, WIDTH), lambda b: (0, 0)),
            pl.BlockSpec((1, WIDTH), lambda b: (0, 0)),
            pl.BlockSpec((1, WIDTH), lambda b: (0, 0)),
        ],
        out_specs=pl.BlockSpec((S, WIDTH), lambda b: (b, 0)),
        out_shape=jax.ShapeDtypeStruct((B * S, WIDTH), bf16),
        scratch_shapes=[pltpu.VMEM((SUBLANES, CONV_HALO + S + CONV_ROWS, WIDTH), f32)],
        compiler_params=_cparams("parallel"),
        name="conformer_conv",
    )(proj, proj, conv_w, conv_b, ln_g, ln_b)


def _value_rows(v_tok):
    vt = v_tok.T
    ones = jnp.ones((V_AUG - C_V_DIM, vt.shape[1]), f32)
    rows = []
    for h in range(ATT_HEADS):
        rows += [vt[h * C_V_DIM:(h + 1) * C_V_DIM], ones]
    return jnp.concatenate(rows, axis=0).astype(bf16)


def _moba_prep_kernel(q_ref, k_ref, v_ref, cos_ref, sin_ref, qg_ref, kg_ref, gsum_ref, rot_ref,
                      place_ref, qo_ref, ko_ref, vt_ref, kmean_scr, *, nblk):
    c = pl.program_id(1)

    @pl.when(c == 0)
    def _():
        kmean_scr[...] = jnp.zeros(kmean_scr.shape, f32)

    cos = cos_ref[0]
    sin = sin_ref[0]

    def norm_rope(x, gain):
        ss = jnp.dot((x * x).astype(bf16), gsum_ref[...], preferred_element_type=f32)
        xn = x * lax.rsqrt(ss + NORM_EPS) * gain
        xr = jnp.dot(xn.astype(bf16), rot_ref[...], preferred_element_type=f32)
        return xn * cos + xr * sin

    q = norm_rope(q_ref[...].astype(f32), qg_ref[...])
    k = norm_rope(k_ref[...].astype(f32), kg_ref[...])
    slot_lane = _iota((ATT_BLOCK, ATT_HEADS * HEAD_SLOT), 1) % HEAD_SLOT
    k_slots = jnp.dot(k.astype(bf16), place_ref[...], preferred_element_type=f32)
    ko_ref[0, 0] = jnp.where(slot_lane == B_HEAD_DIM + c, 1.0, k_slots).astype(bf16)
    vt_ref[0, 0] = _value_rows(v_ref[...].astype(f32))

    head = _iota((nblk, WIDTH), 1) // B_HEAD_DIM
    blk = _iota((nblk, ATT_BLOCK), 0)
    past = blk < c
    kmean = kmean_scr[...]
    zeros_lo = jnp.zeros((B_HEAD_DIM, ATT_BLOCK), f32)
    zeros_hi = jnp.zeros((HEAD_SLOT - B_HEAD_DIM - nblk, ATT_BLOCK), f32)
    bias_rows = []
    for h in range(B_HEADS):
        km = jnp.where(head == h, kmean, 0.0)
        sc = lax.dot_general(km, q, (((1,), (1,)), ((), ())), precision=lax.Precision.HIGHEST,
                             preferred_element_type=f32)
        sc = jnp.where(past, sc, NEG_INF)
        rank = jnp.zeros((nblk, ATT_BLOCK), jnp.int32)
        for j in range(nblk):
            other = sc[j:j + 1, :]
            ahead = (other > sc) | ((other == sc) & (j < blk))
            rank = rank + ahead.astype(jnp.int32)
        visible = (past & (rank < MOBA_TOPK)) | (blk == c)
        bias_rows += [zeros_lo, jnp.where(visible, 0.0, NEG_INF), zeros_hi]
    bias = jnp.concatenate(bias_rows, axis=0)
    q_slots = jnp.dot((q * (B_HEAD_DIM ** -0.5 * LOG2E)).astype(bf16), place_ref[...],
                      preferred_element_type=f32)
    qo_ref[0] = (q_slots.T + bias).astype(bf16)

    kmean_scr[pl.ds(c, 1), :] = jnp.mean(k, axis=0, keepdims=True)


def _moba_prep(proj, B, S, cos_b, sin_b, qg, kg, gsum, rot, place):
    nblk = S // ATT_BLOCK
    assert nblk <= HEAD_SLOT - B_HEAD_DIM
    HD = ATT_HEADS * HEAD_SLOT
    cq, ck, cv = COL_BQ // WIDTH, COL_BK // WIDTH, COL_BV // WIDTH
    tile = lambda col: pl.BlockSpec((ATT_BLOCK, WIDTH), lambda b, c: (b * nblk + c, col))
    table = pl.BlockSpec((1, ATT_BLOCK, WIDTH), lambda b, c: (b, c, 0))
    const = lambda shape: pl.BlockSpec(shape, lambda b, c: (0,) * len(shape))
    return pl.pallas_call(
        functools.partial(_moba_prep_kernel, nblk=nblk),
        grid=(B, nblk),
        in_specs=[tile(cq), tile(ck), tile(cv), table, table,
                  const((1, WIDTH)), const((1, WIDTH)), const((WIDTH, WIDTH)), const((WIDTH, WIDTH)),
                  const((WIDTH, HD))],
        out_specs=[
            pl.BlockSpec((1, HD, ATT_BLOCK), lambda b, c: (b, 0, c)),
            pl.BlockSpec((1, 1, ATT_BLOCK, HD), lambda b, c: (b, c, 0, 0)),
            pl.BlockSpec((1, 1, ATT_HEADS * V_AUG, ATT_BLOCK), lambda b, c: (b, c, 0, 0)),
        ],
        out_shape=[
            jax.ShapeDtypeStruct((B, HD, S), bf16),
            jax.ShapeDtypeStruct((B, nblk, ATT_BLOCK, HD), bf16),
            jax.ShapeDtypeStruct((B, nblk, ATT_HEADS * V_AUG, ATT_BLOCK), bf16),
        ],
        scratch_shapes=[pltpu.VMEM((nblk, WIDTH), f32)],
        compiler_params=_cparams("parallel", "arbitrary"),
        name="moba_prep",
    )(proj, proj, proj, cos_b, sin_b, qg, kg, gsum, rot, place)


def _mla_prep_kernel(cq_ref, ckv_ref, kr_ref, cos_ref, sin_ref, cqg_ref, ckvg_ref, wuq_ref, wuk_ref,
                     wuv_ref, qg_ref, kg_ref, gsum_ref, rot_ref, qo_ref, ko_ref, vt_ref):
    cqn = (_rms(cq_ref[...].astype(f32), C_Q_RANK) * cqg_ref[...]).astype(bf16)
    ckvn = (_rms(ckv_ref[...].astype(f32), C_KV_RANK) * ckvg_ref[...]).astype(bf16)
    q = jnp.dot(cqn, wuq_ref[...], preferred_element_type=f32)
    k = jnp.dot(ckvn, wuk_ref[...], preferred_element_type=f32)
    v = jnp.dot(ckvn, wuv_ref[...], preferred_element_type=f32)
    kr = kr_ref[...].astype(f32)
    cos = cos_ref[0]
    sin = sin_ref[0]

    def norm_rope(x, gain):
        ss = jnp.dot((x * x).astype(bf16), gsum_ref[...], preferred_element_type=f32)
        xn = x * lax.rsqrt(ss + NORM_EPS) * gain
        xr = jnp.dot(xn.astype(bf16), rot_ref[...], preferred_element_type=f32)
        return xn * cos + xr * sin

    for h in range(C_HEADS):
        sl = slice(h * C_HEAD_PAD, (h + 1) * C_HEAD_PAD)
        qh = norm_rope(q[:, sl], qg_ref[...])
        kh = norm_rope(k[:, sl] + kr, kg_ref[...])
        qo_ref[0, sl, :] = (qh * (C_QK_DIM ** -0.5 * LOG2E)).T.astype(bf16)
        ko_ref[0, 0, :, sl] = kh.astype(bf16)
    vt_ref[0, 0] = _value_rows(v)


def _mla_prep(proj, B, S, cos_c, sin_c, cqg, ckvg, wuq, wuk, wuv, qg, kg, gsum, rot):
    nblk = S // ATT_BLOCK
    HD = C_HEADS * C_HEAD_PAD
    table = pl.BlockSpec((1, ATT_BLOCK, C_HEAD_PAD), lambda b, c: (b, c, 0))
    const = lambda shape: pl.BlockSpec(shape, lambda b, c: (0,) * len(shape))
    return pl.pallas_call(
        _mla_prep_kernel,
        grid=(B, nblk),
        in_specs=[
            pl.BlockSpec((ATT_BLOCK, WIDTH), lambda b, c: (b * nblk + c, COL_CQ // WIDTH)),
            pl.BlockSpec((ATT_BLOCK, LANES), lambda b, c: (b * nblk + c, COL_CKV // LANES)),
            pl.BlockSpec((ATT_BLOCK, LANES), lambda b, c: (b * nblk + c, COL_KR // LANES)),
            table, table,
            const((1, WIDTH)), const((1, C_KV_RANK)), const((WIDTH, HD)), const((C_KV_RANK, HD)),
            const((C_KV_RANK, WIDTH)), const((1, C_HEAD_PAD)), const((1, C_HEAD_PAD)),
            const((C_HEAD_PAD, C_HEAD_PAD)), const((C_HEAD_PAD, C_HEAD_PAD)),
        ],
        out_specs=[
            pl.BlockSpec((1, HD, ATT_BLOCK), lambda b, c: (b, 0, c)),
            pl.BlockSpec((1, 1, ATT_BLOCK, HD), lambda b, c: (b, c, 0, 0)),
            pl.BlockSpec((1, 1, ATT_HEADS * V_AUG, ATT_BLOCK), lambda b, c: (b, c, 0, 0)),
        ],
        out_shape=[
            jax.ShapeDtypeStruct((B, HD, S), bf16),
            jax.ShapeDtypeStruct((B, nblk, ATT_BLOCK, HD), bf16),
            jax.ShapeDtypeStruct((B, nblk, ATT_HEADS * V_AUG, ATT_BLOCK), bf16),
        ],
        compiler_params=_cparams("parallel", "parallel"),
        name="mla_prep",
    )(proj, proj, proj, cos_c, sin_c, cqg, ckvg, wuq, wuk, wuv, qg, kg, gsum, rot)


def _attn_kernel(q_ref, k_ref, vt_ref, o_ref, *scratch):
    H = ATT_HEADS
    s_scr, p_scr, a_scr = ([scratch[(2 * n + par) * H:(2 * n + par + 1) * H] for par in range(2)]
                           for n in range(3))
    acc_scr, m_scr = (scratch[n * H:(n + 1) * H] for n in (6, 7))
    out_scr = scratch[8 * H]
    top_scr = [scratch[8 * H + 1 + par * H:8 * H + 1 + (par + 1) * H] for par in range(2)]
    pairs = pl.program_id(1) * (ATT_QBLOCK // ATT_BLOCK // 2)
    chunks = ATT_BLOCK // SM_ROWS

    def scores(j, par, h):
        slot = slice(h * HEAD_SLOT, (h + 1) * HEAD_SLOT)
        s = jnp.dot(k_ref[0, j, :, slot], q_ref[0, slot, :],
                    preferred_element_type=f32)
        s_scr[par][h][...] = s
        top_scr[par][h][...] = jnp.max(s, axis=0, keepdims=True)

    def softmax(par, h, diag_offset=None):
        def rows(r):
            s = s_scr[par][h][r * SM_ROWS:(r + 1) * SM_ROWS, :]
            if diag_offset is not None:
                kpos = _iota((SM_ROWS, ATT_QBLOCK), 0) + (r * SM_ROWS + diag_offset)
                s = jnp.where(kpos <= _iota((SM_ROWS, ATT_QBLOCK), 1), s, NEG_INF)
            return s
        if diag_offset is None:
            top = top_scr[par][h][...]
        else:
            mx = rows(0)
            for r in range(1, chunks):
                mx = jnp.maximum(mx, rows(r))
            top = jnp.max(mx, axis=0, keepdims=True)
        m_old = m_scr[h][...]
        m_new = jnp.maximum(m_old, top)
        a_scr[par][h][...] = jnp.exp2(m_old - m_new)
        m_scr[h][...] = m_new
        for r in range(chunks):
            p_scr[par][h][r * SM_ROWS:(r + 1) * SM_ROWS, :] = (
                jnp.exp2((rows(r) - m_new).astype(bf16)))

    def values(j, par, h):
        pv = jnp.dot(vt_ref[0, j, h * V_AUG:(h + 1) * V_AUG, :], p_scr[par][h][...],
                     preferred_element_type=f32)
        acc_scr[h][...] = a_scr[par][h][...] * acc_scr[h][...] + pv

    for h in range(H):
        m_scr[h][...] = jnp.full(m_scr[h].shape, NEG_INF, f32)
        a_scr[1][h][...] = jnp.zeros(a_scr[1][h].shape, f32)
        acc_scr[h][...] = jnp.zeros(acc_scr[h].shape, f32)
        p_scr[1][h][...] = jnp.zeros(p_scr[1][h].shape, bf16)
        scores(0, 0, h)

    def step(b, par, diag_offset=None, last=False):
        for h in range(H):
            if not last:
                scores(b + 1, 1 - par, h)
            values(jnp.maximum(b - 1, 0), 1 - par, h)
            softmax(par, h, diag_offset)

    def past_pair(t, carry):
        step(2 * t, 0)
        step(2 * t + 1, 1)
        return carry

    lax.fori_loop(0, pairs, past_pair, 0)
    step(2 * pairs, 0, diag_offset=0)
    step(2 * pairs + 1, 1, diag_offset=ATT_BLOCK, last=True)

    for h in range(H):
        values(2 * pairs + 1, 1, h)
        a = acc_scr[h][...]
        out_scr[h * C_V_DIM:(h + 1) * C_V_DIM, :] = a[:C_V_DIM] / a[C_V_DIM:C_V_DIM + 1]
    o_ref[...] = out_scr[...].T.astype(bf16)


def _attention(q, k, vt, B, S, name):
    nblk = S // ATT_BLOCK
    nq = S // ATT_QBLOCK
    HD = ATT_HEADS * HEAD_SLOT
    return pl.pallas_call(
        _attn_kernel,
        grid=(B, nq),
        in_specs=[
            pl.BlockSpec((1, HD, ATT_QBLOCK), lambda b, i: (b, 0, i)),
            pl.BlockSpec((1, nblk, ATT_BLOCK, HD), lambda b, i: (b, 0, 0, 0)),
            pl.BlockSpec((1, nblk, ATT_HEADS * V_AUG, ATT_BLOCK), lambda b, i: (b, 0, 0, 0)),
        ],
        out_specs=pl.BlockSpec((ATT_QBLOCK, WIDTH), lambda b, i: (b * nq + i, 0)),
        out_shape=jax.ShapeDtypeStruct((B * S, WIDTH), bf16),
        scratch_shapes=(
            [pltpu.VMEM((ATT_BLOCK, ATT_QBLOCK), f32) for _ in range(2 * ATT_HEADS)]
            + [pltpu.VMEM((ATT_BLOCK, ATT_QBLOCK), bf16) for _ in range(2 * ATT_HEADS)]
            + [pltpu.VMEM((1, ATT_QBLOCK), f32) for _ in range(2 * ATT_HEADS)]
            + [pltpu.VMEM((V_AUG, ATT_QBLOCK), f32) for _ in range(ATT_HEADS)]
            + [pltpu.VMEM((1, ATT_QBLOCK), f32) for _ in range(ATT_HEADS)]
            + [pltpu.VMEM((WIDTH, ATT_QBLOCK), f32)]
            + [pltpu.VMEM((1, ATT_QBLOCK), f32) for _ in range(2 * ATT_HEADS)]),
        compiler_params=_cparams("parallel", "arbitrary"),
        name=name,
    )(q, k, vt)


ROUTE_E1, ROUTE_E2, ROUTE_W1, ROUTE_W2, ROUTE_R1, ROUTE_R2 = range(6)


def _route_rows(h, wr_ref, br_ref, route_ref, carry_scr):
    tm = h.shape[0]

    @pl.when(pl.program_id(0) == 0)
    def _():
        carry_scr[...] = jnp.zeros(carry_scr.shape, f32)

    lane = _iota((tm, LANES), 1)
    h_hi = h.astype(bf16)
    h_lo = (h - h_hi.astype(f32)).astype(bf16)
    w = wr_ref[...]
    w_hi = w.astype(bf16)
    w_lo = (w - w_hi.astype(f32)).astype(bf16)
    logits = (jnp.dot(h_hi, w_hi, preferred_element_type=f32)
              + (jnp.dot(h_hi, w_lo, preferred_element_type=f32)
                 + jnp.dot(h_lo, w_hi, preferred_element_type=f32))) + br_ref[...]
    logits = jnp.where(lane < N_EXPERTS, logits, NEG_INF)
    m1 = jnp.max(logits, axis=-1, keepdims=True)
    e1 = jnp.min(jnp.where(logits == m1, lane, LANES), axis=-1, keepdims=True)
    rest = jnp.where(lane == e1, NEG_INF, logits)
    m2 = jnp.max(rest, axis=-1, keepdims=True)
    e2 = jnp.min(jnp.where(rest == m2, lane, LANES), axis=-1, keepdims=True)
    ex = jnp.exp(m2 - m1)
    w1 = 1.0 / (1.0 + ex)
    w2 = ex / (1.0 + ex)
    chosen = ((lane == e1) | (lane == e2)).astype(f32)
    earlier = (_iota((tm, tm), 1) < _iota((tm, tm), 0)).astype(bf16)
    before = jnp.dot(earlier, chosen.astype(bf16), preferred_element_type=f32) + carry_scr[...]
    r1 = jnp.sum(jnp.where(lane == e1, before, 0.0), axis=-1, keepdims=True)
    r2 = jnp.sum(jnp.where(lane == e2, before, 0.0), axis=-1, keepdims=True)
    out = jnp.zeros((tm, LANES), f32)
    for idx, val in ((ROUTE_E1, e1.astype(f32)), (ROUTE_E2, e2.astype(f32)), (ROUTE_W1, w1),
                     (ROUTE_W2, w2), (ROUTE_R1, r1), (ROUTE_R2, r2)):
        out = jnp.where(lane == idx, val, out)
    route_ref[...] = out
    carry_scr[...] += jnp.sum(chosen, axis=0, keepdims=True)


def _merge_kernel(x_ref, g0_ref, g1_ref, g2_ref, g3_ref, oa_ref, ob_ref, oc_ref, od_ref,
                  wb_ref, wo_ref, ng_ref, *rest, routed):
    branches = ((g0_ref, oa_ref), (g1_ref, ob_ref), (g2_ref, oc_ref), (g3_ref, od_ref))
    xo_ref, ho_ref = rest[2:4] if routed else rest
    tm = x_ref.shape[0]
    halves = [slice(r * (tm // 2), (r + 1) * (tm // 2)) for r in range(2)]
    projected = [[jnp.dot(o_ref[rows, :], wb_ref[n], preferred_element_type=f32)
                  for n, (_, o_ref) in enumerate(branches)] for rows in halves]
    hs = []
    for rows, proj_half in zip(halves, projected):
        y = None
        for (g_ref, _), br in zip(branches, proj_half):
            t = g_ref[rows, :].astype(f32) * br
            y = t if y is None else y + t
        x = x_ref[rows, :] + jnp.dot(y.astype(bf16), wo_ref[...], preferred_element_type=f32)
        xo_ref[rows, :] = x
        hs.append(_rms(x, D_MODEL) * ng_ref[...])
    if routed:
        wr_ref, br_ref, _, _, route_ref, carry_scr = rest
        h = jnp.concatenate(hs, axis=0)
        ho_ref[...] = h
        _route_rows(h, wr_ref, br_ref, route_ref, carry_scr)
    else:
        for rows, h in zip(halves, hs):
            ho_ref[rows, :] = h.astype(bf16)


def _merge(x2, proj, oa, ob, oc, od, wb, wo, ng, router=None):
    T = x2.shape[0]
    tm = min(512, T)
    routed = router is not None
    gate = lambda n: pl.BlockSpec((tm, D_MODEL), lambda t: (t, n))
    branch = pl.BlockSpec((tm, WIDTH), lambda t: (t, 0))
    rows = pl.BlockSpec((tm, D_MODEL), lambda t: (t, 0))
    in_specs = [
        rows, gate(0), gate(1), gate(2), gate(3), branch, branch, branch, branch,
        pl.BlockSpec((N_BRANCH, WIDTH, D_MODEL), lambda t: (0, 0, 0)),
        pl.BlockSpec((D_MODEL, D_MODEL), lambda t: (0, 0)),
        pl.BlockSpec((1, D_MODEL), lambda t: (0, 0)),
    ]
    args = [x2, proj, proj, proj, proj, oa, ob, oc, od, wb, wo, ng]
    out_specs = [rows, rows]
    out_shape = [jax.ShapeDtypeStruct((T, D_MODEL), f32),
                 jax.ShapeDtypeStruct((T, D_MODEL), f32 if routed else bf16)]
    scratch = []
    if routed:
        in_specs += [pl.BlockSpec((D_MODEL, LANES), lambda t: (0, 0)),
                     pl.BlockSpec((1, LANES), lambda t: (0, 0))]
        args += list(router)
        out_specs += [pl.BlockSpec((tm, LANES), lambda t: (t, 0))]
        out_shape += [jax.ShapeDtypeStruct((T, LANES), f32)]
        scratch = [pltpu.VMEM((1, LANES), f32)]
    return pl.pallas_call(
        functools.partial(_merge_kernel, routed=routed),
        grid=(T // tm,),
        in_specs=in_specs,
        out_specs=out_specs,
        out_shape=out_shape,
        scratch_shapes=scratch,
        compiler_params=_cparams("arbitrary" if routed else "parallel"),
        name="merge_route" if routed else "merge",
    )(*args)


def _swiglu_block(h, wg, wu, wd):
    a = jnp.dot(h, wg, preferred_element_type=f32)
    b = jnp.dot(h, wu, preferred_element_type=f32)
    t = (a * _sigmoid(a) * b).astype(bf16)
    return jnp.dot(t, wd, preferred_element_type=f32)


FFN_CHUNKS = ((0, 1536), (1536, FFN_DIM))


def _ffn_kernel(x_ref, h_ref, wg_ref, wu_ref, wd_ref, o_ref):
    h = h_ref[...]
    acc = x_ref[...]
    for lo, hi in FFN_CHUNKS:
        acc = acc + _swiglu_block(h, wg_ref[:, lo:hi], wu_ref[:, lo:hi], wd_ref[lo:hi, :])
    o_ref[...] = acc


def _ffn(x2, h2, wg, wu, wd):
    T = x2.shape[0]
    tm = min(512, T)
    rows = pl.BlockSpec((tm, D_MODEL), lambda t: (t, 0))
    resident = lambda shape: pl.BlockSpec(shape, lambda t: (0, 0), pipeline_mode=pl.Buffered(1))
    return pl.pallas_call(
        _ffn_kernel,
        grid=(T // tm,),
        in_specs=[rows, rows, resident((D_MODEL, FFN_DIM)), resident((D_MODEL, FFN_DIM)),
                  resident((FFN_DIM, D_MODEL))],
        out_specs=rows,
        out_shape=jax.ShapeDtypeStruct((T, D_MODEL), f32),
        compiler_params=_cparams("parallel"),
        name="dense_swiglu",
    )(x2, h2, wg, wu, wd)


MOE_TILE = 256
RUN_ALIGN = SUBLANES
TILE_SLOTS = 2 * MOE_TILE + N_EXPERTS * RUN_ALIGN


def _pow2_pieces(limit):
    out, b = [], RUN_ALIGN
    while b <= limit:
        out.append(b)
        b *= 2
    return tuple(reversed(out))


def _pieces(value, limit):
    sizes = _pow2_pieces(limit)
    for b in sizes:
        done = value & (~(2 * b - 1) & (2 * sizes[0] - 1))
        yield (value & b) != 0, done, b


def _expert_runs(cnt_ref, off_ref, row_ref, tile):
    for e in range(N_EXPERTS):
        o = off_ref[tile * N_EXPERTS + e]
        d = row_ref[tile * N_EXPERTS + e]
        for needed, done, size in _pieces(cnt_ref[tile * N_EXPERTS + e], MOE_TILE):
            yield (needed, pl.multiple_of(o + done, RUN_ALIGN), pl.multiple_of(d + done, RUN_ALIGN),
                   size)


def _dispatch_kernel(cnt_ref, off_ref, row_ref, total_ref, pends_ref, slots_ref, h_ref, xs_ref,
                     xc_scr, zero_scr, sems, zero_sem, *, spare_blocks):
    i = pl.program_id(0)
    n = pl.num_programs(0)
    buf = i % 2

    def zero_fill(start):
        start = pl.multiple_of(start, EXPERT_ROWS)
        return pltpu.make_async_copy(zero_scr, xs_ref.at[pl.ds(start, EXPERT_ROWS)], zero_sem)

    @pl.when(i == 0)
    def _():
        zero_scr[...] = jnp.zeros(zero_scr.shape, f32)
        used = pends_ref[N_EXPERTS - 1]
        fills = [(jnp.maximum(pends_ref[e] - EXPERT_ROWS, 0), None) for e in range(N_EXPERTS)]
        fills += [(used + e * EXPERT_ROWS, used + e * EXPERT_ROWS < xs_ref.shape[0])
                  for e in range(spare_blocks)]
        for action in ("start", "wait"):
            for start, needed in fills:
                if needed is None:
                    getattr(zero_fill(start), action)()
                else:
                    @pl.when(needed)
                    def _():
                        getattr(zero_fill(start), action)()

    def tile_done(tile, b):
        for needed, _, size in _pieces(total_ref[tile], 2 * MOE_TILE):
            @pl.when(needed)
            def _():
                pltpu.make_async_copy(xc_scr.at[b, pl.ds(0, size)], xs_ref.at[pl.ds(0, size)],
                                      sems.at[b]).wait()

    @pl.when(i >= 2)
    def _():
        tile_done(i - 2, buf)

    place = _iota((TILE_SLOTS, MOE_TILE), 0)
    onehot = (place == slots_ref[0, 0:1, :]) | (place == slots_ref[0, 1:2, :])
    xc_scr[buf] = jnp.dot(jnp.where(onehot, 1.0, 0.0).astype(bf16), h_ref[...].astype(bf16),
                          preferred_element_type=f32)
    for needed, local, row, size in _expert_runs(cnt_ref, off_ref, row_ref, i):
        @pl.when(needed)
        def _():
            pltpu.make_async_copy(xc_scr.at[buf, pl.ds(local, size)], xs_ref.at[pl.ds(row, size)],
                                  sems.at[buf]).start()

    @pl.when(i == n - 1)
    def _():
        tile_done(i, buf)

        @pl.when(i >= 1)
        def _():
            tile_done(i - 1, 1 - buf)


def _dispatch(h2, slots_t, cnt, off, row, total, pends, rows):
    T = h2.shape[0]
    grid_spec = pltpu.PrefetchScalarGridSpec(
        num_scalar_prefetch=5,
        grid=(T // MOE_TILE,),
        in_specs=[
            pl.BlockSpec((1, 2, MOE_TILE), lambda t, *_: (t, 0, 0)),
            pl.BlockSpec((MOE_TILE, D_MODEL), lambda t, *_: (t, 0)),
        ],
        out_specs=pl.BlockSpec(memory_space=pl.ANY),
        scratch_shapes=[pltpu.VMEM((2, TILE_SLOTS, D_MODEL), f32),
                        pltpu.VMEM((EXPERT_ROWS, D_MODEL), f32),
                        pltpu.SemaphoreType.DMA((2,)), pltpu.SemaphoreType.DMA(())],
    )
    spare_blocks = rows // EXPERT_ROWS - 2 * T // EXPERT_ROWS
    return pl.pallas_call(
        functools.partial(_dispatch_kernel, spare_blocks=spare_blocks),
        grid_spec=grid_spec,
        out_shape=jax.ShapeDtypeStruct((rows, D_MODEL), f32),
        compiler_params=_cparams("arbitrary"),
        name="moe_dispatch",
    )(cnt, off, row, total, pends, slots_t, h2)


def _expert_kernel(be_ref, nv_ref, xs_ref, wg_ref, wu_ref, wd_ref, ys_ref):
    j = pl.program_id(0)

    @pl.when(j < nv_ref[0])
    def _():
        ys_ref[...] = _swiglu_block(xs_ref[...].astype(bf16), wg_ref[0], wu_ref[0], wd_ref[0])

    @pl.when(j >= nv_ref[0])
    def _():
        ys_ref[...] = jnp.zeros(ys_ref.shape, f32)


def _experts(block_e, n_valid, xs, wg, wu, wd):
    rows = xs.shape[0]
    grid_spec = pltpu.PrefetchScalarGridSpec(
        num_scalar_prefetch=2,
        grid=(rows // EXPERT_ROWS,),
        in_specs=[
            pl.BlockSpec((EXPERT_ROWS, D_MODEL), lambda j, be, nv: (jnp.minimum(j, nv[0] - 1), 0)),
            pl.BlockSpec((1, D_MODEL, EXPERT_DIM), lambda j, be, nv: (be[j], 0, 0)),
            pl.BlockSpec((1, D_MODEL, EXPERT_DIM), lambda j, be, nv: (be[j], 0, 0)),
            pl.BlockSpec((1, EXPERT_DIM, D_MODEL), lambda j, be, nv: (be[j], 0, 0)),
        ],
        out_specs=pl.BlockSpec((EXPERT_ROWS, D_MODEL), lambda j, be, nv: (j, 0)),
    )
    return pl.pallas_call(
        _expert_kernel,
        grid_spec=grid_spec,
        out_shape=jax.ShapeDtypeStruct((rows, D_MODEL), f32),
        compiler_params=_cparams("arbitrary"),
        name="moe_experts",
    )(block_e, n_valid, xs, wg, wu, wd)


def _combine_kernel(cnt_ref, off_ref, row_ref, total_ref, slots_ref, x_ref, route_ref, ys_ref,
                    o_ref, yc_scr, sems):
    i = pl.program_id(0)
    n = pl.num_programs(0)
    buf = i % 2

    def fetch(tile, b):
        for needed, local, row, size in _expert_runs(cnt_ref, off_ref, row_ref, tile):
            @pl.when(needed)
            def _():
                pltpu.make_async_copy(ys_ref.at[pl.ds(row, size)], yc_scr.at[b, pl.ds(local, size)],
                                      sems.at[b]).start()

    @pl.when(i == 0)
    def _():
        yc_scr[...] = jnp.zeros(yc_scr.shape, f32)
        fetch(0, 0)

    @pl.when(i + 1 < n)
    def _():
        fetch(i + 1, 1 - buf)

    for needed, _, size in _pieces(total_ref[i], 2 * MOE_TILE):
        @pl.when(needed)
        def _():
            pltpu.make_async_copy(ys_ref.at[pl.ds(0, size)], yc_scr.at[buf, pl.ds(0, size)],
                                  sems.at[buf]).wait()
    y = yc_scr[buf]
    y_hi = y.astype(bf16)
    y_lo = (y - y_hi.astype(f32)).astype(bf16)
    y_parts = jnp.concatenate([y_hi, y_lo], axis=0)
    place = _iota((MOE_TILE, 2 * TILE_SLOTS), 1)
    route = route_ref[...]
    out = x_ref[...]
    for k, w_lane in ((0, ROUTE_W1), (1, ROUTE_W2)):
        slot = slots_ref[:, k:k + 1]
        pick = jnp.where((place == slot) | (place == slot + TILE_SLOTS), 1.0, 0.0).astype(bf16)
        out = out + route[:, w_lane:w_lane + 1] * jnp.dot(pick, y_parts,
                                                           preferred_element_type=f32)
    o_ref[...] = out


def _combine(x2, route, slots, cnt, off, row, total, ys):
    T = x2.shape[0]
    grid_spec = pltpu.PrefetchScalarGridSpec(
        num_scalar_prefetch=4,
        grid=(T // MOE_TILE,),
        in_specs=[
            pl.BlockSpec((MOE_TILE, 2), lambda t, *_: (t, 0)),
            pl.BlockSpec((MOE_TILE, D_MODEL), lambda t, *_: (t, 0)),
            pl.BlockSpec((MOE_TILE, LANES), lambda t, *_: (t, 0)),
            pl.BlockSpec(memory_space=pl.ANY),
        ],
        out_specs=pl.BlockSpec((MOE_TILE, D_MODEL), lambda t, *_: (t, 0)),
        scratch_shapes=[pltpu.VMEM((2, TILE_SLOTS, D_MODEL), f32), pltpu.SemaphoreType.DMA((2,))],
    )
    return pl.pallas_call(
        _combine_kernel,
        grid_spec=grid_spec,
        out_shape=jax.ShapeDtypeStruct((T, D_MODEL), f32),
        compiler_params=_cparams("arbitrary"),
        name="moe_combine",
    )(cnt, off, row, total, slots, x2, route, ys)


def _moe(x2, h2, route, wg, wu, wd):
    T = x2.shape[0]
    nt = T // MOE_TILE
    e12 = route[:, ROUTE_E1:ROUTE_E2 + 1].astype(jnp.int32)
    r12 = route[:, ROUTE_R1:ROUTE_R2 + 1].astype(jnp.int32)
    chosen = (e12[:, :, None] == jnp.arange(N_EXPERTS, dtype=jnp.int32)).astype(jnp.int32)
    cnt = chosen.sum(axis=1).reshape(nt, MOE_TILE, N_EXPERTS).sum(axis=1)
    run = (cnt + RUN_ALIGN - 1) // RUN_ALIGN * RUN_ALIGN
    off = jnp.cumsum(run, axis=1) - run
    before = jnp.cumsum(run, axis=0) - run
    padded = (run.sum(axis=0) + EXPERT_ROWS - 1) // EXPERT_ROWS * EXPERT_ROWS
    pends = jnp.cumsum(padded)
    row = (pends - padded)[None, :] + before
    n_blocks = -(-(2 * T + nt * N_EXPERTS * (RUN_ALIGN - 1)) // EXPERT_ROWS) + N_EXPERTS
    block_start = jnp.arange(n_blocks, dtype=jnp.int32) * EXPERT_ROWS
    block_e = jnp.minimum((pends[None, :] <= block_start[:, None]).sum(axis=1),
                          N_EXPERTS - 1).astype(jnp.int32)
    n_valid = (pends[-1:] // EXPERT_ROWS).astype(jnp.int32)
    first_rank = jnp.cumsum(cnt, axis=0) - cnt
    shift = jnp.repeat(off - first_rank, MOE_TILE, axis=0)
    slots = r12 + jnp.take_along_axis(shift, e12, axis=1)
    slots_t = slots.reshape(nt, MOE_TILE, 2).transpose(0, 2, 1)
    flat = lambda a: a.reshape(-1).astype(jnp.int32)
    tables = (flat(run), flat(off), flat(row), flat(run.sum(axis=1)))
    xs = _dispatch(h2, slots_t, *tables, pends.astype(jnp.int32), n_blocks * EXPERT_ROWS)
    ys = _experts(block_e, n_valid, xs, wg, wu, wd)
    return _combine(x2, route, slots, *tables, ys)


def _pad_cols(w, n):
    return jnp.pad(w, ((0, 0), (0, n - w.shape[1])))


def _layer_weights(w_in, c_w_uq, c_w_ukv):
    gate_end = N_BRANCH * D_MODEL
    a_end = gate_end + 2 * WIDTH
    b_end = a_end + 3 * WIDTH
    cq = w_in[:, b_end:b_end + C_Q_RANK]
    ckv = w_in[:, b_end + C_Q_RANK:b_end + C_Q_RANK + C_KV_RANK]
    kr = w_in[:, b_end + C_Q_RANK + C_KV_RANK:b_end + C_Q_RANK + C_KV_RANK + C_ROPE_DIM]
    d_start = b_end + C_Q_RANK + C_KV_RANK + C_ROPE_DIM
    kr_block = jnp.pad(kr, ((0, 0), (C_NOPE_DIM, C_HEAD_PAD - C_QK_DIM)))
    w = jnp.concatenate([w_in[:, :b_end], _pad_cols(cq, WIDTH), ckv, kr_block, w_in[:, d_start:]],
                        axis=1).astype(bf16)
    wuq = c_w_uq.reshape(C_Q_RANK, C_HEADS, C_QK_DIM)
    wuq = jnp.pad(wuq, ((0, WIDTH - C_Q_RANK), (0, 0), (0, C_HEAD_PAD - C_QK_DIM)))
    wuq = wuq.reshape(WIDTH, C_HEADS * C_HEAD_PAD).astype(bf16)
    wukv = c_w_ukv.reshape(C_KV_RANK, C_HEADS, C_NOPE_DIM + C_V_DIM)
    wuk = jnp.pad(wukv[:, :, :C_NOPE_DIM], ((0, 0), (0, 0), (0, C_HEAD_PAD - C_NOPE_DIM)))
    wuk = wuk.reshape(C_KV_RANK, C_HEADS * C_HEAD_PAD).astype(bf16)
    wuv = wukv[:, :, C_NOPE_DIM:].reshape(C_KV_RANK, C_HEADS * C_V_DIM).astype(bf16)
    return w, wuq, wuk, wuv


def _rotation_constants():
    ch = jnp.arange(WIDTH)
    gsum_b = ((ch[:, None] // B_HEAD_DIM) == (ch[None, :] // B_HEAD_DIM)).astype(f32) / B_HEAD_DIM
    half_b = B_ROT_DIM // 2
    src, dst = ch[:, None], ch[None, :]
    same = (src // B_HEAD_DIM) == (dst // B_HEAD_DIM)
    ds_, dd = src % B_HEAD_DIM, dst % B_HEAD_DIM
    rot_b = jnp.where(same & (dd < half_b) & (ds_ == dd + half_b), -1.0,
                      jnp.where(same & (dd >= half_b) & (dd < B_ROT_DIM) & (ds_ == dd - half_b), 1.0, 0.0))
    lc = jnp.arange(C_HEAD_PAD)
    real = lc < C_QK_DIM
    gsum_c = (real[:, None] & real[None, :]).astype(f32) / C_QK_DIM
    half_c = C_ROPE_DIM // 2
    s, d = lc[:, None], lc[None, :]
    lo = (d >= C_NOPE_DIM) & (d < C_NOPE_DIM + half_c)
    hi = (d >= C_NOPE_DIM + half_c) & (d < C_QK_DIM)
    rot_c = jnp.where(lo & (s == d + half_c), -1.0, jnp.where(hi & (s == d - half_c), 1.0, 0.0))
    slot = jnp.arange(ATT_HEADS * HEAD_SLOT)
    place_b = ((slot[None, :] // HEAD_SLOT == ch[:, None] // B_HEAD_DIM)
               & (slot[None, :] % HEAD_SLOT == ch[:, None] % B_HEAD_DIM))
    return (gsum_b.astype(bf16), rot_b.astype(bf16), place_b.astype(bf16),
            gsum_c.astype(bf16), rot_c.astype(bf16))


def kernel(x, positions, attn_norm, ffn_norm, w_in, a_ln_g, a_ln_b, a_ws, a_bs, b_qn, b_kn, c_cq_norm, c_ckv_norm, c_w_uq, c_w_ukv, c_qn, c_kn, d_conv_w, d_conv_b, d_ln_g, d_ln_b, w_branch, w_out, ffn_w_gate, ffn_w_up, ffn_w_down, moe_router, moe_router_b, moe_w_gate, moe_w_up, moe_w_down):
    B, S, _ = x.shape
    T = B * S
    assert S % ATT_BLOCK == 0 and T % EXPERT_ROWS == 0
    x2 = x.reshape(T, D_MODEL)
    cos_b, sin_b, cos_c, sin_c = _rope_tables(positions)
    gsum_b, rot_b, place_b, gsum_c, rot_c = _rotation_constants()
    row = lambda v: v.reshape(1, -1)

    for l in range(DEPTH):
        w, wuq, wuk, wuv = _layer_weights(w_in[l], c_w_uq[l], c_w_ukv[l])
        proj = _inproj(x2, row(attn_norm[l]), w)

        o_a = _sgu(proj, row(a_ln_g[l]), row(a_ln_b[l]), a_ws[l], a_bs[l].T)

        qb, kb, vtb = _moba_prep(proj, B, S, cos_b, sin_b,
                                 row(jnp.tile(b_qn[l], B_HEADS)), row(jnp.tile(b_kn[l], B_HEADS)),
                                 gsum_b, rot_b, place_b)
        o_b = _attention(qb, kb, vtb, B, S, "moba_attention")

        pad_gain = lambda g, n: row(jnp.pad(g, (0, n - g.shape[0])))
        qc, kc, vtc = _mla_prep(proj, B, S, cos_c, sin_c, pad_gain(c_cq_norm[l], WIDTH),
                                row(c_ckv_norm[l]), wuq, wuk, wuv,
                                pad_gain(c_qn[l], C_HEAD_PAD), pad_gain(c_kn[l], C_HEAD_PAD),
                                gsum_c, rot_c)
        o_c = _attention(qc, kc, vtc, B, S, "mla_attention")

        o_d = _conformer(proj, B, S, d_conv_w[l], row(d_conv_b[l]), row(d_ln_g[l]), row(d_ln_b[l]))

        merge_args = (x2, proj, o_a, o_b, o_c, o_d, w_branch[l].astype(bf16),
                      w_out[l].astype(bf16), row(ffn_norm[l]))
        i = l // 2
        if l % 2 == 0:
            x2, h2 = _merge(*merge_args)
            x2 = _ffn(x2, h2, ffn_w_gate[i].astype(bf16), ffn_w_up[i].astype(bf16),
                      ffn_w_down[i].astype(bf16))
        else:
            router = (_pad_cols(moe_router[i], LANES), _pad_cols(row(moe_router_b[i]), LANES))
            x2, h2, route = _merge(*merge_args, router=router)
            x2 = _moe(x2, h2, route, moe_w_gate[i].astype(bf16),
                      moe_w_up[i].astype(bf16), moe_w_down[i].astype(bf16))
    return x2.reshape(B, S, D_MODEL)
```

```python
import functools

import jax
import jax.numpy as jnp
from jax import lax
from jax.experimental import pallas as pl
from jax.experimental.pallas import tpu as pltpu

D_MODEL = 1024
DEPTH = 4
N_BRANCH = 4
ROPE_THETA = 500000.0
NORM_EPS = 1e-6
NEG_INF = -1e30

WIDTH = 256
A_GROUPS = 4
A_GROUP_DIM = 64
A_CHUNK = 128
B_HEADS = 4
B_HEAD_DIM = 64
B_ROT_DIM = 16
MOBA_BLOCK = 256
MOBA_TOPK = 3
C_HEADS = 4
C_Q_RANK = 192
C_KV_RANK = 128
C_NOPE_DIM = 64
C_ROPE_DIM = 32
C_V_DIM = 64
C_QK_DIM = C_NOPE_DIM + C_ROPE_DIM
C_HEAD_PAD = 128
D_CONV = 31
FFN_DIM = 2816
N_EXPERTS = 8
EXPERT_DIM = 1408
EXPERT_ROWS = 512

ATT_BLOCK = 256
ATT_QBLOCK = 512
ATT_HEADS = 4
HEAD_SLOT = 128
V_AUG = C_V_DIM + 16
LOG2E = 1.4426950408889634
SM_ROWS = 32
LANES = 128
SUBLANES = 8
CONV_HALO = 32
CONV_ROWS = 64
LN_ROWS = 128

GATE_COLS = N_BRANCH * D_MODEL
COL_AU = 0
COL_AV = 256
COL_BQ = 512
COL_BK = 768
COL_BV = 1024
COL_CQ = 1280
COL_CKV = 1536
COL_KR = 1664
COL_DA = 1792
COL_DG = 2048
PROJ_COLS = 2304

VMEM_LIMIT = 56 * 1024 * 1024

bf16 = jnp.bfloat16
f32 = jnp.float32


def _cparams(*sem):
    return pltpu.CompilerParams(dimension_semantics=sem, vmem_limit_bytes=VMEM_LIMIT)


def _rms(x, n):
    return x * lax.rsqrt(jnp.sum(x * x, axis=-1, keepdims=True) * (1.0 / n) + NORM_EPS)


def _layernorm(x, g, b):
    mu = jnp.mean(x, axis=-1, keepdims=True)
    xc = x - mu
    return xc * lax.rsqrt(jnp.mean(xc * xc, axis=-1, keepdims=True) + NORM_EPS) * g + b


def _sigmoid(x):
    return 0.5 * jnp.tanh(0.5 * x) + 0.5


def _iota(shape, dim):
    return lax.broadcasted_iota(jnp.int32, shape, dim)


def _rope_table_kernel(pos_ref, inv_ref, eb_ref, ec_ref, cb_ref, sb_ref, cc_ref, sc_ref):
    ang = pos_ref[0].astype(f32) * inv_ref[...]
    c = jnp.cos(ang)
    s = jnp.sin(ang)
    hi = lax.Precision.HIGHEST
    cb_ref[0] = jnp.dot(c, eb_ref[...], precision=hi, preferred_element_type=f32)
    sb_ref[0] = jnp.dot(s, eb_ref[...], precision=hi, preferred_element_type=f32)
    cc_ref[0] = jnp.dot(c, ec_ref[...], precision=hi, preferred_element_type=f32)
    sc_ref[0] = jnp.dot(s, ec_ref[...], precision=hi, preferred_element_type=f32)


def _rope_tables(positions):
    B, S = positions.shape
    R = min(512, S)
    nb = B_ROT_DIM // 2
    nc = C_ROPE_DIM // 2
    inv_b = f32(ROPE_THETA) ** (-jnp.arange(0, B_ROT_DIM, 2, dtype=f32) / B_ROT_DIM)
    inv_c = f32(ROPE_THETA) ** (-jnp.arange(0, C_ROPE_DIM, 2, dtype=f32) / C_ROPE_DIM)
    inv = jnp.zeros((1, LANES), f32).at[0, :nb].set(inv_b).at[0, nb:nb + nc].set(inv_c)
    zero_lane = LANES - 1
    ch = jnp.arange(WIDTH)
    d = ch % B_HEAD_DIM
    src_b = jnp.where(d < B_ROT_DIM, d % nb, zero_lane)
    eb = (jnp.arange(LANES)[:, None] == src_b[None, :]).astype(f32)
    dc = jnp.arange(C_HEAD_PAD)
    in_rope = (dc >= C_NOPE_DIM) & (dc < C_QK_DIM)
    src_c = jnp.where(in_rope, nb + (dc - C_NOPE_DIM) % nc, zero_lane)
    ec = (jnp.arange(LANES)[:, None] == src_c[None, :]).astype(f32)
    pos3 = positions.reshape(B, S, 1)
    out = pl.pallas_call(
        _rope_table_kernel,
        grid=(B, S // R),
        in_specs=[
            pl.BlockSpec((1, R, 1), lambda b, r: (b, r, 0)),
            pl.BlockSpec((1, LANES), lambda b, r: (0, 0)),
            pl.BlockSpec((LANES, WIDTH), lambda b, r: (0, 0)),
            pl.BlockSpec((LANES, C_HEAD_PAD), lambda b, r: (0, 0)),
        ],
        out_specs=[
            pl.BlockSpec((1, R, WIDTH), lambda b, r: (b, r, 0)),
            pl.BlockSpec((1, R, WIDTH), lambda b, r: (b, r, 0)),
            pl.BlockSpec((1, R, C_HEAD_PAD), lambda b, r: (b, r, 0)),
            pl.BlockSpec((1, R, C_HEAD_PAD), lambda b, r: (b, r, 0)),
        ],
        out_shape=[
            jax.ShapeDtypeStruct((B, S, WIDTH), f32),
            jax.ShapeDtypeStruct((B, S, WIDTH), f32),
            jax.ShapeDtypeStruct((B, S, C_HEAD_PAD), f32),
            jax.ShapeDtypeStruct((B, S, C_HEAD_PAD), f32),
        ],
        compiler_params=_cparams("parallel", "parallel"),
        name="rope_tables",
    )(pos3, inv, eb, ec)
    return out


PROJ_CHUNKS = ((0, COL_CQ), (COL_CQ, PROJ_COLS))


def _inproj_kernel(x_ref, g_ref, w_ref, o_ref):
    h = (_rms(x_ref[...], D_MODEL) * g_ref[...]).astype(bf16)
    for lo, hi in PROJ_CHUNKS:
        o_ref[:, lo:hi] = jnp.dot(h, w_ref[:, lo:hi], preferred_element_type=f32).astype(bf16)


def _inproj(x2, g, w):
    T = x2.shape[0]
    tm = min(512, T)
    return pl.pallas_call(
        _inproj_kernel,
        grid=(T // tm,),
        in_specs=[
            pl.BlockSpec((tm, D_MODEL), lambda t: (t, 0)),
            pl.BlockSpec((1, D_MODEL), lambda t: (0, 0)),
            pl.BlockSpec((D_MODEL, PROJ_COLS), lambda t: (0, 0)),
        ],
        out_specs=pl.BlockSpec((tm, PROJ_COLS), lambda t: (t, 0)),
        out_shape=jax.ShapeDtypeStruct((T, PROJ_COLS), bf16),
        compiler_params=_cparams("parallel"),
        name="inproj",
    )(x2, g, w)


def _sgu_kernel(u_ref, v_ref, g_ref, b_ref, ws_ref, bst_ref, o_ref, *, chunks):
    row = _iota((A_CHUNK, A_CHUNK), 0)
    col = _iota((A_CHUNK, A_CHUNK), 1)
    causal = col <= row
    group = _iota((A_CHUNK, WIDTH), 1) // A_GROUP_DIM
    wm = [jnp.where(causal, ws_ref[g], 0.0).astype(bf16) for g in range(A_GROUPS)]
    bias = jnp.zeros((A_CHUNK, WIDTH), f32)
    for g in range(A_GROUPS):
        bias = jnp.where(group == g, bst_ref[:, g:g + 1], bias)
    for c in range(chunks):
        rows = slice(c * A_CHUNK, (c + 1) * A_CHUNK)
        vn = _layernorm(v_ref[rows, :].astype(f32), g_ref[...], b_ref[...]).astype(bf16)
        z = bias
        for g in range(A_GROUPS):
            zg = jnp.dot(wm[g], vn, preferred_element_type=f32)
            z = z + jnp.where(group == g, zg, 0.0)
        o_ref[rows, :] = (u_ref[rows, :].astype(f32) * z).astype(bf16)


def _sgu(proj, ln_g, ln_b, ws, bs_t):
    T = proj.shape[0]
    ta = min(512, T)
    cu, cv = COL_AU // WIDTH, COL_AV // WIDTH
    return pl.pallas_call(
        functools.partial(_sgu_kernel, chunks=ta // A_CHUNK),
        grid=(T // ta,),
        in_specs=[
            pl.BlockSpec((ta, WIDTH), lambda t: (t, cu)),
            pl.BlockSpec((ta, WIDTH), lambda t: (t, cv)),
            pl.BlockSpec((1, WIDTH), lambda t: (0, 0)),
            pl.BlockSpec((1, WIDTH), lambda t: (0, 0)),
            pl.BlockSpec((A_GROUPS, A_CHUNK, A_CHUNK), lambda t: (0, 0, 0)),
            pl.BlockSpec((A_CHUNK, A_GROUPS), lambda t: (0, 0)),
        ],
        out_specs=pl.BlockSpec((ta, WIDTH), lambda t: (t, 0)),
        out_shape=jax.ShapeDtypeStruct((T, WIDTH), bf16),
        compiler_params=_cparams("parallel"),
        name="sgu",
    )(proj, proj, ln_g, ln_b, ws, bs_t)


def _conv_kernel(a_ref, g_ref, w_ref, cb_ref, lg_ref, lb_ref, o_ref, z_scr, *, seq):
    first = CONV_HALO - (D_CONV - 1)
    z_scr[0, 0:CONV_HALO, :] = jnp.zeros((CONV_HALO, WIDTH), f32)
    z_scr[0, CONV_HALO + seq:, :] = jnp.zeros((CONV_ROWS, WIDTH), f32)

    def glu(c, carry):
        base = pl.multiple_of(c * LN_ROWS, LN_ROWS)
        rows = pl.ds(base, LN_ROWS)
        z_scr[0, pl.ds(base + CONV_HALO, LN_ROWS), :] = (
            a_ref[rows, :].astype(f32) * _sigmoid(g_ref[rows, :].astype(f32)))
        return carry

    lax.fori_loop(0, seq // LN_ROWS, glu, 0)

    def shift(c, carry):
        base = pl.multiple_of(c * CONV_ROWS, CONV_ROWS)
        win = z_scr[0, pl.ds(base, CONV_ROWS + SUBLANES), :]
        for r in range(1, SUBLANES):
            rolled = pltpu.roll(win, shift=CONV_ROWS + SUBLANES - r, axis=0)
            z_scr[r, pl.ds(base, CONV_ROWS), :] = rolled[:CONV_ROWS]
        return carry

    lax.fori_loop(0, seq // CONV_ROWS + 1, shift, 0)

    def taps(c, carry):
        base = pl.multiple_of(c * CONV_ROWS, CONV_ROWS)
        acc = jnp.zeros((CONV_ROWS, WIDTH), f32) + cb_ref[...]
        for k in range(D_CONV):
            off = first + k
            lo = pl.multiple_of(base + off // SUBLANES * SUBLANES, SUBLANES)
            acc = acc + w_ref[k:k + 1, :] * z_scr[off % SUBLANES, pl.ds(lo, CONV_ROWS), :]
        z_scr[1, pl.ds(base, CONV_ROWS), :] = acc
        return carry

    lax.fori_loop(0, seq // CONV_ROWS, taps, 0)

    def norm(c, carry):
        base = pl.multiple_of(c * LN_ROWS, LN_ROWS)
        y = _layernorm(z_scr[1, pl.ds(base, LN_ROWS), :], lg_ref[...], lb_ref[...])
        o_ref[pl.ds(base, LN_ROWS), :] = (y * _sigmoid(y)).astype(bf16)
        return carry

    lax.fori_loop(0, seq // LN_ROWS, norm, 0, unroll=2)


def _conformer(proj, B, S, conv_w, conv_b, ln_g, ln_b):
    ca, cg = COL_DA // WIDTH, COL_DG // WIDTH
    return pl.pallas_call(
        functools.partial(_conv_kernel, seq=S),
        grid=(B,),
        in_specs=[
            pl.BlockSpec((S, WIDTH), lambda b: (b, ca)),
            pl.BlockSpec((S, WIDTH), lambda b: (b, cg)),
            pl.BlockSpec((D_CONV, WIDTH), lambda b: (0, 0)),
            pl.BlockSpec((1, WIDTH), lambda b: (0, 0)),
            pl.BlockSpec((1, WIDTH), lambda b: (0, 0)),
            pl.BlockSpec((1, WIDTH), lambda b: (0, 0)),
        ],
        out_specs=pl.BlockSpec((S, WIDTH), lambda b: (b, 0)),
        out_shape=jax.ShapeDtypeStruct((B * S, WIDTH), bf16),
        scratch_shapes=[pltpu.VMEM((SUBLANES, CONV_HALO + S + CONV_ROWS, WIDTH), f32)],
        compiler_params=_cparams("parallel"),
        name="conformer_conv",
    )(proj, proj, conv_w, conv_b, ln_g, ln_b)


def _value_rows(v_tok):
    vt = v_tok.T
    ones = jnp.ones((V_AUG - C_V_DIM, vt.shape[1]), f32)
    rows = []
    for h in range(ATT_HEADS):
        rows += [vt[h * C_V_DIM:(h + 1) * C_V_DIM], ones]
    return jnp.concatenate(rows, axis=0).astype(bf16)


def _moba_prep_kernel(q_ref, k_ref, v_ref, cos_ref, sin_ref, qg_ref, kg_ref, gsum_ref, rot_ref,
                      place_ref, qo_ref, ko_ref, vt_ref, kmean_scr, *, nblk):
    c = pl.program_id(1)

    @pl.when(c == 0)
    def _():
        kmean_scr[...] = jnp.zeros(kmean_scr.shape, f32)

    cos = cos_ref[0]
    sin = sin_ref[0]

    def norm_rope(x, gain):
        ss = jnp.dot((x * x).astype(bf16), gsum_ref[...], preferred_element_type=f32)
        xn = x * lax.rsqrt(ss + NORM_EPS) * gain
        xr = jnp.dot(xn.astype(bf16), rot_ref[...], preferred_element_type=f32)
        return xn * cos + xr * sin

    q = norm_rope(q_ref[...].astype(f32), qg_ref[...])
    k = norm_rope(k_ref[...].astype(f32), kg_ref[...])
    slot_lane = _iota((ATT_BLOCK, ATT_HEADS * HEAD_SLOT), 1) % HEAD_SLOT
    k_slots = jnp.dot(k.astype(bf16), place_ref[...], preferred_element_type=f32)
    ko_ref[0, 0] = jnp.where(slot_lane == B_HEAD_DIM + c, 1.0, k_slots).astype(bf16)
    vt_ref[0, 0] = _value_rows(v_ref[...].astype(f32))

    head = _iota((nblk, WIDTH), 1) // B_HEAD_DIM
    blk = _iota((nblk, ATT_BLOCK), 0)
    past = blk < c
    kmean = kmean_scr[...]
    zeros_lo = jnp.zeros((B_HEAD_DIM, ATT_BLOCK), f32)
    zeros_hi = jnp.zeros((HEAD_SLOT - B_HEAD_DIM - nblk, ATT_BLOCK), f32)
    bias_rows = []
    for h in range(B_HEADS):
        km = jnp.where(head == h, kmean, 0.0)
        sc = lax.dot_general(km, q, (((1,), (1,)), ((), ())), precision=lax.Precision.HIGHEST,
                             preferred_element_type=f32)
        sc = jnp.where(past, sc, NEG_INF)
        rank = jnp.zeros((nblk, ATT_BLOCK), jnp.int32)
        for j in range(nblk):
            other = sc[j:j + 1, :]
            ahead = (other > sc) | ((other == sc) & (j < blk))
            rank = rank + ahead.astype(jnp.int32)
        visible = (past & (rank < MOBA_TOPK)) | (blk == c)
        bias_rows += [zeros_lo, jnp.where(visible, 0.0, NEG_INF), zeros_hi]
    bias = jnp.concatenate(bias_rows, axis=0)
    q_slots = jnp.dot((q * (B_HEAD_DIM ** -0.5 * LOG2E)).astype(bf16), place_ref[...],
                      preferred_element_type=f32)
    qo_ref[0] = (q_slots.T + bias).astype(bf16)

    kmean_scr[pl.ds(c, 1), :] = jnp.mean(k, axis=0, keepdims=True)


def _moba_prep(proj, B, S, cos_b, sin_b, qg, kg, gsum, rot, place):
    nblk = S // ATT_BLOCK
    assert nblk <= HEAD_SLOT - B_HEAD_DIM
    HD = ATT_HEADS * HEAD_SLOT
    cq, ck, cv = COL_BQ // WIDTH, COL_BK // WIDTH, COL_BV // WIDTH
    tile = lambda col: pl.BlockSpec((ATT_BLOCK, WIDTH), lambda b, c: (b * nblk + c, col))
    table = pl.BlockSpec((1, ATT_BLOCK, WIDTH), lambda b, c: (b, c, 0))
    const = lambda shape: pl.BlockSpec(shape, lambda b, c: (0,) * len(shape))
    return pl.pallas_call(
        functools.partial(_moba_prep_kernel, nblk=nblk),
        grid=(B, nblk),
        in_specs=[tile(cq), tile(ck), tile(cv), table, table,
                  const((1, WIDTH)), const((1, WIDTH)), const((WIDTH, WIDTH)), const((WIDTH, WIDTH)),
                  const((WIDTH, HD))],
        out_specs=[
            pl.BlockSpec((1, HD, ATT_BLOCK), lambda b, c: (b, 0, c)),
            pl.BlockSpec((1, 1, ATT_BLOCK, HD), lambda b, c: (b, c, 0, 0)),
            pl.BlockSpec((1, 1, ATT_HEADS * V_AUG, ATT_BLOCK), lambda b, c: (b, c, 0, 0)),
        ],
        out_shape=[
            jax.ShapeDtypeStruct((B, HD, S), bf16),
            jax.ShapeDtypeStruct((B, nblk, ATT_BLOCK, HD), bf16),
            jax.ShapeDtypeStruct((B, nblk, ATT_HEADS * V_AUG, ATT_BLOCK), bf16),
        ],
        scratch_shapes=[pltpu.VMEM((nblk, WIDTH), f32)],
        compiler_params=_cparams("parallel", "arbitrary"),
        name="moba_prep",
    )(proj, proj, proj, cos_b, sin_b, qg, kg, gsum, rot, place)


def _mla_prep_kernel(cq_ref, ckv_ref, kr_ref, cos_ref, sin_ref, cqg_ref, ckvg_ref, wuq_ref, wuk_ref,
                     wuv_ref, qg_ref, kg_ref, gsum_ref, rot_ref, qo_ref, ko_ref, vt_ref):
    cqn = (_rms(cq_ref[...].astype(f32), C_Q_RANK) * cqg_ref[...]).astype(bf16)
    ckvn = (_rms(ckv_ref[...].astype(f32), C_KV_RANK) * ckvg_ref[...]).astype(bf16)
    q = jnp.dot(cqn, wuq_ref[...], preferred_element_type=f32)
    k = jnp.dot(ckvn, wuk_ref[...], preferred_element_type=f32)
    v = jnp.dot(ckvn, wuv_ref[...], preferred_element_type=f32)
    kr = kr_ref[...].astype(f32)
    cos = cos_ref[0]
    sin = sin_ref[0]

    def norm_rope(x, gain):
        ss = jnp.dot((x * x).astype(bf16), gsum_ref[...], preferred_element_type=f32)
        xn = x * lax.rsqrt(ss + NORM_EPS) * gain
        xr = jnp.dot(xn.astype(bf16), rot_ref[...], preferred_element_type=f32)
        return xn * cos + xr * sin

    for h in range(C_HEADS):
        sl = slice(h * C_HEAD_PAD, (h + 1) * C_HEAD_PAD)
        qh = norm_rope(q[:, sl], qg_ref[...])
        kh = norm_rope(k[:, sl] + kr, kg_ref[...])
        qo_ref[0, sl, :] = (qh * (C_QK_DIM ** -0.5 * LOG2E)).T.astype(bf16)
        ko_ref[0, 0, :, sl] = kh.astype(bf16)
    vt_ref[0, 0] = _value_rows(v)


def _mla_prep(proj, B, S, cos_c, sin_c, cqg, ckvg, wuq, wuk, wuv, qg, kg, gsum, rot):
    nblk = S // ATT_BLOCK
    HD = C_HEADS * C_HEAD_PAD
    table = pl.BlockSpec((1, ATT_BLOCK, C_HEAD_PAD), lambda b, c: (b, c, 0))
    const = lambda shape: pl.BlockSpec(shape, lambda b, c: (0,) * len(shape))
    return pl.pallas_call(
        _mla_prep_kernel,
        grid=(B, nblk),
        in_specs=[
            pl.BlockSpec((ATT_BLOCK, WIDTH), lambda b, c: (b * nblk + c, COL_CQ // WIDTH)),
            pl.BlockSpec((ATT_BLOCK, LANES), lambda b, c: (b * nblk + c, COL_CKV // LANES)),
            pl.BlockSpec((ATT_BLOCK, LANES), lambda b, c: (b * nblk + c, COL_KR // LANES)),
            table, table,
            const((1, WIDTH)), const((1, C_KV_RANK)), const((WIDTH, HD)), const((C_KV_RANK, HD)),
            const((C_KV_RANK, WIDTH)), const((1, C_HEAD_PAD)), const((1, C_HEAD_PAD)),
            const((C_HEAD_PAD, C_HEAD_PAD)), const((C_HEAD_PAD, C_HEAD_PAD)),
        ],
        out_specs=[
            pl.BlockSpec((1, HD, ATT_BLOCK), lambda b, c: (b, 0, c)),
            pl.BlockSpec((1, 1, ATT_BLOCK, HD), lambda b, c: (b, c, 0, 0)),
            pl.BlockSpec((1, 1, ATT_HEADS * V_AUG, ATT_BLOCK), lambda b, c: (b, c, 0, 0)),
        ],
        out_shape=[
            jax.ShapeDtypeStruct((B, HD, S), bf16),
            jax.ShapeDtypeStruct((B, nblk, ATT_BLOCK, HD), bf16),
            jax.ShapeDtypeStruct((B, nblk, ATT_HEADS * V_AUG, ATT_BLOCK), bf16),
        ],
        compiler_params=_cparams("parallel", "parallel"),
        name="mla_prep",
    )(proj, proj, proj, cos_c, sin_c, cqg, ckvg, wuq, wuk, wuv, qg, kg, gsum, rot)


def _attn_kernel(q_ref, k_ref, vt_ref, o_ref, *scratch):
    H = ATT_HEADS
    s_scr, p_scr, a_scr = ([scratch[(2 * n + par) * H:(2 * n + par + 1) * H] for par in range(2)]
                           for n in range(3))
    acc_scr, m_scr = (scratch[n * H:(n + 1) * H] for n in (6, 7))
    out_scr = scratch[8 * H]
    top_scr = [scratch[8 * H + 1 + par * H:8 * H + 1 + (par + 1) * H] for par in range(2)]
    pairs = pl.program_id(1) * (ATT_QBLOCK // ATT_BLOCK // 2)
    chunks = ATT_BLOCK // SM_ROWS

    def scores(j, par, h):
        slot = slice(h * HEAD_SLOT, (h + 1) * HEAD_SLOT)
        s = jnp.dot(k_ref[0, j, :, slot], q_ref[0, slot, :],
                    preferred_element_type=f32)
        s_scr[par][h][...] = s
        top_scr[par][h][...] = jnp.max(s, axis=0, keepdims=True)

    def softmax(par, h, diag_offset=None):
        def rows(r):
            s = s_scr[par][h][r * SM_ROWS:(r + 1) * SM_ROWS, :]
            if diag_offset is not None:
                kpos = _iota((SM_ROWS, ATT_QBLOCK), 0) + (r * SM_ROWS + diag_offset)
                s = jnp.where(kpos <= _iota((SM_ROWS, ATT_QBLOCK), 1), s, NEG_INF)
            return s
        if diag_offset is None:
            top = top_scr[par][h][...]
        else:
            mx = rows(0)
            for r in range(1, chunks):
                mx = jnp.maximum(mx, rows(r))
            top = jnp.max(mx, axis=0, keepdims=True)
        m_old = m_scr[h][...]
        m_new = jnp.maximum(m_old, top)
        a_scr[par][h][...] = jnp.exp2(m_old - m_new)
        m_scr[h][...] = m_new
        for r in range(chunks):
            p_scr[par][h][r * SM_ROWS:(r + 1) * SM_ROWS, :] = (
                jnp.exp2((rows(r) - m_new).astype(bf16)))

    def values(j, par, h):
        pv = jnp.dot(vt_ref[0, j, h * V_AUG:(h + 1) * V_AUG, :], p_scr[par][h][...],
                     preferred_element_type=f32)
        acc_scr[h][...] = a_scr[par][h][...] * acc_scr[h][...] + pv

    for h in range(H):
        m_scr[h][...] = jnp.full(m_scr[h].shape, NEG_INF, f32)
        a_scr[1][h][...] = jnp.zeros(a_scr[1][h].shape, f32)
        acc_scr[h][...] = jnp.zeros(acc_scr[h].shape, f32)
        p_scr[1][h][...] = jnp.zeros(p_scr[1][h].shape, bf16)
        scores(0, 0, h)

    def step(b, par, diag_offset=None, last=False):
        for h in range(H):
            if not last:
                scores(b + 1, 1 - par, h)
            values(jnp.maximum(b - 1, 0), 1 - par, h)
            softmax(par, h, diag_offset)

    def past_pair(t, carry):
        step(2 * t, 0)
        step(2 * t + 1, 1)
        return carry

    lax.fori_loop(0, pairs, past_pair, 0)
    step(2 * pairs, 0, diag_offset=0)
    step(2 * pairs + 1, 1, diag_offset=ATT_BLOCK, last=True)

    for h in range(H):
        values(2 * pairs + 1, 1, h)
        a = acc_scr[h][...]
        out_scr[h * C_V_DIM:(h + 1) * C_V_DIM, :] = a[:C_V_DIM] / a[C_V_DIM:C_V_DIM + 1]
    o_ref[...] = out_scr[...].T.astype(bf16)


def _attention(q, k, vt, B, S, name):
    nblk = S // ATT_BLOCK
    nq = S // ATT_QBLOCK
    HD = ATT_HEADS * HEAD_SLOT
    return pl.pallas_call(
        _attn_kernel,
        grid=(B, nq),
        in_specs=[
            pl.BlockSpec((1, HD, ATT_QBLOCK), lambda b, i: (b, 0, i)),
            pl.BlockSpec((1, nblk, ATT_BLOCK, HD), lambda b, i: (b, 0, 0, 0)),
            pl.BlockSpec((1, nblk, ATT_HEADS * V_AUG, ATT_BLOCK), lambda b, i: (b, 0, 0, 0)),
        ],
        out_specs=pl.BlockSpec((ATT_QBLOCK, WIDTH), lambda b, i: (b * nq + i, 0)),
        out_shape=jax.ShapeDtypeStruct((B * S, WIDTH), bf16),
        scratch_shapes=(
            [pltpu.VMEM((ATT_BLOCK, ATT_QBLOCK), f32) for _ in range(2 * ATT_HEADS)]
            + [pltpu.VMEM((ATT_BLOCK, ATT_QBLOCK), bf16) for _ in range(2 * ATT_HEADS)]
            + [pltpu.VMEM((1, ATT_QBLOCK), f32) for _ in range(2 * ATT_HEADS)]
            + [pltpu.VMEM((V_AUG, ATT_QBLOCK), f32) for _ in range(ATT_HEADS)]
            + [pltpu.VMEM((1, ATT_QBLOCK), f32) for _ in range(ATT_HEADS)]
            + [pltpu.VMEM((WIDTH, ATT_QBLOCK), f32)]
            + [pltpu.VMEM((1, ATT_QBLOCK), f32) for _ in range(2 * ATT_HEADS)]),
        compiler_params=_cparams("parallel", "arbitrary"),
        name=name,
    )(q, k, vt)


ROUTE_E1, ROUTE_E2, ROUTE_W1, ROUTE_W2, ROUTE_R1, ROUTE_R2 = range(6)


def _route_rows(h, wr_ref, br_ref, route_ref, carry_scr):
    tm = h.shape[0]

    @pl.when(pl.program_id(0) == 0)
    def _():
        carry_scr[...] = jnp.zeros(carry_scr.shape, f32)

    lane = _iota((tm, LANES), 1)
    h_hi = h.astype(bf16)
    h_lo = (h - h_hi.astype(f32)).astype(bf16)
    w = wr_ref[...]
    w_hi = w.astype(bf16)
    w_lo = (w - w_hi.astype(f32)).astype(bf16)
    logits = (jnp.dot(h_hi, w_hi, preferred_element_type=f32)
              + (jnp.dot(h_hi, w_lo, preferred_element_type=f32)
                 + jnp.dot(h_lo, w_hi, preferred_element_type=f32))) + br_ref[...]
    logits = jnp.where(lane < N_EXPERTS, logits, NEG_INF)
    m1 = jnp.max(logits, axis=-1, keepdims=True)
    e1 = jnp.min(jnp.where(logits == m1, lane, LANES), axis=-1, keepdims=True)
    rest = jnp.where(lane == e1, NEG_INF, logits)
    m2 = jnp.max(rest, axis=-1, keepdims=True)
    e2 = jnp.min(jnp.where(rest == m2, lane, LANES), axis=-1, keepdims=True)
    ex = jnp.exp(m2 - m1)
    w1 = 1.0 / (1.0 + ex)
    w2 = ex / (1.0 + ex)
    chosen = ((lane == e1) | (lane == e2)).astype(f32)
    earlier = (_iota((tm, tm), 1) < _iota((tm, tm), 0)).astype(bf16)
    before = jnp.dot(earlier, chosen.astype(bf16), preferred_element_type=f32) + carry_scr[...]
    r1 = jnp.sum(jnp.where(lane == e1, before, 0.0), axis=-1, keepdims=True)
    r2 = jnp.sum(jnp.where(lane == e2, before, 0.0), axis=-1, keepdims=True)
    out = jnp.zeros((tm, LANES), f32)
    for idx, val in ((ROUTE_E1, e1.astype(f32)), (ROUTE_E2, e2.astype(f32)), (ROUTE_W1, w1),
                     (ROUTE_W2, w2), (ROUTE_R1, r1), (ROUTE_R2, r2)):
        out = jnp.where(lane == idx, val, out)
    route_ref[...] = out
    carry_scr[...] += jnp.sum(chosen, axis=0, keepdims=True)


def _merge_kernel(x_ref, ag_ref, wgate_ref, oa_ref, ob_ref, oc_ref, od_ref,
                  wb_ref, wo_ref, ng_ref, *rest, routed):
    branches = (oa_ref, ob_ref, oc_ref, od_ref)
    xo_ref, ho_ref = rest[2:4] if routed else rest
    tm = x_ref.shape[0]
    halves = [slice(r * (tm // 2), (r + 1) * (tm // 2)) for r in range(2)]
    gated = []
    for rows in halves:
        hin = (_rms(x_ref[rows, :], D_MODEL) * ag_ref[...]).astype(bf16)
        gates = [jnp.dot(hin, wgate_ref[:, n * D_MODEL:(n + 1) * D_MODEL],
                         preferred_element_type=f32) for n in range(N_BRANCH)]
        projected = [jnp.dot(o_ref[rows, :], wb_ref[n], preferred_element_type=f32)
                     for n, o_ref in enumerate(branches)]
        gated.append((gates, projected))
    hs = []
    for rows, (gates, projected) in zip(halves, gated):
        y = None
        for gate, br in zip(gates, projected):
            t = _sigmoid(gate) * br
            y = t if y is None else y + t
        x = x_ref[rows, :] + jnp.dot(y.astype(bf16), wo_ref[...], preferred_element_type=f32)
        xo_ref[rows, :] = x
        hs.append(_rms(x, D_MODEL) * ng_ref[...])
    if routed:
        wr_ref, br_ref, _, _, route_ref, carry_scr = rest
        h = jnp.concatenate(hs, axis=0)
        ho_ref[...] = h
        _route_rows(h, wr_ref, br_ref, route_ref, carry_scr)
    else:
        for rows, h in zip(halves, hs):
            ho_ref[rows, :] = h.astype(bf16)


def _merge(x2, ag, wgate, oa, ob, oc, od, wb, wo, ng, router=None):
    T = x2.shape[0]
    tm = min(512, T)
    routed = router is not None
    branch = pl.BlockSpec((tm, WIDTH), lambda t: (t, 0))
    rows = pl.BlockSpec((tm, D_MODEL), lambda t: (t, 0))
    resident = lambda shape: pl.BlockSpec(shape, lambda t: (0,) * len(shape),
                                          pipeline_mode=pl.Buffered(1))
    in_specs = [
        rows, pl.BlockSpec((1, D_MODEL), lambda t: (0, 0)), resident((D_MODEL, GATE_COLS)),
        branch, branch, branch, branch,
        resident((N_BRANCH, WIDTH, D_MODEL)), resident((D_MODEL, D_MODEL)),
        pl.BlockSpec((1, D_MODEL), lambda t: (0, 0)),
    ]
    args = [x2, ag, wgate, oa, ob, oc, od, wb, wo, ng]
    out_specs = [rows, rows]
    out_shape = [jax.ShapeDtypeStruct((T, D_MODEL), f32),
                 jax.ShapeDtypeStruct((T, D_MODEL), f32 if routed else bf16)]
    scratch = []
    if routed:
        in_specs += [pl.BlockSpec((D_MODEL, LANES), lambda t: (0, 0)),
                     pl.BlockSpec((1, LANES), lambda t: (0, 0))]
        args += list(router)
        out_specs += [pl.BlockSpec((tm, LANES), lambda t: (t, 0))]
        out_shape += [jax.ShapeDtypeStruct((T, LANES), f32)]
        scratch = [pltpu.VMEM((1, LANES), f32)]
    return pl.pallas_call(
        functools.partial(_merge_kernel, routed=routed),
        grid=(T // tm,),
        in_specs=in_specs,
        out_specs=out_specs,
        out_shape=out_shape,
        scratch_shapes=scratch,
        compiler_params=_cparams("arbitrary" if routed else "parallel"),
        name="merge_route" if routed else "merge",
    )(*args)


def _swiglu_block(h, wg, wu, wd):
    a = jnp.dot(h, wg, preferred_element_type=f32)
    b = jnp.dot(h, wu, preferred_element_type=f32)
    t = (a * _sigmoid(a) * b).astype(bf16)
    return jnp.dot(t, wd, preferred_element_type=f32)


FFN_CHUNKS = ((0, 1536), (1536, FFN_DIM))


def _ffn_kernel(x_ref, h_ref, wg_ref, wu_ref, wd_ref, o_ref):
    h = h_ref[...]
    acc = x_ref[...]
    for lo, hi in FFN_CHUNKS:
        acc = acc + _swiglu_block(h, wg_ref[:, lo:hi], wu_ref[:, lo:hi], wd_ref[lo:hi, :])
    o_ref[...] = acc


def _ffn(x2, h2, wg, wu, wd):
    T = x2.shape[0]
    tm = min(512, T)
    rows = pl.BlockSpec((tm, D_MODEL), lambda t: (t, 0))
    resident = lambda shape: pl.BlockSpec(shape, lambda t: (0, 0), pipeline_mode=pl.Buffered(1))
    return pl.pallas_call(
        _ffn_kernel,
        grid=(T // tm,),
        in_specs=[rows, rows, resident((D_MODEL, FFN_DIM)), resident((D_MODEL, FFN_DIM)),
                  resident((FFN_DIM, D_MODEL))],
        out_specs=rows,
        out_shape=jax.ShapeDtypeStruct((T, D_MODEL), f32),
        compiler_params=_cparams("parallel"),
        name="dense_swiglu",
    )(x2, h2, wg, wu, wd)


MOE_TILE = 256
RUN_ALIGN = SUBLANES
TILE_SLOTS = 2 * MOE_TILE + N_EXPERTS * RUN_ALIGN


def _pow2_pieces(limit):
    out, b = [], RUN_ALIGN
    while b <= limit:
        out.append(b)
        b *= 2
    return tuple(reversed(out))


def _pieces(value, limit):
    sizes = _pow2_pieces(limit)
    for b in sizes:
        done = value & (~(2 * b - 1) & (2 * sizes[0] - 1))
        yield (value & b) != 0, done, b


def _expert_runs(cnt_ref, off_ref, row_ref, tile):
    for e in range(N_EXPERTS):
        o = off_ref[tile * N_EXPERTS + e]
        d = row_ref[tile * N_EXPERTS + e]
        for needed, done, size in _pieces(cnt_ref[tile * N_EXPERTS + e], MOE_TILE):
            yield (needed, pl.multiple_of(o + done, RUN_ALIGN), pl.multiple_of(d + done, RUN_ALIGN),
                   size)


def _dispatch_kernel(cnt_ref, off_ref, row_ref, total_ref, pends_ref, slots_ref, h_ref, xs_ref,
                     xc_scr, zero_scr, sems, zero_sem, *, spare_blocks):
    i = pl.program_id(0)
    n = pl.num_programs(0)
    buf = i % 2

    def zero_fill(start):
        start = pl.multiple_of(start, EXPERT_ROWS)
        return pltpu.make_async_copy(zero_scr, xs_ref.at[pl.ds(start, EXPERT_ROWS)], zero_sem)

    @pl.when(i == 0)
    def _():
        zero_scr[...] = jnp.zeros(zero_scr.shape, f32)
        used = pends_ref[N_EXPERTS - 1]
        fills = [(jnp.maximum(pends_ref[e] - EXPERT_ROWS, 0), None) for e in range(N_EXPERTS)]
        fills += [(used + e * EXPERT_ROWS, used + e * EXPERT_ROWS < xs_ref.shape[0])
                  for e in range(spare_blocks)]
        for action in ("start", "wait"):
            for start, needed in fills:
                if needed is None:
                    getattr(zero_fill(start), action)()
                else:
                    @pl.when(needed)
                    def _():
                        getattr(zero_fill(start), action)()

    def tile_done(tile, b):
        for needed, _, size in _pieces(total_ref[tile], 2 * MOE_TILE):
            @pl.when(needed)
            def _():
                pltpu.make_async_copy(xc_scr.at[b, pl.ds(0, size)], xs_ref.at[pl.ds(0, size)],
                                      sems.at[b]).wait()

    @pl.when(i >= 2)
    def _():
        tile_done(i - 2, buf)

    place = _iota((TILE_SLOTS, MOE_TILE), 0)
    onehot = (place == slots_ref[0, 0:1, :]) | (place == slots_ref[0, 1:2, :])
    xc_scr[buf] = jnp.dot(jnp.where(onehot, 1.0, 0.0).astype(bf16), h_ref[...].astype(bf16),
                          preferred_element_type=f32)
    for needed, local, row, size in _expert_runs(cnt_ref, off_ref, row_ref, i):
        @pl.when(needed)
        def _():
            pltpu.make_async_copy(xc_scr.at[buf, pl.ds(local, size)], xs_ref.at[pl.ds(row, size)],
                                  sems.at[buf]).start()

    @pl.when(i == n - 1)
    def _():
        tile_done(i, buf)

        @pl.when(i >= 1)
        def _():
            tile_done(i - 1, 1 - buf)


def _dispatch(h2, slots_t, cnt, off, row, total, pends, rows):
    T = h2.shape[0]
    grid_spec = pltpu.PrefetchScalarGridSpec(
        num_scalar_prefetch=5,
        grid=(T // MOE_TILE,),
        in_specs=[
            pl.BlockSpec((1, 2, MOE_TILE), lambda t, *_: (t, 0, 0)),
            pl.BlockSpec((MOE_TILE, D_MODEL), lambda t, *_: (t, 0)),
        ],
        out_specs=pl.BlockSpec(memory_space=pl.ANY),
        scratch_shapes=[pltpu.VMEM((2, TILE_SLOTS, D_MODEL), f32),
                        pltpu.VMEM((EXPERT_ROWS, D_MODEL), f32),
                        pltpu.SemaphoreType.DMA((2,)), pltpu.SemaphoreType.DMA(())],
    )
    spare_blocks = rows // EXPERT_ROWS - 2 * T // EXPERT_ROWS
    return pl.pallas_call(
        functools.partial(_dispatch_kernel, spare_blocks=spare_blocks),
        grid_spec=grid_spec,
        out_shape=jax.ShapeDtypeStruct((rows, D_MODEL), f32),
        compiler_params=_cparams("arbitrary"),
        name="moe_dispatch",
    )(cnt, off, row, total, pends, slots_t, h2)


def _expert_kernel(be_ref, nv_ref, xs_ref, wg_ref, wu_ref, wd_ref, ys_ref):
    j = pl.program_id(0)

    @pl.when(j < nv_ref[0])
    def _():
        ys_ref[...] = _swiglu_block(xs_ref[...].astype(bf16), wg_ref[0], wu_ref[0], wd_ref[0])

    @pl.when(j >= nv_ref[0])
    def _():
        ys_ref[...] = jnp.zeros(ys_ref.shape, f32)


def _experts(block_e, n_valid, xs, wg, wu, wd):
    rows = xs.shape[0]
    grid_spec = pltpu.PrefetchScalarGridSpec(
        num_scalar_prefetch=2,
        grid=(rows // EXPERT_ROWS,),
        in_specs=[
            pl.BlockSpec((EXPERT_ROWS, D_MODEL), lambda j, be, nv: (jnp.minimum(j, nv[0] - 1), 0)),
            pl.BlockSpec((1, D_MODEL, EXPERT_DIM), lambda j, be, nv: (be[j], 0, 0)),
            pl.BlockSpec((1, D_MODEL, EXPERT_DIM), lambda j, be, nv: (be[j], 0, 0)),
            pl.BlockSpec((1, EXPERT_DIM, D_MODEL), lambda j, be, nv: (be[j], 0, 0)),
        ],
        out_specs=pl.BlockSpec((EXPERT_ROWS, D_MODEL), lambda j, be, nv: (j, 0)),
    )
    return pl.pallas_call(
        _expert_kernel,
        grid_spec=grid_spec,
        out_shape=jax.ShapeDtypeStruct((rows, D_MODEL), f32),
        compiler_params=_cparams("arbitrary"),
        name="moe_experts",
    )(block_e, n_valid, xs, wg, wu, wd)


def _combine_kernel(cnt_ref, off_ref, row_ref, total_ref, slots_ref, x_ref, route_ref, ys_ref,
                    o_ref, yc_scr, sems):
    i = pl.program_id(0)
    n = pl.num_programs(0)
    buf = i % 2

    def fetch(tile, b):
        for needed, local, row, size in _expert_runs(cnt_ref, off_ref, row_ref, tile):
            @pl.when(needed)
            def _():
                pltpu.make_async_copy(ys_ref.at[pl.ds(row, size)], yc_scr.at[b, pl.ds(local, size)],
                                      sems.at[b]).start()

    @pl.when(i == 0)
    def _():
        yc_scr[...] = jnp.zeros(yc_scr.shape, f32)
        fetch(0, 0)

    @pl.when(i + 1 < n)
    def _():
        fetch(i + 1, 1 - buf)

    for needed, _, size in _pieces(total_ref[i], 2 * MOE_TILE):
        @pl.when(needed)
        def _():
            pltpu.make_async_copy(ys_ref.at[pl.ds(0, size)], yc_scr.at[buf, pl.ds(0, size)],
                                  sems.at[buf]).wait()
    y = yc_scr[buf]
    y_hi = y.astype(bf16)
    y_lo = (y - y_hi.astype(f32)).astype(bf16)
    y_parts = jnp.concatenate([y_hi, y_lo], axis=0)
    place = _iota((MOE_TILE, 2 * TILE_SLOTS), 1)
    route = route_ref[...]
    out = x_ref[...]
    for k, w_lane in ((0, ROUTE_W1), (1, ROUTE_W2)):
        slot = slots_ref[:, k:k + 1]
        pick = jnp.where((place == slot) | (place == slot + TILE_SLOTS), 1.0, 0.0).astype(bf16)
        out = out + route[:, w_lane:w_lane + 1] * jnp.dot(pick, y_parts,
                                                           preferred_element_type=f32)
    o_ref[...] = out


def _combine(x2, route, slots, cnt, off, row, total, ys):
    T = x2.shape[0]
    grid_spec = pltpu.PrefetchScalarGridSpec(
        num_scalar_prefetch=4,
        grid=(T // MOE_TILE,),
        in_specs=[
            pl.BlockSpec((MOE_TILE, 2), lambda t, *_: (t, 0)),
            pl.BlockSpec((MOE_TILE, D_MODEL), lambda t, *_: (t, 0)),
            pl.BlockSpec((MOE_TILE, LANES), lambda t, *_: (t, 0)),
            pl.BlockSpec(memory_space=pl.ANY),
        ],
        out_specs=pl.BlockSpec((MOE_TILE, D_MODEL), lambda t, *_: (t, 0)),
        scratch_shapes=[pltpu.VMEM((2, TILE_SLOTS, D_MODEL), f32), pltpu.SemaphoreType.DMA((2,))],
    )
    return pl.pallas_call(
        _combine_kernel,
        grid_spec=grid_spec,
        out_shape=jax.ShapeDtypeStruct((T, D_MODEL), f32),
        compiler_params=_cparams("arbitrary"),
        name="moe_combine",
    )(cnt, off, row, total, slots, x2, route, ys)


def _moe(x2, h2, route, wg, wu, wd):
    T = x2.shape[0]
    nt = T // MOE_TILE
    e12 = route[:, ROUTE_E1:ROUTE_E2 + 1].astype(jnp.int32)
    r12 = route[:, ROUTE_R1:ROUTE_R2 + 1].astype(jnp.int32)
    chosen = (e12[:, :, None] == jnp.arange(N_EXPERTS, dtype=jnp.int32)).astype(jnp.int32)
    cnt = chosen.sum(axis=1).reshape(nt, MOE_TILE, N_EXPERTS).sum(axis=1)
    run = (cnt + RUN_ALIGN - 1) // RUN_ALIGN * RUN_ALIGN
    off = jnp.cumsum(run, axis=1) - run
    before = jnp.cumsum(run, axis=0) - run
    padded = (run.sum(axis=0) + EXPERT_ROWS - 1) // EXPERT_ROWS * EXPERT_ROWS
    pends = jnp.cumsum(padded)
    row = (pends - padded)[None, :] + before
    n_blocks = -(-(2 * T + nt * N_EXPERTS * (RUN_ALIGN - 1)) // EXPERT_ROWS) + N_EXPERTS
    block_start = jnp.arange(n_blocks, dtype=jnp.int32) * EXPERT_ROWS
    block_e = jnp.minimum((pends[None, :] <= block_start[:, None]).sum(axis=1),
                          N_EXPERTS - 1).astype(jnp.int32)
    n_valid = (pends[-1:] // EXPERT_ROWS).astype(jnp.int32)
    first_rank = jnp.cumsum(cnt, axis=0) - cnt
    shift = jnp.repeat(off - first_rank, MOE_TILE, axis=0)
    slots = r12 + jnp.take_along_axis(shift, e12, axis=1)
    slots_t = slots.reshape(nt, MOE_TILE, 2).transpose(0, 2, 1)
    flat = lambda a: a.reshape(-1).astype(jnp.int32)
    tables = (flat(run), flat(off), flat(row), flat(run.sum(axis=1)))
    xs = _dispatch(h2, slots_t, *tables, pends.astype(jnp.int32), n_blocks * EXPERT_ROWS)
    ys = _experts(block_e, n_valid, xs, wg, wu, wd)
    return _combine(x2, route, slots, *tables, ys)


def _pad_cols(w, n):
    return jnp.pad(w, ((0, 0), (0, n - w.shape[1])))


def _layer_weights(w_in, c_w_uq, c_w_ukv):
    gate_end = N_BRANCH * D_MODEL
    a_end = gate_end + 2 * WIDTH
    b_end = a_end + 3 * WIDTH
    cq = w_in[:, b_end:b_end + C_Q_RANK]
    ckv = w_in[:, b_end + C_Q_RANK:b_end + C_Q_RANK + C_KV_RANK]
    kr = w_in[:, b_end + C_Q_RANK + C_KV_RANK:b_end + C_Q_RANK + C_KV_RANK + C_ROPE_DIM]
    d_start = b_end + C_Q_RANK + C_KV_RANK + C_ROPE_DIM
    kr_block = jnp.pad(kr, ((0, 0), (C_NOPE_DIM, C_HEAD_PAD - C_QK_DIM)))
    w_gate = w_in[:, :gate_end].astype(bf16)
    w = jnp.concatenate([w_in[:, gate_end:b_end], _pad_cols(cq, WIDTH), ckv, kr_block,
                         w_in[:, d_start:]], axis=1).astype(bf16)
    wuq = c_w_uq.reshape(C_Q_RANK, C_HEADS, C_QK_DIM)
    wuq = jnp.pad(wuq, ((0, WIDTH - C_Q_RANK), (0, 0), (0, C_HEAD_PAD - C_QK_DIM)))
    wuq = wuq.reshape(WIDTH, C_HEADS * C_HEAD_PAD).astype(bf16)
    wukv = c_w_ukv.reshape(C_KV_RANK, C_HEADS, C_NOPE_DIM + C_V_DIM)
    wuk = jnp.pad(wukv[:, :, :C_NOPE_DIM], ((0, 0), (0, 0), (0, C_HEAD_PAD - C_NOPE_DIM)))
    wuk = wuk.reshape(C_KV_RANK, C_HEADS * C_HEAD_PAD).astype(bf16)
    wuv = wukv[:, :, C_NOPE_DIM:].reshape(C_KV_RANK, C_HEADS * C_V_DIM).astype(bf16)
    return w_gate, w, wuq, wuk, wuv


def _rotation_constants():
    ch = jnp.arange(WIDTH)
    gsum_b = ((ch[:, None] // B_HEAD_DIM) == (ch[None, :] // B_HEAD_DIM)).astype(f32) / B_HEAD_DIM
    half_b = B_ROT_DIM // 2
    src, dst = ch[:, None], ch[None, :]
    same = (src // B_HEAD_DIM) == (dst // B_HEAD_DIM)
    ds_, dd = src % B_HEAD_DIM, dst % B_HEAD_DIM
    rot_b = jnp.where(same & (dd < half_b) & (ds_ == dd + half_b), -1.0,
                      jnp.where(same & (dd >= half_b) & (dd < B_ROT_DIM) & (ds_ == dd - half_b), 1.0, 0.0))
    lc = jnp.arange(C_HEAD_PAD)
    real = lc < C_QK_DIM
    gsum_c = (real[:, None] & real[None, :]).astype(f32) / C_QK_DIM
    half_c = C_ROPE_DIM // 2
    s, d = lc[:, None], lc[None, :]
    lo = (d >= C_NOPE_DIM) & (d < C_NOPE_DIM + half_c)
    hi = (d >= C_NOPE_DIM + half_c) & (d < C_QK_DIM)
    rot_c = jnp.where(lo & (s == d + half_c), -1.0, jnp.where(hi & (s == d - half_c), 1.0, 0.0))
    slot = jnp.arange(ATT_HEADS * HEAD_SLOT)
    place_b = ((slot[None, :] // HEAD_SLOT == ch[:, None] // B_HEAD_DIM)
               & (slot[None, :] % HEAD_SLOT == ch[:, None] % B_HEAD_DIM))
    return (gsum_b.astype(bf16), rot_b.astype(bf16), place_b.astype(bf16),
            gsum_c.astype(bf16), rot_c.astype(bf16))


def kernel(x, positions, attn_norm, ffn_norm, w_in, a_ln_g, a_ln_b, a_ws, a_bs, b_qn, b_kn, c_cq_norm, c_ckv_norm, c_w_uq, c_w_ukv, c_qn, c_kn, d_conv_w, d_conv_b, d_ln_g, d_ln_b, w_branch, w_out, ffn_w_gate, ffn_w_up, ffn_w_down, moe_router, moe_router_b, moe_w_gate, moe_w_up, moe_w_down):
    B, S, _ = x.shape
    T = B * S
    assert S % ATT_BLOCK == 0 and T % EXPERT_ROWS == 0
    x2 = x.reshape(T, D_MODEL)
    cos_b, sin_b, cos_c, sin_c = _rope_tables(positions)
    gsum_b, rot_b, place_b, gsum_c, rot_c = _rotation_constants()
    row = lambda v: v.reshape(1, -1)

    for l in range(DEPTH):
        w_gate, w, wuq, wuk, wuv = _layer_weights(w_in[l], c_w_uq[l], c_w_ukv[l])
        proj = _inproj(x2, row(attn_norm[l]), w)

        o_a = _sgu(proj, row(a_ln_g[l]), row(a_ln_b[l]), a_ws[l], a_bs[l].T)

        qb, kb, vtb = _moba_prep(proj, B, S, cos_b, sin_b,
                                 row(jnp.tile(b_qn[l], B_HEADS)), row(jnp.tile(b_kn[l], B_HEADS)),
                                 gsum_b, rot_b, place_b)
        o_b = _attention(qb, kb, vtb, B, S, "moba_attention")

        pad_gain = lambda g, n: row(jnp.pad(g, (0, n - g.shape[0])))
        qc, kc, vtc = _mla_prep(proj, B, S, cos_c, sin_c, pad_gain(c_cq_norm[l], WIDTH),
                                row(c_ckv_norm[l]), wuq, wuk, wuv,
                                pad_gain(c_qn[l], C_HEAD_PAD), pad_gain(c_kn[l], C_HEAD_PAD),
                                gsum_c, rot_c)
        o_c = _attention(qc, kc, vtc, B, S, "mla_attention")

        o_d = _conformer(proj, B, S, d_conv_w[l], row(d_conv_b[l]), row(d_ln_g[l]), row(d_ln_b[l]))

        merge_args = (x2, row(attn_norm[l]), w_gate, o_a, o_b, o_c, o_d, w_branch[l].astype(bf16),
                      w_out[l].astype(bf16), row(ffn_norm[l]))
        i = l // 2
        if l % 2 == 0:
            x2, h2 = _merge(*merge_args)
            x2 = _ffn(x2, h2, ffn_w_gate[i].astype(bf16), ffn_w_up[i].astype(bf16),
                      ffn_w_down[i].astype(bf16))
        else:
            router = (_pad_cols(moe_router[i], LANES), _pad_cols(row(moe_router_b[i]), LANES))
            x2, h2, route = _merge(*merge_args, router=router)
            x2 = _moe(x2, h2, route, moe_w_gate[i].astype(bf16),
                      moe_w_up[i].astype(bf16), moe_w_down[i].astype(bf16))
    return x2.reshape(B, S, D_MODEL)
```

```python
import functools

import jax
import jax.numpy as jnp
from jax import lax
from jax.experimental import pallas as pl
from jax.experimental.pallas import tpu as pltpu

D_MODEL = 1024
DEPTH = 4
N_BRANCH = 4
ROPE_THETA = 500000.0
NORM_EPS = 1e-6
NEG_INF = -1e30

WIDTH = 256
A_GROUPS = 4
A_GROUP_DIM = 64
A_CHUNK = 128
B_HEADS = 4
B_HEAD_DIM = 64
B_ROT_DIM = 16
MOBA_BLOCK = 256
MOBA_TOPK = 3
C_HEADS = 4
C_Q_RANK = 192
C_KV_RANK = 128
C_NOPE_DIM = 64
C_ROPE_DIM = 32
C_V_DIM = 64
C_QK_DIM = C_NOPE_DIM + C_ROPE_DIM
C_HEAD_PAD = 128
D_CONV = 31
FFN_DIM = 2816
N_EXPERTS = 8
EXPERT_DIM = 1408
EXPERT_ROWS = 512

ATT_BLOCK = 256
ATT_QBLOCK = 512
ATT_HEADS = 4
HEAD_SLOT = 128
V_AUG = C_V_DIM + 16
LOG2E = 1.4426950408889634
SM_ROWS = 32
LANES = 128
SUBLANES = 8
CONV_HALO = 32
CONV_ROWS = 64
LN_ROWS = 128

GATE_COLS = N_BRANCH * D_MODEL
COL_AU = 0
COL_AV = 256
COL_BQ = 512
COL_BK = 768
COL_BV = 1024
COL_CQ = 1280
COL_CKV = 1536
COL_KR = 1664
COL_DA = 1792
COL_DG = 2048
PROJ_COLS = 2304

VMEM_LIMIT = 56 * 1024 * 1024

bf16 = jnp.bfloat16
f32 = jnp.float32


def _cparams(*sem):
    return pltpu.CompilerParams(dimension_semantics=sem, vmem_limit_bytes=VMEM_LIMIT)


def _rms(x, n):
    return x * lax.rsqrt(jnp.sum(x * x, axis=-1, keepdims=True) * (1.0 / n) + NORM_EPS)


def _layernorm(x, g, b):
    mu = jnp.mean(x, axis=-1, keepdims=True)
    xc = x - mu
    return xc * lax.rsqrt(jnp.mean(xc * xc, axis=-1, keepdims=True) + NORM_EPS) * g + b


def _sigmoid(x):
    return 0.5 * jnp.tanh(0.5 * x) + 0.5


def _iota(shape, dim):
    return lax.broadcasted_iota(jnp.int32, shape, dim)


def _rope_table_kernel(pos_ref, inv_ref, eb_ref, ec_ref, cb_ref, sb_ref, cc_ref, sc_ref):
    ang = pos_ref[0].astype(f32) * inv_ref[...]
    c = jnp.cos(ang)
    s = jnp.sin(ang)
    hi = lax.Precision.HIGHEST
    cb_ref[0] = jnp.dot(c, eb_ref[...], precision=hi, preferred_element_type=f32)
    sb_ref[0] = jnp.dot(s, eb_ref[...], precision=hi, preferred_element_type=f32)
    cc_ref[0] = jnp.dot(c, ec_ref[...], precision=hi, preferred_element_type=f32)
    sc_ref[0] = jnp.dot(s, ec_ref[...], precision=hi, preferred_element_type=f32)


def _rope_tables(positions):
    B, S = positions.shape
    R = min(512, S)
    nb = B_ROT_DIM // 2
    nc = C_ROPE_DIM // 2
    inv_b = f32(ROPE_THETA) ** (-jnp.arange(0, B_ROT_DIM, 2, dtype=f32) / B_ROT_DIM)
    inv_c = f32(ROPE_THETA) ** (-jnp.arange(0, C_ROPE_DIM, 2, dtype=f32) / C_ROPE_DIM)
    inv = jnp.zeros((1, LANES), f32).at[0, :nb].set(inv_b).at[0, nb:nb + nc].set(inv_c)
    zero_lane = LANES - 1
    ch = jnp.arange(WIDTH)
    d = ch % B_HEAD_DIM
    src_b = jnp.where(d < B_ROT_DIM, d % nb, zero_lane)
    eb = (jnp.arange(LANES)[:, None] == src_b[None, :]).astype(f32)
    dc = jnp.arange(C_HEAD_PAD)
    in_rope = (dc >= C_NOPE_DIM) & (dc < C_QK_DIM)
    src_c = jnp.where(in_rope, nb + (dc - C_NOPE_DIM) % nc, zero_lane)
    ec = (jnp.arange(LANES)[:, None] == src_c[None, :]).astype(f32)
    pos3 = positions.reshape(B, S, 1)
    out = pl.pallas_call(
        _rope_table_kernel,
        grid=(B, S // R),
        in_specs=[
            pl.BlockSpec((1, R, 1), lambda b, r: (b, r, 0)),
            pl.BlockSpec((1, LANES), lambda b, r: (0, 0)),
            pl.BlockSpec((LANES, WIDTH), lambda b, r: (0, 0)),
            pl.BlockSpec((LANES, C_HEAD_PAD), lambda b, r: (0, 0)),
        ],
        out_specs=[
            pl.BlockSpec((1, R, WIDTH), lambda b, r: (b, r, 0)),
            pl.BlockSpec((1, R, WIDTH), lambda b, r: (b, r, 0)),
            pl.BlockSpec((1, R, C_HEAD_PAD), lambda b, r: (b, r, 0)),
            pl.BlockSpec((1, R, C_HEAD_PAD), lambda b, r: (b, r, 0)),
        ],
        out_shape=[
            jax.ShapeDtypeStruct((B, S, WIDTH), f32),
            jax.ShapeDtypeStruct((B, S, WIDTH), f32),
            jax.ShapeDtypeStruct((B, S, C_HEAD_PAD), f32),
            jax.ShapeDtypeStruct((B, S, C_HEAD_PAD), f32),
        ],
        compiler_params=_cparams("parallel", "parallel"),
        name="rope_tables",
    )(pos3, inv, eb, ec)
    return out


PROJ_CHUNKS = ((0, COL_CQ), (COL_CQ, PROJ_COLS))


def _inproj_kernel(x_ref, g_ref, w_ref, o_ref):
    h = (_rms(x_ref[...], D_MODEL) * g_ref[...]).astype(bf16)
    for lo, hi in PROJ_CHUNKS:
        o_ref[:, lo:hi] = jnp.dot(h, w_ref[:, lo:hi], preferred_element_type=f32).astype(bf16)


def _inproj(x2, g, w):
    T = x2.shape[0]
    tm = min(512, T)
    return pl.pallas_call(
        _inproj_kernel,
        grid=(T // tm,),
        in_specs=[
            pl.BlockSpec((tm, D_MODEL), lambda t: (t, 0)),
            pl.BlockSpec((1, D_MODEL), lambda t: (0, 0)),
            pl.BlockSpec((D_MODEL, PROJ_COLS), lambda t: (0, 0)),
        ],
        out_specs=pl.BlockSpec((tm, PROJ_COLS), lambda t: (t, 0)),
        out_shape=jax.ShapeDtypeStruct((T, PROJ_COLS), bf16),
        compiler_params=_cparams("parallel"),
        name="inproj",
    )(x2, g, w)


def _sgu_kernel(u_ref, v_ref, g_ref, b_ref, ws_ref, bst_ref, o_ref, *, chunks):
    row = _iota((A_CHUNK, A_CHUNK), 0)
    col = _iota((A_CHUNK, A_CHUNK), 1)
    causal = col <= row
    group = _iota((A_CHUNK, WIDTH), 1) // A_GROUP_DIM
    wm = [jnp.where(causal, ws_ref[g], 0.0).astype(bf16) for g in range(A_GROUPS)]
    bias = jnp.zeros((A_CHUNK, WIDTH), f32)
    for g in range(A_GROUPS):
        bias = jnp.where(group == g, bst_ref[:, g:g + 1], bias)
    for c in range(chunks):
        rows = slice(c * A_CHUNK, (c + 1) * A_CHUNK)
        vn = _layernorm(v_ref[rows, :].astype(f32), g_ref[...], b_ref[...]).astype(bf16)
        z = bias
        for g in range(A_GROUPS):
            zg = jnp.dot(wm[g], vn, preferred_element_type=f32)
            z = z + jnp.where(group == g, zg, 0.0)
        o_ref[rows, :] = (u_ref[rows, :].astype(f32) * z).astype(bf16)


def _sgu(proj, ln_g, ln_b, ws, bs_t):
    T = proj.shape[0]
    ta = min(512, T)
    cu, cv = COL_AU // WIDTH, COL_AV // WIDTH
    return pl.pallas_call(
        functools.partial(_sgu_kernel, chunks=ta // A_CHUNK),
        grid=(T // ta,),
        in_specs=[
            pl.BlockSpec((ta, WIDTH), lambda t: (t, cu)),
            pl.BlockSpec((ta, WIDTH), lambda t: (t, cv)),
            pl.BlockSpec((1, WIDTH), lambda t: (0, 0)),
            pl.BlockSpec((1, WIDTH), lambda t: (0, 0)),
            pl.BlockSpec((A_GROUPS, A_CHUNK, A_CHUNK), lambda t: (0, 0, 0)),
            pl.BlockSpec((A_CHUNK, A_GROUPS), lambda t: (0, 0)),
        ],
        out_specs=pl.BlockSpec((ta, WIDTH), lambda t: (t, 0)),
        out_shape=jax.ShapeDtypeStruct((T, WIDTH), bf16),
        compiler_params=_cparams("parallel"),
        name="sgu",
    )(proj, proj, ln_g, ln_b, ws, bs_t)


def _conv_kernel(a_ref, g_ref, w_ref, cb_ref, lg_ref, lb_ref, o_ref, z_scr, *, seq):
    first = CONV_HALO - (D_CONV - 1)
    z_scr[0, 0:CONV_HALO, :] = jnp.zeros((CONV_HALO, WIDTH), f32)
    z_scr[0, CONV_HALO + seq:, :] = jnp.zeros((CONV_ROWS, WIDTH), f32)

    def glu(c, carry):
        base = pl.multiple_of(c * LN_ROWS, LN_ROWS)
        rows = pl.ds(base, LN_ROWS)
        z_scr[0, pl.ds(base + CONV_HALO, LN_ROWS), :] = (
            a_ref[rows, :].astype(f32) * _sigmoid(g_ref[rows, :].astype(f32)))
        return carry

    lax.fori_loop(0, seq // LN_ROWS, glu, 0)

    def shift(c, carry):
        base = pl.multiple_of(c * CONV_ROWS, CONV_ROWS)
        win = z_scr[0, pl.ds(base, CONV_ROWS + SUBLANES), :]
        for r in range(1, SUBLANES):
            rolled = pltpu.roll(win, shift=CONV_ROWS + SUBLANES - r, axis=0)
            z_scr[r, pl.ds(base, CONV_ROWS), :] = rolled[:CONV_ROWS]
        return carry

    lax.fori_loop(0, seq // CONV_ROWS + 1, shift, 0)

    def taps(c, carry):
        base = pl.multiple_of(c * CONV_ROWS, CONV_ROWS)
        acc = jnp.zeros((CONV_ROWS, WIDTH), f32) + cb_ref[...]
        for k in range(D_CONV):
            off = first + k
            lo = pl.multiple_of(base + off // SUBLANES * SUBLANES, SUBLANES)
            acc = acc + w_ref[k:k + 1, :] * z_scr[off % SUBLANES, pl.ds(lo, CONV_ROWS), :]
        z_scr[1, pl.ds(base, CONV_ROWS), :] = acc
        return carry

    lax.fori_loop(0, seq // CONV_ROWS, taps, 0)

    def norm(c, carry):
        base = pl.multiple_of(c * LN_ROWS, LN_ROWS)
        y = _layernorm(z_scr[1, pl.ds(base, LN_ROWS), :], lg_ref[...], lb_ref[...])
        o_ref[pl.ds(base, LN_ROWS), :] = (y * _sigmoid(y)).astype(bf16)
        return carry

    lax.fori_loop(0, seq // LN_ROWS, norm, 0, unroll=2)


def _conformer(proj, B, S, conv_w, conv_b, ln_g, ln_b):
    ca, cg = COL_DA // WIDTH, COL_DG // WIDTH
    return pl.pallas_call(
        functools.partial(_conv_kernel, seq=S),
        grid=(B,),
        in_specs=[
            pl.BlockSpec((S, WIDTH), lambda b: (b, ca)),
            pl.BlockSpec((S, WIDTH), lambda b: (b, cg)),
            pl.BlockSpec((D_CONV, WIDTH), lambda b: (0, 0)),
            pl.BlockSpec((1, WIDTH), lambda b: (0, 0)),
            pl.BlockSpec((1, WIDTH), lambda b: (0, 0)),
            pl.BlockSpec((1, WIDTH), lambda b: (0, 0)),
        ],
        out_specs=pl.BlockSpec((S, WIDTH), lambda b: (b, 0)),
        out_shape=jax.ShapeDtypeStruct((B * S, WIDTH), bf16),
        scratch_shapes=[pltpu.VMEM((SUBLANES, CONV_HALO + S + CONV_ROWS, WIDTH), f32)],
        compiler_params=_cparams("parallel"),
        name="conformer_conv",
    )(proj, proj, conv_w, conv_b, ln_g, ln_b)


def _value_rows(v_tok):
    vt = v_tok.T
    ones = jnp.ones((V_AUG - C_V_DIM, vt.shape[1]), f32)
    rows = []
    for h in range(ATT_HEADS):
        rows += [vt[h * C_V_DIM:(h + 1) * C_V_DIM], ones]
    return jnp.concatenate(rows, axis=0).astype(bf16)


def _moba_prep_kernel(q_ref, k_ref, v_ref, cos_ref, sin_ref, qg_ref, kg_ref, gsum_ref, rot_ref,
                      place_ref, qo_ref, ko_ref, vt_ref, kmean_scr, *, nblk):
    c = pl.program_id(1)

    @pl.when(c == 0)
    def _():
        kmean_scr[...] = jnp.zeros(kmean_scr.shape, f32)

    cos = cos_ref[0]
    sin = sin_ref[0]

    def norm_rope(x, gain):
        ss = jnp.dot((x * x).astype(bf16), gsum_ref[...], preferred_element_type=f32)
        xn = x * lax.rsqrt(ss + NORM_EPS) * gain
        xr = jnp.dot(xn.astype(bf16), rot_ref[...], preferred_element_type=f32)
        return xn * cos + xr * sin

    q = norm_rope(q_ref[...].astype(f32), qg_ref[...])
    k = norm_rope(k_ref[...].astype(f32), kg_ref[...])
    slot_lane = _iota((ATT_BLOCK, ATT_HEADS * HEAD_SLOT), 1) % HEAD_SLOT
    k_slots = jnp.dot(k.astype(bf16), place_ref[...], preferred_element_type=f32)
    ko_ref[0, 0] = jnp.where(slot_lane == B_HEAD_DIM + c, 1.0, k_slots).astype(bf16)
    vt_ref[0, 0] = _value_rows(v_ref[...].astype(f32))

    head = _iota((nblk, WIDTH), 1) // B_HEAD_DIM
    blk = _iota((nblk, ATT_BLOCK), 0)
    past = blk < c
    kmean = kmean_scr[...]
    zeros_lo = jnp.zeros((B_HEAD_DIM, ATT_BLOCK), f32)
    zeros_hi = jnp.zeros((HEAD_SLOT - B_HEAD_DIM - nblk, ATT_BLOCK), f32)
    bias_rows = []
    for h in range(B_HEADS):
        km = jnp.where(head == h, kmean, 0.0)
        sc = lax.dot_general(km, q, (((1,), (1,)), ((), ())), precision=lax.Precision.HIGHEST,
                             preferred_element_type=f32)
        sc = jnp.where(past, sc, NEG_INF)
        rank = jnp.zeros((nblk, ATT_BLOCK), jnp.int32)
        for j in range(nblk):
            other = sc[j:j + 1, :]
            ahead = (other > sc) | ((other == sc) & (j < blk))
            rank = rank + ahead.astype(jnp.int32)
        visible = (past & (rank < MOBA_TOPK)) | (blk == c)
        bias_rows += [zeros_lo, jnp.where(visible, 0.0, NEG_INF), zeros_hi]
    bias = jnp.concatenate(bias_rows, axis=0)
    q_slots = jnp.dot((q * (B_HEAD_DIM ** -0.5 * LOG2E)).astype(bf16), place_ref[...],
                      preferred_element_type=f32)
    qo_ref[0] = (q_slots.T + bias).astype(bf16)

    kmean_scr[pl.ds(c, 1), :] = jnp.mean(k, axis=0, keepdims=True)


def _moba_prep(proj, B, S, cos_b, sin_b, qg, kg, gsum, rot, place):
    nblk = S // ATT_BLOCK
    assert nblk <= HEAD_SLOT - B_HEAD_DIM
    HD = ATT_HEADS * HEAD_SLOT
    cq, ck, cv = COL_BQ // WIDTH, COL_BK // WIDTH, COL_BV // WIDTH
    tile = lambda col: pl.BlockSpec((ATT_BLOCK, WIDTH), lambda b, c: (b * nblk + c, col))
    table = pl.BlockSpec((1, ATT_BLOCK, WIDTH), lambda b, c: (b, c, 0))
    const = lambda shape: pl.BlockSpec(shape, lambda b, c: (0,) * len(shape))
    return pl.pallas_call(
        functools.partial(_moba_prep_kernel, nblk=nblk),
        grid=(B, nblk),
        in_specs=[tile(cq), tile(ck), tile(cv), table, table,
                  const((1, WIDTH)), const((1, WIDTH)), const((WIDTH, WIDTH)), const((WIDTH, WIDTH)),
                  const((WIDTH, HD))],
        out_specs=[
            pl.BlockSpec((1, HD, ATT_BLOCK), lambda b, c: (b, 0, c)),
            pl.BlockSpec((1, 1, ATT_BLOCK, HD), lambda b, c: (b, c, 0, 0)),
            pl.BlockSpec((1, 1, ATT_HEADS * V_AUG, ATT_BLOCK), lambda b, c: (b, c, 0, 0)),
        ],
        out_shape=[
            jax.ShapeDtypeStruct((B, HD, S), bf16),
            jax.ShapeDtypeStruct((B, nblk, ATT_BLOCK, HD), bf16),
            jax.ShapeDtypeStruct((B, nblk, ATT_HEADS * V_AUG, ATT_BLOCK), bf16),
        ],
        scratch_shapes=[pltpu.VMEM((nblk, WIDTH), f32)],
        compiler_params=_cparams("parallel", "arbitrary"),
        name="moba_prep",
    )(proj, proj, proj, cos_b, sin_b, qg, kg, gsum, rot, place)


def _mla_prep_kernel(cq_ref, ckv_ref, kr_ref, cos_ref, sin_ref, cqg_ref, ckvg_ref, wuq_ref, wuk_ref,
                     wuv_ref, qg_ref, kg_ref, gsum_ref, rot_ref, qo_ref, ko_ref, vt_ref):
    cqn = (_rms(cq_ref[...].astype(f32), C_Q_RANK) * cqg_ref[...]).astype(bf16)
    ckvn = (_rms(ckv_ref[...].astype(f32), C_KV_RANK) * ckvg_ref[...]).astype(bf16)
    q = jnp.dot(cqn, wuq_ref[...], preferred_element_type=f32)
    k = jnp.dot(ckvn, wuk_ref[...], preferred_element_type=f32)
    v = jnp.dot(ckvn, wuv_ref[...], preferred_element_type=f32)
    kr = kr_ref[...].astype(f32)
    cos = cos_ref[0]
    sin = sin_ref[0]

    def norm_rope(x, gain):
        ss = jnp.dot((x * x).astype(bf16), gsum_ref[...], preferred_element_type=f32)
        xn = x * lax.rsqrt(ss + NORM_EPS) * gain
        xr = jnp.dot(xn.astype(bf16), rot_ref[...], preferred_element_type=f32)
        return xn * cos + xr * sin

    for h in range(C_HEADS):
        sl = slice(h * C_HEAD_PAD, (h + 1) * C_HEAD_PAD)
        qh = norm_rope(q[:, sl], qg_ref[...])
        kh = norm_rope(k[:, sl] + kr, kg_ref[...])
        qo_ref[0, sl, :] = (qh * (C_QK_DIM ** -0.5 * LOG2E)).T.astype(bf16)
        ko_ref[0, 0, :, sl] = kh.astype(bf16)
    vt_ref[0, 0] = _value_rows(v)


def _mla_prep(proj, B, S, cos_c, sin_c, cqg, ckvg, wuq, wuk, wuv, qg, kg, gsum, rot):
    nblk = S // ATT_BLOCK
    HD = C_HEADS * C_HEAD_PAD
    table = pl.BlockSpec((1, ATT_BLOCK, C_HEAD_PAD), lambda b, c: (b, c, 0))
    const = lambda shape: pl.BlockSpec(shape, lambda b, c: (0,) * len(shape))
    return pl.pallas_call(
        _mla_prep_kernel,
        grid=(B, nblk),
        in_specs=[
            pl.BlockSpec((ATT_BLOCK, WIDTH), lambda b, c: (b * nblk + c, COL_CQ // WIDTH)),
            pl.BlockSpec((ATT_BLOCK, LANES), lambda b, c: (b * nblk + c, COL_CKV // LANES)),
            pl.BlockSpec((ATT_BLOCK, LANES), lambda b, c: (b * nblk + c, COL_KR // LANES)),
            table, table,
            const((1, WIDTH)), const((1, C_KV_RANK)), const((WIDTH, HD)), const((C_KV_RANK, HD)),
            const((C_KV_RANK, WIDTH)), const((1, C_HEAD_PAD)), const((1, C_HEAD_PAD)),
            const((C_HEAD_PAD, C_HEAD_PAD)), const((C_HEAD_PAD, C_HEAD_PAD)),
        ],
        out_specs=[
            pl.BlockSpec((1, HD, ATT_BLOCK), lambda b, c: (b, 0, c)),
            pl.BlockSpec((1, 1, ATT_BLOCK, HD), lambda b, c: (b, c, 0, 0)),
            pl.BlockSpec((1, 1, ATT_HEADS * V_AUG, ATT_BLOCK), lambda b, c: (b, c, 0, 0)),
        ],
        out_shape=[
            jax.ShapeDtypeStruct((B, HD, S), bf16),
            jax.ShapeDtypeStruct((B, nblk, ATT_BLOCK, HD), bf16),
            jax.ShapeDtypeStruct((B, nblk, ATT_HEADS * V_AUG, ATT_BLOCK), bf16),
        ],
        compiler_params=_cparams("parallel", "parallel"),
        name="mla_prep",
    )(proj, proj, proj, cos_c, sin_c, cqg, ckvg, wuq, wuk, wuv, qg, kg, gsum, rot)


def _attn_kernel(q_ref, k_ref, vt_ref, o_ref, *scratch):
    H = ATT_HEADS
    s_scr, p_scr, a_scr = ([scratch[(2 * n + par) * H:(2 * n + par + 1) * H] for par in range(2)]
                           for n in range(3))
    acc_scr, m_scr = (scratch[n * H:(n + 1) * H] for n in (6, 7))
    out_scr = scratch[8 * H]
    top_scr = [scratch[8 * H + 1 + par * H:8 * H + 1 + (par + 1) * H] for par in range(2)]
    pairs = pl.program_id(1) * (ATT_QBLOCK // ATT_BLOCK // 2)
    chunks = ATT_BLOCK // SM_ROWS

    def scores(j, par, h):
        slot = slice(h * HEAD_SLOT, (h + 1) * HEAD_SLOT)
        s = jnp.dot(k_ref[0, j, :, slot], q_ref[0, slot, :],
                    preferred_element_type=f32)
        s_scr[par][h][...] = s
        top_scr[par][h][...] = jnp.max(s, axis=0, keepdims=True)

    def softmax(par, h, diag_offset=None):
        def rows(r):
            s = s_scr[par][h][r * SM_ROWS:(r + 1) * SM_ROWS, :]
            if diag_offset is not None:
                kpos = _iota((SM_ROWS, ATT_QBLOCK), 0) + (r * SM_ROWS + diag_offset)
                s = jnp.where(kpos <= _iota((SM_ROWS, ATT_QBLOCK), 1), s, NEG_INF)
            return s
        if diag_offset is None:
            top = top_scr[par][h][...]
        else:
            mx = rows(0)
            for r in range(1, chunks):
                mx = jnp.maximum(mx, rows(r))
            top = jnp.max(mx, axis=0, keepdims=True)
        m_old = m_scr[h][...]
        m_new = jnp.maximum(m_old, top)
        a_scr[par][h][...] = jnp.exp2(m_old - m_new)
        m_scr[h][...] = m_new
        for r in range(chunks):
            p_scr[par][h][r * SM_ROWS:(r + 1) * SM_ROWS, :] = (
                jnp.exp2((rows(r) - m_new).astype(bf16)))

    def values(j, par, h):
        pv = jnp.dot(vt_ref[0, j, h * V_AUG:(h + 1) * V_AUG, :], p_scr[par][h][...],
                     preferred_element_type=f32)
        acc_scr[h][...] = a_scr[par][h][...] * acc_scr[h][...] + pv

    for h in range(H):
        m_scr[h][...] = jnp.full(m_scr[h].shape, NEG_INF, f32)
        a_scr[1][h][...] = jnp.zeros(a_scr[1][h].shape, f32)
        acc_scr[h][...] = jnp.zeros(acc_scr[h].shape, f32)
        p_scr[1][h][...] = jnp.zeros(p_scr[1][h].shape, bf16)
        scores(0, 0, h)

    def step(b, par, diag_offset=None, last=False):
        for h in range(H):
            if not last:
                scores(b + 1, 1 - par, h)
            values(jnp.maximum(b - 1, 0), 1 - par, h)
            softmax(par, h, diag_offset)

    def past_pair(t, carry):
        step(2 * t, 0)
        step(2 * t + 1, 1)
        return carry

    lax.fori_loop(0, pairs, past_pair, 0)
    step(2 * pairs, 0, diag_offset=0)
    step(2 * pairs + 1, 1, diag_offset=ATT_BLOCK, last=True)

    for h in range(H):
        values(2 * pairs + 1, 1, h)
        a = acc_scr[h][...]
        out_scr[h * C_V_DIM:(h + 1) * C_V_DIM, :] = a[:C_V_DIM] / a[C_V_DIM:C_V_DIM + 1]
    o_ref[...] = out_scr[...].T.astype(bf16)


def _attention(q, k, vt, B, S, name):
    nblk = S // ATT_BLOCK
    nq = S // ATT_QBLOCK
    HD = ATT_HEADS * HEAD_SLOT
    return pl.pallas_call(
        _attn_kernel,
        grid=(B, nq),
        in_specs=[
            pl.BlockSpec((1, HD, ATT_QBLOCK), lambda b, i: (b, 0, i)),
            pl.BlockSpec((1, nblk, ATT_BLOCK, HD), lambda b, i: (b, 0, 0, 0)),
            pl.BlockSpec((1, nblk, ATT_HEADS * V_AUG, ATT_BLOCK), lambda b, i: (b, 0, 0, 0)),
        ],
        out_specs=pl.BlockSpec((ATT_QBLOCK, WIDTH), lambda b, i: (b * nq + i, 0)),
        out_shape=jax.ShapeDtypeStruct((B * S, WIDTH), bf16),
        scratch_shapes=(
            [pltpu.VMEM((ATT_BLOCK, ATT_QBLOCK), f32) for _ in range(2 * ATT_HEADS)]
            + [pltpu.VMEM((ATT_BLOCK, ATT_QBLOCK), bf16) for _ in range(2 * ATT_HEADS)]
            + [pltpu.VMEM((1, ATT_QBLOCK), f32) for _ in range(2 * ATT_HEADS)]
            + [pltpu.VMEM((V_AUG, ATT_QBLOCK), f32) for _ in range(ATT_HEADS)]
            + [pltpu.VMEM((1, ATT_QBLOCK), f32) for _ in range(ATT_HEADS)]
            + [pltpu.VMEM((WIDTH, ATT_QBLOCK), f32)]
            + [pltpu.VMEM((1, ATT_QBLOCK), f32) for _ in range(2 * ATT_HEADS)]),
        compiler_params=_cparams("parallel", "arbitrary"),
        name=name,
    )(q, k, vt)


ROUTE_E1, ROUTE_E2, ROUTE_W1, ROUTE_W2, ROUTE_S1, ROUTE_S2 = range(6)


def _route_rows(h, wr_ref, br_ref, route_ref, tcount_ref):
    tm = h.shape[0]
    lane = _iota((tm, LANES), 1)
    h_hi = h.astype(bf16)
    h_lo = (h - h_hi.astype(f32)).astype(bf16)
    w = wr_ref[...]
    w_hi = w.astype(bf16)
    w_lo = (w - w_hi.astype(f32)).astype(bf16)
    logits = (jnp.dot(h_hi, w_hi, preferred_element_type=f32)
              + (jnp.dot(h_hi, w_lo, preferred_element_type=f32)
                 + jnp.dot(h_lo, w_hi, preferred_element_type=f32))) + br_ref[...]
    logits = jnp.where(lane < N_EXPERTS, logits, NEG_INF)
    m1 = jnp.max(logits, axis=-1, keepdims=True)
    e1 = jnp.min(jnp.where(logits == m1, lane, LANES), axis=-1, keepdims=True)
    rest = jnp.where(lane == e1, NEG_INF, logits)
    m2 = jnp.max(rest, axis=-1, keepdims=True)
    e2 = jnp.min(jnp.where(rest == m2, lane, LANES), axis=-1, keepdims=True)
    ex = jnp.exp(m2 - m1)
    w1 = 1.0 / (1.0 + ex)
    w2 = ex / (1.0 + ex)
    chosen = ((lane == e1) | (lane == e2)).astype(f32)
    row, col = _iota((tm, tm), 0), _iota((tm, tm), 1)
    earlier = ((col < row) & (col // MOE_TILE == row // MOE_TILE)).astype(bf16)
    before = jnp.dot(earlier, chosen.astype(bf16), preferred_element_type=f32)
    upper = (_iota((LANES, LANES), 0) < _iota((LANES, LANES), 1)).astype(bf16)
    token_tile = _iota((tm, LANES), 0) // MOE_TILE
    place = before
    for r in range(tm // MOE_TILE):
        cnt = jnp.sum(chosen[r * MOE_TILE:(r + 1) * MOE_TILE], axis=0, keepdims=True)
        tcount_ref[0, r:r + 1, :] = cnt
        run = jnp.floor((cnt + (RUN_ALIGN - 1)) * (1.0 / RUN_ALIGN)) * RUN_ALIGN
        off = jnp.dot(jnp.broadcast_to(run, (SUBLANES, LANES)).astype(bf16), upper,
                      preferred_element_type=f32)
        place = place + jnp.where(token_tile == r, off[0:1], 0.0)
    s1 = jnp.sum(jnp.where(lane == e1, place, 0.0), axis=-1, keepdims=True)
    s2 = jnp.sum(jnp.where(lane == e2, place, 0.0), axis=-1, keepdims=True)
    out = jnp.zeros((tm, LANES), f32)
    for idx, val in ((ROUTE_E1, e1.astype(f32)), (ROUTE_E2, e2.astype(f32)), (ROUTE_W1, w1),
                     (ROUTE_W2, w2), (ROUTE_S1, s1), (ROUTE_S2, s2)):
        out = jnp.where(lane == idx, val, out)
    route_ref[...] = out


def _merge_kernel(x_ref, ag_ref, wgate_ref, oa_ref, ob_ref, oc_ref, od_ref,
                  wb_ref, wo_ref, ng_ref, *rest, routed):
    branches = (oa_ref, ob_ref, oc_ref, od_ref)
    xo_ref, ho_ref = rest[2:4] if routed else rest
    tm = x_ref.shape[0]
    halves = [slice(r * (tm // 2), (r + 1) * (tm // 2)) for r in range(2)]
    gated = []
    for rows in halves:
        hin = (_rms(x_ref[rows, :], D_MODEL) * ag_ref[...]).astype(bf16)
        gates = [jnp.dot(hin, wgate_ref[:, n * D_MODEL:(n + 1) * D_MODEL],
                         preferred_element_type=f32) for n in range(N_BRANCH)]
        projected = [jnp.dot(o_ref[rows, :], wb_ref[n], preferred_element_type=f32)
                     for n, o_ref in enumerate(branches)]
        gated.append((gates, projected))
    hs = []
    for rows, (gates, projected) in zip(halves, gated):
        y = None
        for gate, br in zip(gates, projected):
            t = _sigmoid(gate) * br
            y = t if y is None else y + t
        x = x_ref[rows, :] + jnp.dot(y.astype(bf16), wo_ref[...], preferred_element_type=f32)
        xo_ref[rows, :] = x
        hs.append(_rms(x, D_MODEL) * ng_ref[...])
    if routed:
        wr_ref, br_ref, _, _, route_ref, tcount_ref = rest
        h = jnp.concatenate(hs, axis=0)
        ho_ref[...] = h
        _route_rows(h, wr_ref, br_ref, route_ref, tcount_ref)
    else:
        for rows, h in zip(halves, hs):
            ho_ref[rows, :] = h.astype(bf16)


def _merge(x2, ag, wgate, oa, ob, oc, od, wb, wo, ng, router=None):
    T = x2.shape[0]
    tm = min(512, T)
    routed = router is not None
    branch = pl.BlockSpec((tm, WIDTH), lambda t: (t, 0))
    rows = pl.BlockSpec((tm, D_MODEL), lambda t: (t, 0))
    resident = lambda shape: pl.BlockSpec(shape, lambda t: (0,) * len(shape),
                                          pipeline_mode=pl.Buffered(1))
    in_specs = [
        rows, pl.BlockSpec((1, D_MODEL), lambda t: (0, 0)), resident((D_MODEL, GATE_COLS)),
        branch, branch, branch, branch,
        resident((N_BRANCH, WIDTH, D_MODEL)), resident((D_MODEL, D_MODEL)),
        pl.BlockSpec((1, D_MODEL), lambda t: (0, 0)),
    ]
    args = [x2, ag, wgate, oa, ob, oc, od, wb, wo, ng]
    out_specs = [rows, rows]
    out_shape = [jax.ShapeDtypeStruct((T, D_MODEL), f32),
                 jax.ShapeDtypeStruct((T, D_MODEL), f32 if routed else bf16)]
    if routed:
        in_specs += [pl.BlockSpec((D_MODEL, LANES), lambda t: (0, 0)),
                     pl.BlockSpec((1, LANES), lambda t: (0, 0))]
        args += list(router)
        out_specs += [pl.BlockSpec((tm, LANES), lambda t: (t, 0)),
                      pl.BlockSpec((1, tm // MOE_TILE, LANES), lambda t: (t, 0, 0))]
        out_shape += [jax.ShapeDtypeStruct((T, LANES), f32),
                      jax.ShapeDtypeStruct((T // tm, tm // MOE_TILE, LANES), f32)]
    return pl.pallas_call(
        functools.partial(_merge_kernel, routed=routed),
        grid=(T // tm,),
        in_specs=in_specs,
        out_specs=out_specs,
        out_shape=out_shape,
        compiler_params=_cparams("parallel"),
        name="merge_route" if routed else "merge",
    )(*args)


def _swiglu_block(h, wg, wu, wd):
    a = jnp.dot(h, wg, preferred_element_type=f32)
    b = jnp.dot(h, wu, preferred_element_type=f32)
    t = (a * _sigmoid(a) * b).astype(bf16)
    return jnp.dot(t, wd, preferred_element_type=f32)


FFN_CHUNKS = ((0, 1536), (1536, FFN_DIM))


def _ffn_kernel(x_ref, h_ref, wg_ref, wu_ref, wd_ref, o_ref):
    h = h_ref[...]
    acc = x_ref[...]
    for lo, hi in FFN_CHUNKS:
        acc = acc + _swiglu_block(h, wg_ref[:, lo:hi], wu_ref[:, lo:hi], wd_ref[lo:hi, :])
    o_ref[...] = acc


def _ffn(x2, h2, wg, wu, wd):
    T = x2.shape[0]
    tm = min(512, T)
    rows = pl.BlockSpec((tm, D_MODEL), lambda t: (t, 0))
    resident = lambda shape: pl.BlockSpec(shape, lambda t: (0, 0), pipeline_mode=pl.Buffered(1))
    return pl.pallas_call(
        _ffn_kernel,
        grid=(T // tm,),
        in_specs=[rows, rows, resident((D_MODEL, FFN_DIM)), resident((D_MODEL, FFN_DIM)),
                  resident((FFN_DIM, D_MODEL))],
        out_specs=rows,
        out_shape=jax.ShapeDtypeStruct((T, D_MODEL), f32),
        compiler_params=_cparams("parallel"),
        name="dense_swiglu",
    )(x2, h2, wg, wu, wd)


MOE_TILE = 256
RUN_ALIGN = SUBLANES
TILE_SLOTS = 2 * MOE_TILE + N_EXPERTS * RUN_ALIGN


def _pow2_pieces(limit):
    out, b = [], RUN_ALIGN
    while b <= limit:
        out.append(b)
        b *= 2
    return tuple(reversed(out))


def _pieces(value, limit):
    sizes = _pow2_pieces(limit)
    for b in sizes:
        done = value & (~(2 * b - 1) & (2 * sizes[0] - 1))
        yield (value & b) != 0, done, b


def _expert_runs(cnt_ref, off_ref, row_ref, tile):
    for e in range(N_EXPERTS):
        o = off_ref[tile * N_EXPERTS + e]
        d = row_ref[tile * N_EXPERTS + e]
        for needed, done, size in _pieces(cnt_ref[tile * N_EXPERTS + e], MOE_TILE):
            yield (needed, pl.multiple_of(o + done, RUN_ALIGN), pl.multiple_of(d + done, RUN_ALIGN),
                   size)


def _dispatch_kernel(cnt_ref, off_ref, row_ref, total_ref, pends_ref, slots_ref, h_ref, xs_ref,
                     xc_scr, zero_scr, sems, zero_sem, *, spare_blocks):
    i = pl.program_id(0)
    n = pl.num_programs(0)
    buf = i % 2

    def zero_fill(start):
        start = pl.multiple_of(start, EXPERT_ROWS)
        return pltpu.make_async_copy(zero_scr, xs_ref.at[pl.ds(start, EXPERT_ROWS)], zero_sem)

    @pl.when(i == 0)
    def _():
        zero_scr[...] = jnp.zeros(zero_scr.shape, f32)
        used = pends_ref[N_EXPERTS - 1]
        fills = [(jnp.maximum(pends_ref[e] - EXPERT_ROWS, 0), None) for e in range(N_EXPERTS)]
        fills += [(used + e * EXPERT_ROWS, used + e * EXPERT_ROWS < xs_ref.shape[0])
                  for e in range(spare_blocks)]
        for action in ("start", "wait"):
            for start, needed in fills:
                if needed is None:
                    getattr(zero_fill(start), action)()
                else:
                    @pl.when(needed)
                    def _():
                        getattr(zero_fill(start), action)()

    def tile_done(tile, b):
        for needed, _, size in _pieces(total_ref[tile], 2 * MOE_TILE):
            @pl.when(needed)
            def _():
                pltpu.make_async_copy(xc_scr.at[b, pl.ds(0, size)], xs_ref.at[pl.ds(0, size)],
                                      sems.at[b]).wait()

    @pl.when(i >= 2)
    def _():
        tile_done(i - 2, buf)

    place = _iota((TILE_SLOTS, MOE_TILE), 0)
    onehot = (place == slots_ref[0, 0:1, :]) | (place == slots_ref[0, 1:2, :])
    xc_scr[buf] = jnp.dot(jnp.where(onehot, 1.0, 0.0).astype(bf16), h_ref[...].astype(bf16),
                          preferred_element_type=f32)
    for needed, local, row, size in _expert_runs(cnt_ref, off_ref, row_ref, i):
        @pl.when(needed)
        def _():
            pltpu.make_async_copy(xc_scr.at[buf, pl.ds(local, size)], xs_ref.at[pl.ds(row, size)],
                                  sems.at[buf]).start()

    @pl.when(i == n - 1)
    def _():
        tile_done(i, buf)

        @pl.when(i >= 1)
        def _():
            tile_done(i - 1, 1 - buf)


def _dispatch(h2, slots_t, cnt, off, row, total, pends, rows):
    T = h2.shape[0]
    grid_spec = pltpu.PrefetchScalarGridSpec(
        num_scalar_prefetch=5,
        grid=(T // MOE_TILE,),
        in_specs=[
            pl.BlockSpec((1, 2, MOE_TILE), lambda t, *_: (t, 0, 0)),
            pl.BlockSpec((MOE_TILE, D_MODEL), lambda t, *_: (t, 0)),
        ],
        out_specs=pl.BlockSpec(memory_space=pl.ANY),
        scratch_shapes=[pltpu.VMEM((2, TILE_SLOTS, D_MODEL), f32),
                        pltpu.VMEM((EXPERT_ROWS, D_MODEL), f32),
                        pltpu.SemaphoreType.DMA((2,)), pltpu.SemaphoreType.DMA(())],
    )
    spare_blocks = rows // EXPERT_ROWS - 2 * T // EXPERT_ROWS
    return pl.pallas_call(
        functools.partial(_dispatch_kernel, spare_blocks=spare_blocks),
        grid_spec=grid_spec,
        out_shape=jax.ShapeDtypeStruct((rows, D_MODEL), f32),
        compiler_params=_cparams("arbitrary"),
        name="moe_dispatch",
    )(cnt, off, row, total, pends, slots_t, h2)


def _expert_kernel(be_ref, nv_ref, xs_ref, wg_ref, wu_ref, wd_ref, ys_ref):
    j = pl.program_id(0)

    @pl.when(j < nv_ref[0])
    def _():
        ys_ref[...] = _swiglu_block(xs_ref[...].astype(bf16), wg_ref[0], wu_ref[0], wd_ref[0])

    @pl.when(j >= nv_ref[0])
    def _():
        ys_ref[...] = jnp.zeros(ys_ref.shape, f32)


def _experts(block_e, n_valid, xs, wg, wu, wd):
    rows = xs.shape[0]
    grid_spec = pltpu.PrefetchScalarGridSpec(
        num_scalar_prefetch=2,
        grid=(rows // EXPERT_ROWS,),
        in_specs=[
            pl.BlockSpec((EXPERT_ROWS, D_MODEL), lambda j, be, nv: (jnp.minimum(j, nv[0] - 1), 0)),
            pl.BlockSpec((1, D_MODEL, EXPERT_DIM), lambda j, be, nv: (be[j], 0, 0)),
            pl.BlockSpec((1, D_MODEL, EXPERT_DIM), lambda j, be, nv: (be[j], 0, 0)),
            pl.BlockSpec((1, EXPERT_DIM, D_MODEL), lambda j, be, nv: (be[j], 0, 0)),
        ],
        out_specs=pl.BlockSpec((EXPERT_ROWS, D_MODEL), lambda j, be, nv: (j, 0)),
    )
    return pl.pallas_call(
        _expert_kernel,
        grid_spec=grid_spec,
        out_shape=jax.ShapeDtypeStruct((rows, D_MODEL), f32),
        compiler_params=_cparams("arbitrary"),
        name="moe_experts",
    )(block_e, n_valid, xs, wg, wu, wd)


def _combine_kernel(cnt_ref, off_ref, row_ref, total_ref, slots_ref, x_ref, route_ref, ys_ref,
                    o_ref, yc_scr, sems):
    i = pl.program_id(0)
    n = pl.num_programs(0)
    buf = i % 2

    def fetch(tile, b):
        for needed, local, row, size in _expert_runs(cnt_ref, off_ref, row_ref, tile):
            @pl.when(needed)
            def _():
                pltpu.make_async_copy(ys_ref.at[pl.ds(row, size)], yc_scr.at[b, pl.ds(local, size)],
                                      sems.at[b]).start()

    @pl.when(i == 0)
    def _():
        yc_scr[...] = jnp.zeros(yc_scr.shape, f32)
        fetch(0, 0)

    @pl.when(i + 1 < n)
    def _():
        fetch(i + 1, 1 - buf)

    for needed, _, size in _pieces(total_ref[i], 2 * MOE_TILE):
        @pl.when(needed)
        def _():
            pltpu.make_async_copy(ys_ref.at[pl.ds(0, size)], yc_scr.at[buf, pl.ds(0, size)],
                                  sems.at[buf]).wait()
    y = yc_scr[buf]
    y_hi = y.astype(bf16)
    y_lo = (y - y_hi.astype(f32)).astype(bf16)
    y_parts = jnp.concatenate([y_hi, y_lo], axis=0)
    place = _iota((MOE_TILE, 2 * TILE_SLOTS), 1)
    route = route_ref[...]
    out = x_ref[...]
    for k, w_lane in ((0, ROUTE_W1), (1, ROUTE_W2)):
        slot = slots_ref[:, k:k + 1]
        pick = jnp.where((place == slot) | (place == slot + TILE_SLOTS), 1.0, 0.0).astype(bf16)
        out = out + route[:, w_lane:w_lane + 1] * jnp.dot(pick, y_parts,
                                                           preferred_element_type=f32)
    o_ref[...] = out


def _combine(x2, route, slots, cnt, off, row, total, ys):
    T = x2.shape[0]
    grid_spec = pltpu.PrefetchScalarGridSpec(
        num_scalar_prefetch=4,
        grid=(T // MOE_TILE,),
        in_specs=[
            pl.BlockSpec((MOE_TILE, 2), lambda t, *_: (t, 0)),
            pl.BlockSpec((MOE_TILE, D_MODEL), lambda t, *_: (t, 0)),
            pl.BlockSpec((MOE_TILE, LANES), lambda t, *_: (t, 0)),
            pl.BlockSpec(memory_space=pl.ANY),
        ],
        out_specs=pl.BlockSpec((MOE_TILE, D_MODEL), lambda t, *_: (t, 0)),
        scratch_shapes=[pltpu.VMEM((2, TILE_SLOTS, D_MODEL), f32), pltpu.SemaphoreType.DMA((2,))],
    )
    return pl.pallas_call(
        _combine_kernel,
        grid_spec=grid_spec,
        out_shape=jax.ShapeDtypeStruct((T, D_MODEL), f32),
        compiler_params=_cparams("arbitrary"),
        name="moe_combine",
    )(cnt, off, row, total, slots, x2, route, ys)


def _moe(x2, h2, route, tile_counts, wg, wu, wd):
    T = x2.shape[0]
    nt = T // MOE_TILE
    cnt = tile_counts.reshape(nt, LANES)[:, :N_EXPERTS].astype(jnp.int32)
    run = (cnt + RUN_ALIGN - 1) // RUN_ALIGN * RUN_ALIGN
    off = jnp.cumsum(run, axis=1) - run
    before = jnp.cumsum(run, axis=0) - run
    padded = (run.sum(axis=0) + EXPERT_ROWS - 1) // EXPERT_ROWS * EXPERT_ROWS
    pends = jnp.cumsum(padded)
    row = (pends - padded)[None, :] + before
    n_blocks = -(-(2 * T + nt * N_EXPERTS * (RUN_ALIGN - 1)) // EXPERT_ROWS) + N_EXPERTS
    block_start = jnp.arange(n_blocks, dtype=jnp.int32) * EXPERT_ROWS
    block_e = jnp.minimum((pends[None, :] <= block_start[:, None]).sum(axis=1),
                          N_EXPERTS - 1).astype(jnp.int32)
    n_valid = (pends[-1:] // EXPERT_ROWS).astype(jnp.int32)
    slots = route[:, ROUTE_S1:ROUTE_S2 + 1].astype(jnp.int32)
    slots_t = slots.reshape(nt, MOE_TILE, 2).transpose(0, 2, 1)
    flat = lambda a: a.reshape(-1).astype(jnp.int32)
    tables = (flat(run), flat(off), flat(row), flat(run.sum(axis=1)))
    xs = _dispatch(h2, slots_t, *tables, pends.astype(jnp.int32), n_blocks * EXPERT_ROWS)
    ys = _experts(block_e, n_valid, xs, wg, wu, wd)
    return _combine(x2, route, slots, *tables, ys)


def _pad_cols(w, n):
    return jnp.pad(w, ((0, 0), (0, n - w.shape[1])))


def _layer_weights(w_in, c_w_uq, c_w_ukv):
    gate_end = N_BRANCH * D_MODEL
    a_end = gate_end + 2 * WIDTH
    b_end = a_end + 3 * WIDTH
    cq = w_in[:, b_end:b_end + C_Q_RANK]
    ckv = w_in[:, b_end + C_Q_RANK:b_end + C_Q_RANK + C_KV_RANK]
    kr = w_in[:, b_end + C_Q_RANK + C_KV_RANK:b_end + C_Q_RANK + C_KV_RANK + C_ROPE_DIM]
    d_start = b_end + C_Q_RANK + C_KV_RANK + C_ROPE_DIM
    kr_block = jnp.pad(kr, ((0, 0), (C_NOPE_DIM, C_HEAD_PAD - C_QK_DIM)))
    w_gate = w_in[:, :gate_end].astype(bf16)
    w = jnp.concatenate([w_in[:, gate_end:b_end], _pad_cols(cq, WIDTH), ckv, kr_block,
                         w_in[:, d_start:]], axis=1).astype(bf16)
    wuq = c_w_uq.reshape(C_Q_RANK, C_HEADS, C_QK_DIM)
    wuq = jnp.pad(wuq, ((0, WIDTH - C_Q_RANK), (0, 0), (0, C_HEAD_PAD - C_QK_DIM)))
    wuq = wuq.reshape(WIDTH, C_HEADS * C_HEAD_PAD).astype(bf16)
    wukv = c_w_ukv.reshape(C_KV_RANK, C_HEADS, C_NOPE_DIM + C_V_DIM)
    wuk = jnp.pad(wukv[:, :, :C_NOPE_DIM], ((0, 0), (0, 0), (0, C_HEAD_PAD - C_NOPE_DIM)))
    wuk = wuk.reshape(C_KV_RANK, C_HEADS * C_HEAD_PAD).astype(bf16)
    wuv = wukv[:, :, C_NOPE_DIM:].reshape(C_KV_RANK, C_HEADS * C_V_DIM).astype(bf16)
    return w_gate, w, wuq, wuk, wuv


def _rotation_constants():
    ch = jnp.arange(WIDTH)
    gsum_b = ((ch[:, None] // B_HEAD_DIM) == (ch[None, :] // B_HEAD_DIM)).astype(f32) / B_HEAD_DIM
    half_b = B_ROT_DIM // 2
    src, dst = ch[:, None], ch[None, :]
    same = (src // B_HEAD_DIM) == (dst // B_HEAD_DIM)
    ds_, dd = src % B_HEAD_DIM, dst % B_HEAD_DIM
    rot_b = jnp.where(same & (dd < half_b) & (ds_ == dd + half_b), -1.0,
                      jnp.where(same & (dd >= half_b) & (dd < B_ROT_DIM) & (ds_ == dd - half_b), 1.0, 0.0))
    lc = jnp.arange(C_HEAD_PAD)
    real = lc < C_QK_DIM
    gsum_c = (real[:, None] & real[None, :]).astype(f32) / C_QK_DIM
    half_c = C_ROPE_DIM // 2
    s, d = lc[:, None], lc[None, :]
    lo = (d >= C_NOPE_DIM) & (d < C_NOPE_DIM + half_c)
    hi = (d >= C_NOPE_DIM + half_c) & (d < C_QK_DIM)
    rot_c = jnp.where(lo & (s == d + half_c), -1.0, jnp.where(hi & (s == d - half_c), 1.0, 0.0))
    slot = jnp.arange(ATT_HEADS * HEAD_SLOT)
    place_b = ((slot[None, :] // HEAD_SLOT == ch[:, None] // B_HEAD_DIM)
               & (slot[None, :] % HEAD_SLOT == ch[:, None] % B_HEAD_DIM))
    return (gsum_b.astype(bf16), rot_b.astype(bf16), place_b.astype(bf16),
            gsum_c.astype(bf16), rot_c.astype(bf16))


def kernel(x, positions, attn_norm, ffn_norm, w_in, a_ln_g, a_ln_b, a_ws, a_bs, b_qn, b_kn, c_cq_norm, c_ckv_norm, c_w_uq, c_w_ukv, c_qn, c_kn, d_conv_w, d_conv_b, d_ln_g, d_ln_b, w_branch, w_out, ffn_w_gate, ffn_w_up, ffn_w_down, moe_router, moe_router_b, moe_w_gate, moe_w_up, moe_w_down):
    B, S, _ = x.shape
    T = B * S
    assert S % ATT_BLOCK == 0 and T % EXPERT_ROWS == 0
    x2 = x.reshape(T, D_MODEL)
    cos_b, sin_b, cos_c, sin_c = _rope_tables(positions)
    gsum_b, rot_b, place_b, gsum_c, rot_c = _rotation_constants()
    row = lambda v: v.reshape(1, -1)

    for l in range(DEPTH):
        w_gate, w, wuq, wuk, wuv = _layer_weights(w_in[l], c_w_uq[l], c_w_ukv[l])
        proj = _inproj(x2, row(attn_norm[l]), w)

        o_a = _sgu(proj, row(a_ln_g[l]), row(a_ln_b[l]), a_ws[l], a_bs[l].T)

        qb, kb, vtb = _moba_prep(proj, B, S, cos_b, sin_b,
                                 row(jnp.tile(b_qn[l], B_HEADS)), row(jnp.tile(b_kn[l], B_HEADS)),
                                 gsum_b, rot_b, place_b)
        o_b = _attention(qb, kb, vtb, B, S, "moba_attention")

        pad_gain = lambda g, n: row(jnp.pad(g, (0, n - g.shape[0])))
        qc, kc, vtc = _mla_prep(proj, B, S, cos_c, sin_c, pad_gain(c_cq_norm[l], WIDTH),
                                row(c_ckv_norm[l]), wuq, wuk, wuv,
                                pad_gain(c_qn[l], C_HEAD_PAD), pad_gain(c_kn[l], C_HEAD_PAD),
                                gsum_c, rot_c)
        o_c = _attention(qc, kc, vtc, B, S, "mla_attention")

        o_d = _conformer(proj, B, S, d_conv_w[l], row(d_conv_b[l]), row(d_ln_g[l]), row(d_ln_b[l]))

        merge_args = (x2, row(attn_norm[l]), w_gate, o_a, o_b, o_c, o_d, w_branch[l].astype(bf16),
                      w_out[l].astype(bf16), row(ffn_norm[l]))
        i = l // 2
        if l % 2 == 0:
            x2, h2 = _merge(*merge_args)
            x2 = _ffn(x2, h2, ffn_w_gate[i].astype(bf16), ffn_w_up[i].astype(bf16),
                      ffn_w_down[i].astype(bf16))
        else:
            router = (_pad_cols(moe_router[i], LANES), _pad_cols(row(moe_router_b[i]), LANES))
            x2, h2, route, tile_counts = _merge(*merge_args, router=router)
            x2 = _moe(x2, h2, route, tile_counts, moe_w_gate[i].astype(bf16),
                      moe_w_up[i].astype(bf16), moe_w_down[i].astype(bf16))
    return x2.reshape(B, S, D_MODEL)
```

```python
import functools

import jax
import jax.numpy as jnp
from jax import lax
from jax.experimental import pallas as pl
from jax.experimental.pallas import tpu as pltpu

D_MODEL = 1024
DEPTH = 4
N_BRANCH = 4
ROPE_THETA = 500000.0
NORM_EPS = 1e-6
NEG_INF = -1e30

WIDTH = 256
A_GROUPS = 4
A_GROUP_DIM = 64
A_CHUNK = 128
B_HEADS = 4
B_HEAD_DIM = 64
B_ROT_DIM = 16
MOBA_BLOCK = 256
MOBA_TOPK = 3
C_HEADS = 4
C_Q_RANK = 192
C_KV_RANK = 128
C_NOPE_DIM = 64
C_ROPE_DIM = 32
C_V_DIM = 64
C_QK_DIM = C_NOPE_DIM + C_ROPE_DIM
C_HEAD_PAD = 128
D_CONV = 31
FFN_DIM = 2816
N_EXPERTS = 8
EXPERT_DIM = 1408
EXPERT_ROWS = 512

ATT_BLOCK = 256
ATT_QBLOCK = 512
ATT_HEADS = 4
HEAD_SLOT = 128
V_AUG = C_V_DIM + 16
LOG2E = 1.4426950408889634
SM_ROWS = 32
LANES = 128
SUBLANES = 8
CONV_HALO = 32
CONV_ROWS = 64
LN_ROWS = 128

GATE_COLS = N_BRANCH * D_MODEL
COL_AU = 0
COL_AV = 256
COL_BQ = 512
COL_BK = 768
COL_BV = 1024
COL_CQ = 1280
COL_CKV = 1536
COL_KR = 1664
COL_DA = 1792
COL_DG = 2048
PROJ_COLS = 2304

VMEM_LIMIT = 56 * 1024 * 1024

bf16 = jnp.bfloat16
f32 = jnp.float32


def _cparams(*sem):
    return pltpu.CompilerParams(dimension_semantics=sem, vmem_limit_bytes=VMEM_LIMIT)


def _rms(x, n):
    return x * lax.rsqrt(jnp.sum(x * x, axis=-1, keepdims=True) * (1.0 / n) + NORM_EPS)


def _layernorm(x, g, b):
    mu = jnp.mean(x, axis=-1, keepdims=True)
    xc = x - mu
    return xc * lax.rsqrt(jnp.mean(xc * xc, axis=-1, keepdims=True) + NORM_EPS) * g + b


def _sigmoid(x):
    return 0.5 * jnp.tanh(0.5 * x) + 0.5


def _iota(shape, dim):
    return lax.broadcasted_iota(jnp.int32, shape, dim)


def _rope_table_kernel(pos_ref, inv_ref, eb_ref, ec_ref, cb_ref, sb_ref, cc_ref, sc_ref):
    ang = pos_ref[0].astype(f32) * inv_ref[...]
    c = jnp.cos(ang)
    s = jnp.sin(ang)
    hi = lax.Precision.HIGHEST
    cb_ref[0] = jnp.dot(c, eb_ref[...], precision=hi, preferred_element_type=f32)
    sb_ref[0] = jnp.dot(s, eb_ref[...], precision=hi, preferred_element_type=f32)
    cc_ref[0] = jnp.dot(c, ec_ref[...], precision=hi, preferred_element_type=f32)
    sc_ref[0] = jnp.dot(s, ec_ref[...], precision=hi, preferred_element_type=f32)


def _rope_tables(positions):
    B, S = positions.shape
    R = min(512, S)
    nb = B_ROT_DIM // 2
    nc = C_ROPE_DIM // 2
    inv_b = f32(ROPE_THETA) ** (-jnp.arange(0, B_ROT_DIM, 2, dtype=f32) / B_ROT_DIM)
    inv_c = f32(ROPE_THETA) ** (-jnp.arange(0, C_ROPE_DIM, 2, dtype=f32) / C_ROPE_DIM)
    inv = jnp.zeros((1, LANES), f32).at[0, :nb].set(inv_b).at[0, nb:nb + nc].set(inv_c)
    zero_lane = LANES - 1
    ch = jnp.arange(WIDTH)
    d = ch % B_HEAD_DIM
    src_b = jnp.where(d < B_ROT_DIM, d % nb, zero_lane)
    eb = (jnp.arange(LANES)[:, None] == src_b[None, :]).astype(f32)
    dc = jnp.arange(C_HEAD_PAD)
    in_rope = (dc >= C_NOPE_DIM) & (dc < C_QK_DIM)
    src_c = jnp.where(in_rope, nb + (dc - C_NOPE_DIM) % nc, zero_lane)
    ec = (jnp.arange(LANES)[:, None] == src_c[None, :]).astype(f32)
    pos3 = positions.reshape(B, S, 1)
    out = pl.pallas_call(
        _rope_table_kernel,
        grid=(B, S // R),
        in_specs=[
            pl.BlockSpec((1, R, 1), lambda b, r: (b, r, 0)),
            pl.BlockSpec((1, LANES), lambda b, r: (0, 0)),
            pl.BlockSpec((LANES, WIDTH), lambda b, r: (0, 0)),
            pl.BlockSpec((LANES, C_HEAD_PAD), lambda b, r: (0, 0)),
        ],
        out_specs=[
            pl.BlockSpec((1, R, WIDTH), lambda b, r: (b, r, 0)),
            pl.BlockSpec((1, R, WIDTH), lambda b, r: (b, r, 0)),
            pl.BlockSpec((1, R, C_HEAD_PAD), lambda b, r: (b, r, 0)),
            pl.BlockSpec((1, R, C_HEAD_PAD), lambda b, r: (b, r, 0)),
        ],
        out_shape=[
            jax.ShapeDtypeStruct((B, S, WIDTH), f32),
            jax.ShapeDtypeStruct((B, S, WIDTH), f32),
            jax.ShapeDtypeStruct((B, S, C_HEAD_PAD), f32),
            jax.ShapeDtypeStruct((B, S, C_HEAD_PAD), f32),
        ],
        compiler_params=_cparams("parallel", "parallel"),
        name="rope_tables",
    )(pos3, inv, eb, ec)
    return out


PROJ_CHUNKS = ((0, COL_CQ), (COL_CQ, PROJ_COLS))


def _inproj_kernel(x_ref, g_ref, w_ref, o_ref):
    h = (_rms(x_ref[...], D_MODEL) * g_ref[...]).astype(bf16)
    for lo, hi in PROJ_CHUNKS:
        o_ref[:, lo:hi] = jnp.dot(h, w_ref[:, lo:hi], preferred_element_type=f32).astype(bf16)


def _inproj(x2, g, w):
    T = x2.shape[0]
    tm = min(512, T)
    return pl.pallas_call(
        _inproj_kernel,
        grid=(T // tm,),
        in_specs=[
            pl.BlockSpec((tm, D_MODEL), lambda t: (t, 0)),
            pl.BlockSpec((1, D_MODEL), lambda t: (0, 0)),
            pl.BlockSpec((D_MODEL, PROJ_COLS), lambda t: (0, 0)),
        ],
        out_specs=pl.BlockSpec((tm, PROJ_COLS), lambda t: (t, 0)),
        out_shape=jax.ShapeDtypeStruct((T, PROJ_COLS), bf16),
        compiler_params=_cparams("parallel"),
        name="inproj",
    )(x2, g, w)


def _sgu_kernel(u_ref, v_ref, g_ref, b_ref, ws_ref, bst_ref, o_ref, *, chunks):
    row = _iota((A_CHUNK, A_CHUNK), 0)
    col = _iota((A_CHUNK, A_CHUNK), 1)
    causal = col <= row
    group = _iota((A_CHUNK, WIDTH), 1) // A_GROUP_DIM
    wm = [jnp.where(causal, ws_ref[g], 0.0).astype(bf16) for g in range(A_GROUPS)]
    bias = jnp.zeros((A_CHUNK, WIDTH), f32)
    for g in range(A_GROUPS):
        bias = jnp.where(group == g, bst_ref[:, g:g + 1], bias)
    for c in range(chunks):
        rows = slice(c * A_CHUNK, (c + 1) * A_CHUNK)
        vn = _layernorm(v_ref[rows, :].astype(f32), g_ref[...], b_ref[...]).astype(bf16)
        z = bias
        for g in range(A_GROUPS):
            zg = jnp.dot(wm[g], vn, preferred_element_type=f32)
            z = z + jnp.where(group == g, zg, 0.0)
        o_ref[rows, :] = (u_ref[rows, :].astype(f32) * z).astype(bf16)


def _sgu(proj, ln_g, ln_b, ws, bs_t):
    T = proj.shape[0]
    ta = min(512, T)
    cu, cv = COL_AU // WIDTH, COL_AV // WIDTH
    return pl.pallas_call(
        functools.partial(_sgu_kernel, chunks=ta // A_CHUNK),
        grid=(T // ta,),
        in_specs=[
            pl.BlockSpec((ta, WIDTH), lambda t: (t, cu)),
            pl.BlockSpec((ta, WIDTH), lambda t: (t, cv)),
            pl.BlockSpec((1, WIDTH), lambda t: (0, 0)),
            pl.BlockSpec((1, WIDTH), lambda t: (0, 0)),
            pl.BlockSpec((A_GROUPS, A_CHUNK, A_CHUNK), lambda t: (0, 0, 0)),
            pl.BlockSpec((A_CHUNK, A_GROUPS), lambda t: (0, 0)),
        ],
        out_specs=pl.BlockSpec((ta, WIDTH), lambda t: (t, 0)),
        out_shape=jax.ShapeDtypeStruct((T, WIDTH), bf16),
        compiler_params=_cparams("parallel"),
        name="sgu",
    )(proj, proj, ln_g, ln_b, ws, bs_t)


def _conv_kernel(a_ref, g_ref, w_ref, cb_ref, lg_ref, lb_ref, o_ref, z_scr, *, seq):
    first = CONV_HALO - (D_CONV - 1)
    z_scr[0, 0:CONV_HALO, :] = jnp.zeros((CONV_HALO, WIDTH), f32)
    z_scr[0, CONV_HALO + seq:, :] = jnp.zeros((CONV_ROWS, WIDTH), f32)

    def glu(c, carry):
        base = pl.multiple_of(c * LN_ROWS, LN_ROWS)
        rows = pl.ds(base, LN_ROWS)
        z_scr[0, pl.ds(base + CONV_HALO, LN_ROWS), :] = (
            a_ref[rows, :].astype(f32) * _sigmoid(g_ref[rows, :].astype(f32)))
        return carry

    lax.fori_loop(0, seq // LN_ROWS, glu, 0)

    def shift(c, carry):
        base = pl.multiple_of(c * CONV_ROWS, CONV_ROWS)
        win = z_scr[0, pl.ds(base, CONV_ROWS + SUBLANES), :]
        for r in range(1, SUBLANES):
            rolled = pltpu.roll(win, shift=CONV_ROWS + SUBLANES - r, axis=0)
            z_scr[r, pl.ds(base, CONV_ROWS), :] = rolled[:CONV_ROWS]
        return carry

    lax.fori_loop(0, seq // CONV_ROWS + 1, shift, 0)

    def taps(c, carry):
        base = pl.multiple_of(c * CONV_ROWS, CONV_ROWS)
        acc = jnp.zeros((CONV_ROWS, WIDTH), f32) + cb_ref[...]
        for k in range(D_CONV):
            off = first + k
            lo = pl.multiple_of(base + off // SUBLANES * SUBLANES, SUBLANES)
            acc = acc + w_ref[k:k + 1, :] * z_scr[off % SUBLANES, pl.ds(lo, CONV_ROWS), :]
        z_scr[1, pl.ds(base, CONV_ROWS), :] = acc
        return carry

    lax.fori_loop(0, seq // CONV_ROWS, taps, 0)

    def norm(c, carry):
        base = pl.multiple_of(c * LN_ROWS, LN_ROWS)
        y = _layernorm(z_scr[1, pl.ds(base, LN_ROWS), :], lg_ref[...], lb_ref[...])
        o_ref[pl.ds(base, LN_ROWS), :] = (y * _sigmoid(y)).astype(bf16)
        return carry

    lax.fori_loop(0, seq // LN_ROWS, norm, 0, unroll=2)


def _conformer(proj, B, S, conv_w, conv_b, ln_g, ln_b):
    ca, cg = COL_DA // WIDTH, COL_DG // WIDTH
    return pl.pallas_call(
        functools.partial(_conv_kernel, seq=S),
        grid=(B,),
        in_specs=[
            pl.BlockSpec((S, WIDTH), lambda b: (b, ca)),
            pl.BlockSpec((S, WIDTH), lambda b: (b, cg)),
            pl.BlockSpec((D_CONV, WIDTH), lambda b: (0, 0)),
            pl.BlockSpec((1, WIDTH), lambda b: (0, 0)),
            pl.BlockSpec((1, WIDTH), lambda b: (0, 0)),
            pl.BlockSpec((1, WIDTH), lambda b: (0, 0)),
        ],
        out_specs=pl.BlockSpec((S, WIDTH), lambda b: (b, 0)),
        out_shape=jax.ShapeDtypeStruct((B * S, WIDTH), bf16),
        scratch_shapes=[pltpu.VMEM((SUBLANES, CONV_HALO + S + CONV_ROWS, WIDTH), f32)],
        compiler_params=_cparams("parallel"),
        name="conformer_conv",
    )(proj, proj, conv_w, conv_b, ln_g, ln_b)


def _value_rows(v_tok):
    vt = v_tok.T
    ones = jnp.ones((V_AUG - C_V_DIM, vt.shape[1]), f32)
    rows = []
    for h in range(ATT_HEADS):
        rows += [vt[h * C_V_DIM:(h + 1) * C_V_DIM], ones]
    return jnp.concatenate(rows, axis=0).astype(bf16)


def _moba_prep_kernel(q_ref, k_ref, v_ref, cos_ref, sin_ref, qg_ref, kg_ref, gsum_ref, rot_ref,
                      place_ref, qo_ref, ko_ref, vt_ref, kmean_scr, *, nblk):
    c = pl.program_id(1)

    @pl.when(c == 0)
    def _():
        kmean_scr[...] = jnp.zeros(kmean_scr.shape, f32)

    cos = cos_ref[0]
    sin = sin_ref[0]

    def norm_rope(x, gain):
        ss = jnp.dot((x * x).astype(bf16), gsum_ref[...], preferred_element_type=f32)
        xn = x * lax.rsqrt(ss + NORM_EPS) * gain
        xr = jnp.dot(xn.astype(bf16), rot_ref[...], preferred_element_type=f32)
        return xn * cos + xr * sin

    q = norm_rope(q_ref[...].astype(f32), qg_ref[...])
    k = norm_rope(k_ref[...].astype(f32), kg_ref[...])
    slot_lane = _iota((ATT_BLOCK, ATT_HEADS * HEAD_SLOT), 1) % HEAD_SLOT
    k_slots = jnp.dot(k.astype(bf16), place_ref[...], preferred_element_type=f32)
    ko_ref[0, 0] = jnp.where(slot_lane == B_HEAD_DIM + c, 1.0, k_slots).astype(bf16)
    vt_ref[0, 0] = _value_rows(v_ref[...].astype(f32))

    head = _iota((nblk, WIDTH), 1) // B_HEAD_DIM
    blk = _iota((nblk, ATT_BLOCK), 0)
    past = blk < c
    kmean = kmean_scr[...]
    zeros_lo = jnp.zeros((B_HEAD_DIM, ATT_BLOCK), f32)
    zeros_hi = jnp.zeros((HEAD_SLOT - B_HEAD_DIM - nblk, ATT_BLOCK), f32)
    bias_rows = []
    km_all = jnp.concatenate([jnp.where(head == h, kmean, 0.0) for h in range(B_HEADS)], axis=0)
    sc_all = lax.dot_general(km_all, q, (((1,), (1,)), ((), ())), precision=lax.Precision.HIGHEST,
                             preferred_element_type=f32)
    for h in range(B_HEADS):
        sc = jnp.where(past, sc_all[h * nblk:(h + 1) * nblk], NEG_INF)
        rank = jnp.zeros((nblk, ATT_BLOCK), jnp.int32)
        for j in range(nblk):
            other = sc[j:j + 1, :]
            ahead = (other > sc) | ((other == sc) & (j < blk))
            rank = rank + ahead.astype(jnp.int32)
        visible = (past & (rank < MOBA_TOPK)) | (blk == c)
        bias_rows += [zeros_lo, jnp.where(visible, 0.0, NEG_INF), zeros_hi]
    bias = jnp.concatenate(bias_rows, axis=0)
    q_slots = jnp.dot((q * (B_HEAD_DIM ** -0.5 * LOG2E)).astype(bf16), place_ref[...],
                      preferred_element_type=f32)
    qo_ref[0] = (q_slots.T + bias).astype(bf16)

    kmean_scr[pl.ds(c, 1), :] = jnp.mean(k, axis=0, keepdims=True)


def _moba_prep(proj, B, S, cos_b, sin_b, qg, kg, gsum, rot, place):
    nblk = S // ATT_BLOCK
    assert nblk <= HEAD_SLOT - B_HEAD_DIM
    HD = ATT_HEADS * HEAD_SLOT
    cq, ck, cv = COL_BQ // WIDTH, COL_BK // WIDTH, COL_BV // WIDTH
    tile = lambda col: pl.BlockSpec((ATT_BLOCK, WIDTH), lambda b, c: (b * nblk + c, col))
    table = pl.BlockSpec((1, ATT_BLOCK, WIDTH), lambda b, c: (b, c, 0))
    const = lambda shape: pl.BlockSpec(shape, lambda b, c: (0,) * len(shape))
    return pl.pallas_call(
        functools.partial(_moba_prep_kernel, nblk=nblk),
        grid=(B, nblk),
        in_specs=[tile(cq), tile(ck), tile(cv), table, table,
                  const((1, WIDTH)), const((1, WIDTH)), const((WIDTH, WIDTH)), const((WIDTH, WIDTH)),
                  const((WIDTH, HD))],
        out_specs=[
            pl.BlockSpec((1, HD, ATT_BLOCK), lambda b, c: (b, 0, c)),
            pl.BlockSpec((1, 1, ATT_BLOCK, HD), lambda b, c: (b, c, 0, 0)),
            pl.BlockSpec((1, 1, ATT_HEADS * V_AUG, ATT_BLOCK), lambda b, c: (b, c, 0, 0)),
        ],
        out_shape=[
            jax.ShapeDtypeStruct((B, HD, S), bf16),
            jax.ShapeDtypeStruct((B, nblk, ATT_BLOCK, HD), bf16),
            jax.ShapeDtypeStruct((B, nblk, ATT_HEADS * V_AUG, ATT_BLOCK), bf16),
        ],
        scratch_shapes=[pltpu.VMEM((nblk, WIDTH), f32)],
        compiler_params=_cparams("parallel", "arbitrary"),
        name="moba_prep",
    )(proj, proj, proj, cos_b, sin_b, qg, kg, gsum, rot, place)


def _mla_prep_kernel(cq_ref, ckv_ref, kr_ref, cos_ref, sin_ref, cqg_ref, ckvg_ref, wuq_ref, wuk_ref,
                     wuv_ref, qg_ref, kg_ref, gsum_ref, rot_ref, qo_ref, ko_ref, vt_ref):
    cqn = (_rms(cq_ref[...].astype(f32), C_Q_RANK) * cqg_ref[...]).astype(bf16)
    ckvn = (_rms(ckv_ref[...].astype(f32), C_KV_RANK) * ckvg_ref[...]).astype(bf16)
    q = jnp.dot(cqn, wuq_ref[...], preferred_element_type=f32)
    k = jnp.dot(ckvn, wuk_ref[...], preferred_element_type=f32)
    v = jnp.dot(ckvn, wuv_ref[...], preferred_element_type=f32)
    kr = kr_ref[...].astype(f32)
    cos = cos_ref[0]
    sin = sin_ref[0]

    def norm_rope(x, gain):
        ss = jnp.dot((x * x).astype(bf16), gsum_ref[...], preferred_element_type=f32)
        xn = x * lax.rsqrt(ss + NORM_EPS) * gain
        xr = jnp.dot(xn.astype(bf16), rot_ref[...], preferred_element_type=f32)
        return xn * cos + xr * sin

    for h in range(C_HEADS):
        sl = slice(h * C_HEAD_PAD, (h + 1) * C_HEAD_PAD)
        qh = norm_rope(q[:, sl], qg_ref[...])
        kh = norm_rope(k[:, sl] + kr, kg_ref[...])
        qo_ref[0, sl, :] = (qh * (C_QK_DIM ** -0.5 * LOG2E)).T.astype(bf16)
        ko_ref[0, 0, :, sl] = kh.astype(bf16)
    vt_ref[0, 0] = _value_rows(v)


def _mla_prep(proj, B, S, cos_c, sin_c, cqg, ckvg, wuq, wuk, wuv, qg, kg, gsum, rot):
    nblk = S // ATT_BLOCK
    HD = C_HEADS * C_HEAD_PAD
    table = pl.BlockSpec((1, ATT_BLOCK, C_HEAD_PAD), lambda b, c: (b, c, 0))
    const = lambda shape: pl.BlockSpec(shape, lambda b, c: (0,) * len(shape))
    return pl.pallas_call(
        _mla_prep_kernel,
        grid=(B, nblk),
        in_specs=[
            pl.BlockSpec((ATT_BLOCK, WIDTH), lambda b, c: (b * nblk + c, COL_CQ // WIDTH)),
            pl.BlockSpec((ATT_BLOCK, LANES), lambda b, c: (b * nblk + c, COL_CKV // LANES)),
            pl.BlockSpec((ATT_BLOCK, LANES), lambda b, c: (b * nblk + c, COL_KR // LANES)),
            table, table,
            const((1, WIDTH)), const((1, C_KV_RANK)), const((WIDTH, HD)), const((C_KV_RANK, HD)),
            const((C_KV_RANK, WIDTH)), const((1, C_HEAD_PAD)), const((1, C_HEAD_PAD)),
            const((C_HEAD_PAD, C_HEAD_PAD)), const((C_HEAD_PAD, C_HEAD_PAD)),
        ],
        out_specs=[
            pl.BlockSpec((1, HD, ATT_BLOCK), lambda b, c: (b, 0, c)),
            pl.BlockSpec((1, 1, ATT_BLOCK, HD), lambda b, c: (b, c, 0, 0)),
            pl.BlockSpec((1, 1, ATT_HEADS * V_AUG, ATT_BLOCK), lambda b, c: (b, c, 0, 0)),
        ],
        out_shape=[
            jax.ShapeDtypeStruct((B, HD, S), bf16),
            jax.ShapeDtypeStruct((B, nblk, ATT_BLOCK, HD), bf16),
            jax.ShapeDtypeStruct((B, nblk, ATT_HEADS * V_AUG, ATT_BLOCK), bf16),
        ],
        compiler_params=_cparams("parallel", "parallel"),
        name="mla_prep",
    )(proj, proj, proj, cos_c, sin_c, cqg, ckvg, wuq, wuk, wuv, qg, kg, gsum, rot)


def _attn_kernel(q_ref, k_ref, vt_ref, o_ref, *scratch):
    H = ATT_HEADS
    s_scr, p_scr, a_scr = ([scratch[(2 * n + par) * H:(2 * n + par + 1) * H] for par in range(2)]
                           for n in range(3))
    acc_scr, m_scr = (scratch[n * H:(n + 1) * H] for n in (6, 7))
    out_scr = scratch[8 * H]
    top_scr = [scratch[8 * H + 1 + par * H:8 * H + 1 + (par + 1) * H] for par in range(2)]
    pairs = pl.program_id(1) * (ATT_QBLOCK // ATT_BLOCK // 2)
    chunks = ATT_BLOCK // SM_ROWS

    def scores(j, par, h):
        slot = slice(h * HEAD_SLOT, (h + 1) * HEAD_SLOT)
        s = jnp.dot(k_ref[0, j, :, slot], q_ref[0, slot, :],
                    preferred_element_type=f32)
        s_scr[par][h][...] = s
        top_scr[par][h][...] = jnp.max(s, axis=0, keepdims=True)

    def softmax(par, h, diag_offset=None):
        def rows(r):
            s = s_scr[par][h][r * SM_ROWS:(r + 1) * SM_ROWS, :]
            if diag_offset is not None:
                kpos = _iota((SM_ROWS, ATT_QBLOCK), 0) + (r * SM_ROWS + diag_offset)
                s = jnp.where(kpos <= _iota((SM_ROWS, ATT_QBLOCK), 1), s, NEG_INF)
            return s
        if diag_offset is None:
            top = top_scr[par][h][...]
        else:
            mx = rows(0)
            for r in range(1, chunks):
                mx = jnp.maximum(mx, rows(r))
            top = jnp.max(mx, axis=0, keepdims=True)
        m_old = m_scr[h][...]
        m_new = jnp.maximum(m_old, top)
        a_scr[par][h][...] = jnp.exp2(m_old - m_new)
        m_scr[h][...] = m_new
        for r in range(chunks):
            p_scr[par][h][r * SM_ROWS:(r + 1) * SM_ROWS, :] = (
                jnp.exp2((rows(r) - m_new).astype(bf16)))

    def values(j, par, h):
        pv = jnp.dot(vt_ref[0, j, h * V_AUG:(h + 1) * V_AUG, :], p_scr[par][h][...],
                     preferred_element_type=f32)
        acc_scr[h][...] = a_scr[par][h][...] * acc_scr[h][...] + pv

    for h in range(H):
        m_scr[h][...] = jnp.full(m_scr[h].shape, NEG_INF, f32)
        a_scr[1][h][...] = jnp.zeros(a_scr[1][h].shape, f32)
        acc_scr[h][...] = jnp.zeros(acc_scr[h].shape, f32)
        p_scr[1][h][...] = jnp.zeros(p_scr[1][h].shape, bf16)
        scores(0, 0, h)

    def step(b, par, diag_offset=None, last=False):
        for h in range(H):
            if not last:
                scores(b + 1, 1 - par, h)
            values(jnp.maximum(b - 1, 0), 1 - par, h)
            softmax(par, h, diag_offset)

    def past_pair(t, carry):
        step(2 * t, 0)
        step(2 * t + 1, 1)
        return carry

    lax.fori_loop(0, pairs, past_pair, 0)
    step(2 * pairs, 0, diag_offset=0)
    step(2 * pairs + 1, 1, diag_offset=ATT_BLOCK, last=True)

    for h in range(H):
        values(2 * pairs + 1, 1, h)
        a = acc_scr[h][...]
        out_scr[h * C_V_DIM:(h + 1) * C_V_DIM, :] = a[:C_V_DIM] / a[C_V_DIM:C_V_DIM + 1]
    o_ref[...] = out_scr[...].T.astype(bf16)


def _attention(q, k, vt, B, S, name):
    nblk = S // ATT_BLOCK
    nq = S // ATT_QBLOCK
    HD = ATT_HEADS * HEAD_SLOT
    return pl.pallas_call(
        _attn_kernel,
        grid=(B, nq),
        in_specs=[
            pl.BlockSpec((1, HD, ATT_QBLOCK), lambda b, i: (b, 0, i)),
            pl.BlockSpec((1, nblk, ATT_BLOCK, HD), lambda b, i: (b, 0, 0, 0)),
            pl.BlockSpec((1, nblk, ATT_HEADS * V_AUG, ATT_BLOCK), lambda b, i: (b, 0, 0, 0)),
        ],
        out_specs=pl.BlockSpec((ATT_QBLOCK, WIDTH), lambda b, i: (b * nq + i, 0)),
        out_shape=jax.ShapeDtypeStruct((B * S, WIDTH), bf16),
        scratch_shapes=(
            [pltpu.VMEM((ATT_BLOCK, ATT_QBLOCK), f32) for _ in range(2 * ATT_HEADS)]
            + [pltpu.VMEM((ATT_BLOCK, ATT_QBLOCK), bf16) for _ in range(2 * ATT_HEADS)]
            + [pltpu.VMEM((1, ATT_QBLOCK), f32) for _ in range(2 * ATT_HEADS)]
            + [pltpu.VMEM((V_AUG, ATT_QBLOCK), f32) for _ in range(ATT_HEADS)]
            + [pltpu.VMEM((1, ATT_QBLOCK), f32) for _ in range(ATT_HEADS)]
            + [pltpu.VMEM((WIDTH, ATT_QBLOCK), f32)]
            + [pltpu.VMEM((1, ATT_QBLOCK), f32) for _ in range(2 * ATT_HEADS)]),
        compiler_params=_cparams("parallel", "arbitrary"),
        name=name,
    )(q, k, vt)


ROUTE_E1, ROUTE_E2, ROUTE_W1, ROUTE_W2, ROUTE_S1, ROUTE_S2 = range(6)


def _route_rows(h, wr_ref, br_ref, route_ref, tcount_ref):
    tm = h.shape[0]
    lane = _iota((tm, LANES), 1)
    h_hi = h.astype(bf16)
    h_lo = (h - h_hi.astype(f32)).astype(bf16)
    w = wr_ref[...]
    w_hi = w.astype(bf16)
    w_lo = (w - w_hi.astype(f32)).astype(bf16)
    logits = (jnp.dot(h_hi, w_hi, preferred_element_type=f32)
              + (jnp.dot(h_hi, w_lo, preferred_element_type=f32)
                 + jnp.dot(h_lo, w_hi, preferred_element_type=f32))) + br_ref[...]
    logits = jnp.where(lane < N_EXPERTS, logits, NEG_INF)
    m1 = jnp.max(logits, axis=-1, keepdims=True)
    e1 = jnp.min(jnp.where(logits == m1, lane, LANES), axis=-1, keepdims=True)
    rest = jnp.where(lane == e1, NEG_INF, logits)
    m2 = jnp.max(rest, axis=-1, keepdims=True)
    e2 = jnp.min(jnp.where(rest == m2, lane, LANES), axis=-1, keepdims=True)
    ex = jnp.exp(m2 - m1)
    w1 = 1.0 / (1.0 + ex)
    w2 = ex / (1.0 + ex)
    chosen = ((lane == e1) | (lane == e2)).astype(f32)
    row, col = _iota((tm, tm), 0), _iota((tm, tm), 1)
    earlier = ((col < row) & (col // MOE_TILE == row // MOE_TILE)).astype(bf16)
    before = jnp.dot(earlier, chosen.astype(bf16), preferred_element_type=f32)
    upper = (_iota((LANES, LANES), 0) < _iota((LANES, LANES), 1)).astype(bf16)
    token_tile = _iota((tm, LANES), 0) // MOE_TILE
    place = before
    for r in range(tm // MOE_TILE):
        cnt = jnp.sum(chosen[r * MOE_TILE:(r + 1) * MOE_TILE], axis=0, keepdims=True)
        tcount_ref[0, r:r + 1, :] = cnt
        run = jnp.floor((cnt + (RUN_ALIGN - 1)) * (1.0 / RUN_ALIGN)) * RUN_ALIGN
        off = jnp.dot(jnp.broadcast_to(run, (SUBLANES, LANES)).astype(bf16), upper,
                      preferred_element_type=f32)
        place = place + jnp.where(token_tile == r, off[0:1], 0.0)
    s1 = jnp.sum(jnp.where(lane == e1, place, 0.0), axis=-1, keepdims=True)
    s2 = jnp.sum(jnp.where(lane == e2, place, 0.0), axis=-1, keepdims=True)
    out = jnp.zeros((tm, LANES), f32)
    for idx, val in ((ROUTE_E1, e1.astype(f32)), (ROUTE_E2, e2.astype(f32)), (ROUTE_W1, w1),
                     (ROUTE_W2, w2), (ROUTE_S1, s1), (ROUTE_S2, s2)):
        out = jnp.where(lane == idx, val, out)
    route_ref[...] = out


def _merge_kernel(x_ref, ag_ref, wgate_ref, oa_ref, ob_ref, oc_ref, od_ref,
                  wb_ref, wo_ref, ng_ref, *rest, routed):
    branches = (oa_ref, ob_ref, oc_ref, od_ref)
    xo_ref, ho_ref = rest[2:4] if routed else rest
    tm = x_ref.shape[0]
    halves = [slice(r * (tm // 2), (r + 1) * (tm // 2)) for r in range(2)]
    gated = []
    for rows in halves:
        hin = (_rms(x_ref[rows, :], D_MODEL) * ag_ref[...]).astype(bf16)
        gates = [jnp.dot(hin, wgate_ref[:, n * D_MODEL:(n + 1) * D_MODEL],
                         preferred_element_type=f32) for n in range(N_BRANCH)]
        projected = [jnp.dot(o_ref[rows, :], wb_ref[n], preferred_element_type=f32)
                     for n, o_ref in enumerate(branches)]
        gated.append((gates, projected))
    hs = []
    for rows, (gates, projected) in zip(halves, gated):
        y = None
        for gate, br in zip(gates, projected):
            t = _sigmoid(gate) * br
            y = t if y is None else y + t
        x = x_ref[rows, :] + jnp.dot(y.astype(bf16), wo_ref[...], preferred_element_type=f32)
        xo_ref[rows, :] = x
        hs.append(_rms(x, D_MODEL) * ng_ref[...])
    if routed:
        wr_ref, br_ref, _, _, route_ref, tcount_ref = rest
        h = jnp.concatenate(hs, axis=0)
        ho_ref[...] = h
        _route_rows(h, wr_ref, br_ref, route_ref, tcount_ref)
    else:
        for rows, h in zip(halves, hs):
            ho_ref[rows, :] = h.astype(bf16)


def _merge(x2, ag, wgate, oa, ob, oc, od, wb, wo, ng, router=None):
    T = x2.shape[0]
    tm = min(512, T)
    routed = router is not None
    branch = pl.BlockSpec((tm, WIDTH), lambda t: (t, 0))
    rows = pl.BlockSpec((tm, D_MODEL), lambda t: (t, 0))
    resident = lambda shape: pl.BlockSpec(shape, lambda t: (0,) * len(shape),
                                          pipeline_mode=pl.Buffered(1))
    in_specs = [
        rows, pl.BlockSpec((1, D_MODEL), lambda t: (0, 0)), resident((D_MODEL, GATE_COLS)),
        branch, branch, branch, branch,
        resident((N_BRANCH, WIDTH, D_MODEL)), resident((D_MODEL, D_MODEL)),
        pl.BlockSpec((1, D_MODEL), lambda t: (0, 0)),
    ]
    args = [x2, ag, wgate, oa, ob, oc, od, wb, wo, ng]
    out_specs = [rows, rows]
    out_shape = [jax.ShapeDtypeStruct((T, D_MODEL), f32),
                 jax.ShapeDtypeStruct((T, D_MODEL), f32 if routed else bf16)]
    if routed:
        in_specs += [pl.BlockSpec((D_MODEL, LANES), lambda t: (0, 0)),
                     pl.BlockSpec((1, LANES), lambda t: (0, 0))]
        args += list(router)
        out_specs += [pl.BlockSpec((tm, LANES), lambda t: (t, 0)),
                      pl.BlockSpec((1, tm // MOE_TILE, LANES), lambda t: (t, 0, 0))]
        out_shape += [jax.ShapeDtypeStruct((T, LANES), f32),
                      jax.ShapeDtypeStruct((T // tm, tm // MOE_TILE, LANES), f32)]
    return pl.pallas_call(
        functools.partial(_merge_kernel, routed=routed),
        grid=(T // tm,),
        in_specs=in_specs,
        out_specs=out_specs,
        out_shape=out_shape,
        compiler_params=_cparams("parallel"),
        name="merge_route" if routed else "merge",
    )(*args)


def _swiglu_block(h, wg, wu, wd):
    a = jnp.dot(h, wg, preferred_element_type=f32)
    b = jnp.dot(h, wu, preferred_element_type=f32)
    t = (a * _sigmoid(a) * b).astype(bf16)
    return jnp.dot(t, wd, preferred_element_type=f32)


FFN_CHUNKS = ((0, 1536), (1536, FFN_DIM))


def _ffn_kernel(x_ref, h_ref, wg_ref, wu_ref, wd_ref, o_ref):
    h = h_ref[...]
    acc = x_ref[...]
    for lo, hi in FFN_CHUNKS:
        acc = acc + _swiglu_block(h, wg_ref[:, lo:hi], wu_ref[:, lo:hi], wd_ref[lo:hi, :])
    o_ref[...] = acc


def _ffn(x2, h2, wg, wu, wd):
    T = x2.shape[0]
    tm = min(512, T)
    rows = pl.BlockSpec((tm, D_MODEL), lambda t: (t, 0))
    resident = lambda shape: pl.BlockSpec(shape, lambda t: (0, 0), pipeline_mode=pl.Buffered(1))
    return pl.pallas_call(
        _ffn_kernel,
        grid=(T // tm,),
        in_specs=[rows, rows, resident((D_MODEL, FFN_DIM)), resident((D_MODEL, FFN_DIM)),
                  resident((FFN_DIM, D_MODEL))],
        out_specs=rows,
        out_shape=jax.ShapeDtypeStruct((T, D_MODEL), f32),
        compiler_params=_cparams("parallel"),
        name="dense_swiglu",
    )(x2, h2, wg, wu, wd)


MOE_TILE = 256
RUN_ALIGN = SUBLANES
TILE_SLOTS = 2 * MOE_TILE + N_EXPERTS * RUN_ALIGN


def _pow2_pieces(limit):
    out, b = [], RUN_ALIGN
    while b <= limit:
        out.append(b)
        b *= 2
    return tuple(reversed(out))


def _pieces(value, limit):
    sizes = _pow2_pieces(limit)
    for b in sizes:
        done = value & (~(2 * b - 1) & (2 * sizes[0] - 1))
        yield (value & b) != 0, done, b


def _expert_runs(cnt_ref, off_ref, row_ref, tile):
    for e in range(N_EXPERTS):
        o = off_ref[tile * N_EXPERTS + e]
        d = row_ref[tile * N_EXPERTS + e]
        for needed, done, size in _pieces(cnt_ref[tile * N_EXPERTS + e], MOE_TILE):
            yield (needed, pl.multiple_of(o + done, RUN_ALIGN), pl.multiple_of(d + done, RUN_ALIGN),
                   size)


def _dispatch_kernel(cnt_ref, off_ref, row_ref, total_ref, pends_ref, slots_ref, h_ref, xs_ref,
                     xc_scr, zero_scr, sems, zero_sem, *, spare_blocks):
    i = pl.program_id(0)
    n = pl.num_programs(0)
    buf = i % 2

    def zero_fill(start):
        start = pl.multiple_of(start, EXPERT_ROWS)
        return pltpu.make_async_copy(zero_scr, xs_ref.at[pl.ds(start, EXPERT_ROWS)], zero_sem)

    @pl.when(i == 0)
    def _():
        zero_scr[...] = jnp.zeros(zero_scr.shape, f32)
        used = pends_ref[N_EXPERTS - 1]
        fills = [(jnp.maximum(pends_ref[e] - EXPERT_ROWS, 0), None) for e in range(N_EXPERTS)]
        fills += [(used + e * EXPERT_ROWS, used + e * EXPERT_ROWS < xs_ref.shape[0])
                  for e in range(spare_blocks)]
        for action in ("start", "wait"):
            for start, needed in fills:
                if needed is None:
                    getattr(zero_fill(start), action)()
                else:
                    @pl.when(needed)
                    def _():
                        getattr(zero_fill(start), action)()

    def tile_done(tile, b):
        for needed, _, size in _pieces(total_ref[tile], 2 * MOE_TILE):
            @pl.when(needed)
            def _():
                pltpu.make_async_copy(xc_scr.at[b, pl.ds(0, size)], xs_ref.at[pl.ds(0, size)],
                                      sems.at[b]).wait()

    @pl.when(i >= 2)
    def _():
        tile_done(i - 2, buf)

    place = _iota((TILE_SLOTS, MOE_TILE), 0)
    onehot = (place == slots_ref[0, 0:1, :]) | (place == slots_ref[0, 1:2, :])
    xc_scr[buf] = jnp.dot(jnp.where(onehot, 1.0, 0.0).astype(bf16), h_ref[...].astype(bf16),
                          preferred_element_type=f32)
    for needed, local, row, size in _expert_runs(cnt_ref, off_ref, row_ref, i):
        @pl.when(needed)
        def _():
            pltpu.make_async_copy(xc_scr.at[buf, pl.ds(local, size)], xs_ref.at[pl.ds(row, size)],
                                  sems.at[buf]).start()

    @pl.when(i == n - 1)
    def _():
        tile_done(i, buf)

        @pl.when(i >= 1)
        def _():
            tile_done(i - 1, 1 - buf)


def _dispatch(h2, slots_t, cnt, off, row, total, pends, rows):
    T = h2.shape[0]
    grid_spec = pltpu.PrefetchScalarGridSpec(
        num_scalar_prefetch=5,
        grid=(T // MOE_TILE,),
        in_specs=[
            pl.BlockSpec((1, 2, MOE_TILE), lambda t, *_: (t, 0, 0)),
            pl.BlockSpec((MOE_TILE, D_MODEL), lambda t, *_: (t, 0)),
        ],
        out_specs=pl.BlockSpec(memory_space=pl.ANY),
        scratch_shapes=[pltpu.VMEM((2, TILE_SLOTS, D_MODEL), f32),
                        pltpu.VMEM((EXPERT_ROWS, D_MODEL), f32),
                        pltpu.SemaphoreType.DMA((2,)), pltpu.SemaphoreType.DMA(())],
    )
    spare_blocks = rows // EXPERT_ROWS - 2 * T // EXPERT_ROWS
    return pl.pallas_call(
        functools.partial(_dispatch_kernel, spare_blocks=spare_blocks),
        grid_spec=grid_spec,
        out_shape=jax.ShapeDtypeStruct((rows, D_MODEL), f32),
        compiler_params=_cparams("arbitrary"),
        name="moe_dispatch",
    )(cnt, off, row, total, pends, slots_t, h2)


def _expert_kernel(be_ref, nv_ref, xs_ref, wg_ref, wu_ref, wd_ref, ys_ref):
    j = pl.program_id(0)

    @pl.when(j < nv_ref[0])
    def _():
        ys_ref[...] = _swiglu_block(xs_ref[...].astype(bf16), wg_ref[0], wu_ref[0], wd_ref[0])

    @pl.when(j >= nv_ref[0])
    def _():
        ys_ref[...] = jnp.zeros(ys_ref.shape, f32)


def _experts(block_e, n_valid, xs, wg, wu, wd):
    rows = xs.shape[0]
    grid_spec = pltpu.PrefetchScalarGridSpec(
        num_scalar_prefetch=2,
        grid=(rows // EXPERT_ROWS,),
        in_specs=[
            pl.BlockSpec((EXPERT_ROWS, D_MODEL), lambda j, be, nv: (jnp.minimum(j, nv[0] - 1), 0)),
            pl.BlockSpec((1, D_MODEL, EXPERT_DIM), lambda j, be, nv: (be[j], 0, 0)),
            pl.BlockSpec((1, D_MODEL, EXPERT_DIM), lambda j, be, nv: (be[j], 0, 0)),
            pl.BlockSpec((1, EXPERT_DIM, D_MODEL), lambda j, be, nv: (be[j], 0, 0)),
        ],
        out_specs=pl.BlockSpec((EXPERT_ROWS, D_MODEL), lambda j, be, nv: (j, 0)),
    )
    return pl.pallas_call(
        _expert_kernel,
        grid_spec=grid_spec,
        out_shape=jax.ShapeDtypeStruct((rows, D_MODEL), f32),
        compiler_params=_cparams("arbitrary"),
        name="moe_experts",
    )(block_e, n_valid, xs, wg, wu, wd)


def _combine_kernel(cnt_ref, off_ref, row_ref, total_ref, slots_ref, x_ref, route_ref, ys_ref,
                    o_ref, yc_scr, sems):
    i = pl.program_id(0)
    n = pl.num_programs(0)
    buf = i % 2

    def fetch(tile, b):
        for needed, local, row, size in _expert_runs(cnt_ref, off_ref, row_ref, tile):
            @pl.when(needed)
            def _():
                pltpu.make_async_copy(ys_ref.at[pl.ds(row, size)], yc_scr.at[b, pl.ds(local, size)],
                                      sems.at[b]).start()

    @pl.when(i == 0)
    def _():
        yc_scr[...] = jnp.zeros(yc_scr.shape, f32)
        fetch(0, 0)

    @pl.when(i + 1 < n)
    def _():
        fetch(i + 1, 1 - buf)

    for needed, _, size in _pieces(total_ref[i], 2 * MOE_TILE):
        @pl.when(needed)
        def _():
            pltpu.make_async_copy(ys_ref.at[pl.ds(0, size)], yc_scr.at[buf, pl.ds(0, size)],
                                  sems.at[buf]).wait()
    y = yc_scr[buf]
    y_hi = y.astype(bf16)
    y_lo = (y - y_hi.astype(f32)).astype(bf16)
    y_parts = jnp.concatenate([y_hi, y_lo], axis=0)
    place = _iota((MOE_TILE, 2 * TILE_SLOTS), 1)
    route = route_ref[...]
    out = x_ref[...]
    for k, w_lane in ((0, ROUTE_W1), (1, ROUTE_W2)):
        slot = slots_ref[:, k:k + 1]
        pick = jnp.where((place == slot) | (place == slot + TILE_SLOTS), 1.0, 0.0).astype(bf16)
        out = out + route[:, w_lane:w_lane + 1] * jnp.dot(pick, y_parts,
                                                           preferred_element_type=f32)
    o_ref[...] = out


def _combine(x2, route, slots, cnt, off, row, total, ys):
    T = x2.shape[0]
    grid_spec = pltpu.PrefetchScalarGridSpec(
        num_scalar_prefetch=4,
        grid=(T // MOE_TILE,),
        in_specs=[
            pl.BlockSpec((MOE_TILE, 2), lambda t, *_: (t, 0)),
            pl.BlockSpec((MOE_TILE, D_MODEL), lambda t, *_: (t, 0)),
            pl.BlockSpec((MOE_TILE, LANES), lambda t, *_: (t, 0)),
            pl.BlockSpec(memory_space=pl.ANY),
        ],
        out_specs=pl.BlockSpec((MOE_TILE, D_MODEL), lambda t, *_: (t, 0)),
        scratch_shapes=[pltpu.VMEM((2, TILE_SLOTS, D_MODEL), f32), pltpu.SemaphoreType.DMA((2,))],
    )
    return pl.pallas_call(
        _combine_kernel,
        grid_spec=grid_spec,
        out_shape=jax.ShapeDtypeStruct((T, D_MODEL), f32),
        compiler_params=_cparams("arbitrary"),
        name="moe_combine",
    )(cnt, off, row, total, slots, x2, route, ys)


def _moe(x2, h2, route, tile_counts, wg, wu, wd):
    T = x2.shape[0]
    nt = T // MOE_TILE
    cnt = tile_counts.reshape(nt, LANES)[:, :N_EXPERTS].astype(jnp.int32)
    run = (cnt + RUN_ALIGN - 1) // RUN_ALIGN * RUN_ALIGN
    off = jnp.cumsum(run, axis=1) - run
    before = jnp.cumsum(run, axis=0) - run
    padded = (run.sum(axis=0) + EXPERT_ROWS - 1) // EXPERT_ROWS * EXPERT_ROWS
    pends = jnp.cumsum(padded)
    row = (pends - padded)[None, :] + before
    n_blocks = -(-(2 * T + nt * N_EXPERTS * (RUN_ALIGN - 1)) // EXPERT_ROWS) + N_EXPERTS
    block_start = jnp.arange(n_blocks, dtype=jnp.int32) * EXPERT_ROWS
    block_e = jnp.minimum((pends[None, :] <= block_start[:, None]).sum(axis=1),
                          N_EXPERTS - 1).astype(jnp.int32)
    n_valid = (pends[-1:] // EXPERT_ROWS).astype(jnp.int32)
    slots = route[:, ROUTE_S1:ROUTE_S2 + 1].astype(jnp.int32)
    slots_t = slots.reshape(nt, MOE_TILE, 2).transpose(0, 2, 1)
    flat = lambda a: a.reshape(-1).astype(jnp.int32)
    tables = (flat(run), flat(off), flat(row), flat(run.sum(axis=1)))
    xs = _dispatch(h2, slots_t, *tables, pends.astype(jnp.int32), n_blocks * EXPERT_ROWS)
    ys = _experts(block_e, n_valid, xs, wg, wu, wd)
    return _combine(x2, route, slots, *tables, ys)


def _pad_cols(w, n):
    return jnp.pad(w, ((0, 0), (0, n - w.shape[1])))


def _layer_weights(w_in, c_w_uq, c_w_ukv):
    gate_end = N_BRANCH * D_MODEL
    a_end = gate_end + 2 * WIDTH
    b_end = a_end + 3 * WIDTH
    cq = w_in[:, b_end:b_end + C_Q_RANK]
    ckv = w_in[:, b_end + C_Q_RANK:b_end + C_Q_RANK + C_KV_RANK]
    kr = w_in[:, b_end + C_Q_RANK + C_KV_RANK:b_end + C_Q_RANK + C_KV_RANK + C_ROPE_DIM]
    d_start = b_end + C_Q_RANK + C_KV_RANK + C_ROPE_DIM
    kr_block = jnp.pad(kr, ((0, 0), (C_NOPE_DIM, C_HEAD_PAD - C_QK_DIM)))
    w_gate = w_in[:, :gate_end].astype(bf16)
    w = jnp.concatenate([w_in[:, gate_end:b_end], _pad_cols(cq, WIDTH), ckv, kr_block,
                         w_in[:, d_start:]], axis=1).astype(bf16)
    wuq = c_w_uq.reshape(C_Q_RANK, C_HEADS, C_QK_DIM)
    wuq = jnp.pad(wuq, ((0, WIDTH - C_Q_RANK), (0, 0), (0, C_HEAD_PAD - C_QK_DIM)))
    wuq = wuq.reshape(WIDTH, C_HEADS * C_HEAD_PAD).astype(bf16)
    wukv = c_w_ukv.reshape(C_KV_RANK, C_HEADS, C_NOPE_DIM + C_V_DIM)
    wuk = jnp.pad(wukv[:, :, :C_NOPE_DIM], ((0, 0), (0, 0), (0, C_HEAD_PAD - C_NOPE_DIM)))
    wuk = wuk.reshape(C_KV_RANK, C_HEADS * C_HEAD_PAD).astype(bf16)
    wuv = wukv[:, :, C_NOPE_DIM:].reshape(C_KV_RANK, C_HEADS * C_V_DIM).astype(bf16)
    return w_gate, w, wuq, wuk, wuv


def _rotation_constants():
    ch = jnp.arange(WIDTH)
    gsum_b = ((ch[:, None] // B_HEAD_DIM) == (ch[None, :] // B_HEAD_DIM)).astype(f32) / B_HEAD_DIM
    half_b = B_ROT_DIM // 2
    src, dst = ch[:, None], ch[None, :]
    same = (src // B_HEAD_DIM) == (dst // B_HEAD_DIM)
    ds_, dd = src % B_HEAD_DIM, dst % B_HEAD_DIM
    rot_b = jnp.where(same & (dd < half_b) & (ds_ == dd + half_b), -1.0,
                      jnp.where(same & (dd >= half_b) & (dd < B_ROT_DIM) & (ds_ == dd - half_b), 1.0, 0.0))
    lc = jnp.arange(C_HEAD_PAD)
    real = lc < C_QK_DIM
    gsum_c = (real[:, None] & real[None, :]).astype(f32) / C_QK_DIM
    half_c = C_ROPE_DIM // 2
    s, d = lc[:, None], lc[None, :]
    lo = (d >= C_NOPE_DIM) & (d < C_NOPE_DIM + half_c)
    hi = (d >= C_NOPE_DIM + half_c) & (d < C_QK_DIM)
    rot_c = jnp.where(lo & (s == d + half_c), -1.0, jnp.where(hi & (s == d - half_c), 1.0, 0.0))
    slot = jnp.arange(ATT_HEADS * HEAD_SLOT)
    place_b = ((slot[None, :] // HEAD_SLOT == ch[:, None] // B_HEAD_DIM)
               & (slot[None, :] % HEAD_SLOT == ch[:, None] % B_HEAD_DIM))
    return (gsum_b.astype(bf16), rot_b.astype(bf16), place_b.astype(bf16),
            gsum_c.astype(bf16), rot_c.astype(bf16))


def kernel(x, positions, attn_norm, ffn_norm, w_in, a_ln_g, a_ln_b, a_ws, a_bs, b_qn, b_kn, c_cq_norm, c_ckv_norm, c_w_uq, c_w_ukv, c_qn, c_kn, d_conv_w, d_conv_b, d_ln_g, d_ln_b, w_branch, w_out, ffn_w_gate, ffn_w_up, ffn_w_down, moe_router, moe_router_b, moe_w_gate, moe_w_up, moe_w_down):
    B, S, _ = x.shape
    T = B * S
    assert S % ATT_BLOCK == 0 and T % EXPERT_ROWS == 0
    x2 = x.reshape(T, D_MODEL)
    cos_b, sin_b, cos_c, sin_c = _rope_tables(positions)
    gsum_b, rot_b, place_b, gsum_c, rot_c = _rotation_constants()
    row = lambda v: v.reshape(1, -1)

    for l in range(DEPTH):
        w_gate, w, wuq, wuk, wuv = _layer_weights(w_in[l], c_w_uq[l], c_w_ukv[l])
        proj = _inproj(x2, row(attn_norm[l]), w)

        o_a = _sgu(proj, row(a_ln_g[l]), row(a_ln_b[l]), a_ws[l], a_bs[l].T)

        qb, kb, vtb = _moba_prep(proj, B, S, cos_b, sin_b,
                                 row(jnp.tile(b_qn[l], B_HEADS)), row(jnp.tile(b_kn[l], B_HEADS)),
                                 gsum_b, rot_b, place_b)
        o_b = _attention(qb, kb, vtb, B, S, "moba_attention")

        pad_gain = lambda g, n: row(jnp.pad(g, (0, n - g.shape[0])))
        qc, kc, vtc = _mla_prep(proj, B, S, cos_c, sin_c, pad_gain(c_cq_norm[l], WIDTH),
                                row(c_ckv_norm[l]), wuq, wuk, wuv,
                                pad_gain(c_qn[l], C_HEAD_PAD), pad_gain(c_kn[l], C_HEAD_PAD),
                                gsum_c, rot_c)
        o_c = _attention(qc, kc, vtc, B, S, "mla_attention")

        o_d = _conformer(proj, B, S, d_conv_w[l], row(d_conv_b[l]), row(d_ln_g[l]), row(d_ln_b[l]))

        merge_args = (x2, row(attn_norm[l]), w_gate, o_a, o_b, o_c, o_d, w_branch[l].astype(bf16),
                      w_out[l].astype(bf16), row(ffn_norm[l]))
        i = l // 2
        if l % 2 == 0:
            x2, h2 = _merge(*merge_args)
            x2 = _ffn(x2, h2, ffn_w_gate[i].astype(bf16), ffn_w_up[i].astype(bf16),
                      ffn_w_down[i].astype(bf16))
        else:
            router = (_pad_cols(moe_router[i], LANES), _pad_cols(row(moe_router_b[i]), LANES))
            x2, h2, route, tile_counts = _merge(*merge_args, router=router)
            x2 = _moe(x2, h2, route, tile_counts, moe_w_gate[i].astype(bf16),
                      moe_w_up[i].astype(bf16), moe_w_down[i].astype(bf16))
    return x2.reshape(B, S, D_MODEL)
```
